```python
import math
import numpy as np
import jax
import jax.numpy as jnp
from jax import lax

D_MODEL = 1024
BATCH = 4
SEQ = 8192
DEPTH = 4

GRID_W = 64
CTX_LEN = 256
N_MIXERS = 3
N_LAYERS_A = (DEPTH + 2) // N_MIXERS
N_LAYERS_B = (DEPTH + 1) // N_MIXERS
N_LAYERS_C = DEPTH // N_MIXERS

HEAD_DIM = 64
NORM_EPS = 1e-6
A_Q_HEADS = D_MODEL // HEAD_DIM
A_KV_HEADS = A_Q_HEADS // 4
A_WINDOW = 128
A_BLOCK = 128
ROPE_BASE = 10000.0
HY_WIDTH = D_MODEL
HY_ORDER = 2
HY_EMB_DIM = 33
HY_FILTER_DIM = 64
HY_SHORT = 3
HY_FAST_DECAY = 0.3
HY_SLOW_DECAY = 1.5
HY_TARGET = 1e-2
C_HEADS = D_MODEL // HEAD_DIM
NA_ROWS = 8
NA_COLS = 16
N_EXPERTS = 16
EC_CAPACITY = 2
EXPERT_FF = 2 * D_MODEL
F32 = jnp.float32

kernel_name = 'hybrid_diffusion_trunk'


def rms_norm(x, g):
    x32 = x.astype(F32)
    y = x32 * lax.rsqrt(jnp.mean(x32 * x32, axis=-1, keepdims=True) + NORM_EPS)
    return y.astype(x.dtype) * g.astype(x.dtype)


def modulate(x, g, shift, scale):
    return rms_norm(x, g) * (1 + scale) + shift


def axial_rope_tables(n_tok):
    t = jnp.arange(n_tok)
    row = (t // GRID_W).astype(F32)
    col = (t % GRID_W).astype(F32)
    axis_dim = HEAD_DIM // 2
    inv_freq = 1.0 / (ROPE_BASE ** (jnp.arange(0, axis_dim, 2, dtype=F32) / axis_dim))
    ang = jnp.concatenate([row[:, None] * inv_freq, col[:, None] * inv_freq], axis=-1)
    return jnp.cos(ang), jnp.sin(ang)


def _rotate(x, cos, sin):
    m = x.shape[-1] // 2
    x1, x2 = x[..., :m], x[..., m:]
    return jnp.concatenate([x1 * cos - x2 * sin, x2 * cos + x1 * sin], axis=-1)


def apply_axial_rope(x, cos, sin):
    axis_dim = HEAD_DIM // 2
    m = axis_dim // 2
    cos = cos[None, :, None, :].astype(x.dtype)
    sin = sin[None, :, None, :].astype(x.dtype)
    return jnp.concatenate([
        _rotate(x[..., :axis_dim], cos[..., :m], sin[..., :m]),
        _rotate(x[..., axis_dim:], cos[..., m:], sin[..., m:])], axis=-1)


def mixer_window_gqa(u, uc, w_in, w_out, q_g, k_g, sink, need_ctx):
    b, n, _ = u.shape
    n_ctx = uc.shape[1]
    hkv, grp, dh = A_KV_HEADS, A_Q_HEADS // A_KV_HEADS, HEAD_DIM
    qw, kvw = A_Q_HEADS * dh, A_KV_HEADS * dh
    scale = dh ** -0.5
    p = u @ w_in
    q = rms_norm(p[..., :qw].reshape(b, n, A_Q_HEADS, dh), q_g)
    k = rms_norm(p[..., qw:qw + kvw].reshape(b, n, hkv, dh), k_g)
    v = p[..., qw + kvw:].reshape(b, n, hkv, dh)
    kc = rms_norm((uc @ w_in[:, qw:qw + kvw]).reshape(b, n_ctx, hkv, dh), k_g)
    vc = (uc @ w_in[:, qw + kvw:]).reshape(b, n_ctx, hkv, dh)
    cos, sin = axial_rope_tables(n)
    q_rot = apply_axial_rope(q, cos, sin)
    k_rot = apply_axial_rope(k, cos, sin)
    nb = n // A_BLOCK
    span = 3 * A_BLOCK

    def to_blocks(t):
        return t.reshape(b, nb, A_BLOCK, hkv, grp, dh).transpose(1, 0, 2, 3, 4, 5)

    pad = ((0, 0), (A_BLOCK, A_BLOCK), (0, 0), (0, 0))
    k_pad = jnp.pad(k_rot, pad)
    v_pad = jnp.pad(v, pad)
    rel = np.arange(A_BLOCK)[:, None] - (np.arange(span)[None, :] - A_BLOCK)
    band = np.abs(rel) <= A_WINDOW
    sink_hg = sink.astype(F32).reshape(hkv, grp)

    def block(args):
        qb, qb_plain, i = args
        start = i * A_BLOCK
        kw = lax.dynamic_slice_in_dim(k_pad, start, span, axis=1)
        vw = lax.dynamic_slice_in_dim(v_pad, start, span, axis=1)
        key_pos = start - A_BLOCK + jnp.arange(span)
        valid = jnp.asarray(band) & ((key_pos >= 0) & (key_pos < n))[None, :]
        s_loc = jnp.einsum('bqhgd,bkhd->bhgqk', qb, kw).astype(F32) * scale
        s_loc = jnp.where(valid, s_loc, -jnp.inf)
        s_ctx = jnp.einsum('bqhgd,bkhd->bhgqk', qb_plain, kc).astype(F32) * scale
        s_sink = jnp.broadcast_to(sink_hg[None, :, :, None, None], s_loc.shape[:-1] + (1,))
        prob = jax.nn.softmax(jnp.concatenate([s_loc, s_ctx, s_sink], axis=-1), axis=-1)
        p_loc = prob[..., :span].astype(v.dtype)
        p_ctx = prob[..., span:span + n_ctx].astype(v.dtype)
        return (jnp.einsum('bhgqk,bkhd->bqhgd', p_loc, vw)
                + jnp.einsum('bhgqk,bkhd->bqhgd', p_ctx, vc))

    o = lax.map(block, (to_blocks(q_rot), to_blocks(q), jnp.arange(nb)))
    y = o.transpose(1, 0, 2, 3, 4, 5).reshape(b, n, qw) @ w_out
    yc = None
    if need_ctx:
        qc = rms_norm((uc @ w_in[:, :qw]).reshape(b, n_ctx, hkv, grp, dh), q_g)
        s = jnp.einsum('bqhgd,bkhd->bhgqk', qc, kc).astype(F32) * scale
        s_sink = jnp.broadcast_to(sink_hg[None, :, :, None, None], s.shape[:-1] + (1,))
        prob = jax.nn.softmax(jnp.concatenate([s, s_sink], axis=-1), axis=-1)[..., :n_ctx]
        oc = jnp.einsum('bhgqk,bkhd->bqhgd', prob.astype(vc.dtype), vc)
        yc = oc.reshape(b, n_ctx, qw) @ w_out
    return y, yc


def hyena_filters(n, w1, b1, f1, w2, b2, f2, w3):
    t = jnp.linspace(0.0, 1.0, n, dtype=F32)[:, None]
    bands = (HY_EMB_DIM - 1) // 2
    w = 2.0 * math.pi * jnp.arange(n, dtype=F32)[:, None] / n
    f = jnp.linspace(1e-4, bands - 1, bands, dtype=F32)[None, :]
    z = jnp.concatenate([t, jnp.cos(f * w), -jnp.sin(f * w)], axis=-1)
    h = jnp.sin(f1.astype(F32) * (z @ w1.astype(F32) + b1.astype(F32)))
    h = jnp.sin(f2.astype(F32) * (h @ w2.astype(F32) + b2.astype(F32)))
    h = h @ w3.astype(F32)
    max_decay = math.log(HY_TARGET) / HY_FAST_DECAY
    min_decay = math.log(HY_TARGET) / HY_SLOW_DECAY
    deltas = jnp.linspace(min_decay, max_decay, HY_WIDTH, dtype=F32)
    decay = jnp.exp(-t * jnp.abs(deltas)[None, :])
    return h.reshape(n, HY_ORDER, 2, HY_WIDTH) * decay[:, None, None, :]


def two_sided_filter(h_fwd, h_bwd):
    n, d = h_fwd.shape
    k = jnp.concatenate([h_fwd, jnp.zeros((1, d), F32), h_bwd[1:][::-1]], axis=0)
    return k * lax.rsqrt(jnp.sum(k * k, axis=0, keepdims=True) + NORM_EPS)


def long_conv(z, k, bias):
    n = z.shape[1]
    z32 = z.astype(F32)
    zf = jnp.fft.rfft(z32, n=2 * n, axis=1)
    kf = jnp.fft.rfft(k, n=2 * n, axis=0)
    y = jnp.fft.irfft(zf * kf[None], n=2 * n, axis=1)[:, :n]
    return (y + z32 * bias.astype(F32)).astype(z.dtype)


def hyena_sequence(u, w_in, short_w, short_b, w1, b1, f1, w2, b2, f2, w3, fbias, w_out):
    n = u.shape[1]
    p = u @ w_in
    half = HY_SHORT // 2
    pp = jnp.pad(p, ((0, 0), (half, half), (0, 0)))
    p = sum(pp[:, j:j + n] * short_w[j] for j in range(HY_SHORT)) + short_b
    v, x1, x2 = jnp.split(p, 3, axis=-1)
    filt = hyena_filters(n, w1, b1, f1, w2, b2, f2, w3)
    z = v
    for o, gate in enumerate((x1, x2)):
        k = two_sided_filter(filt[:, o, 0], filt[:, o, 1])
        z = gate * long_conv(z, k, fbias[o])
    return z @ w_out


def mixer_hyena(u, uc, w_in, short_w, short_b, w1, b1, f1, w2, b2, f2, w3, fbias, w_out, need_ctx):
    y = hyena_sequence(u, w_in, short_w, short_b, w1, b1, f1, w2, b2, f2, w3, fbias, w_out)
    yc = None
    if need_ctx:
        yc = hyena_sequence(uc, w_in, short_w, short_b, w1, b1, f1, w2, b2, f2, w3, fbias, w_out)
    return y, yc


def mixer_neighbourhood(u, uc, w_in, w_out, q_g, k_g, rpb, need_ctx):
    b, n, _ = u.shape
    n_ctx = uc.shape[1]
    nh, dh = C_HEADS, HEAD_DIM
    hw = nh * dh
    scale = dh ** -0.5
    rows = n // GRID_W
    kr = min(NA_ROWS, rows)
    kc_n = NA_COLS
    p = u @ w_in
    q = rms_norm(p[..., :hw].reshape(b, rows, GRID_W, nh, dh), q_g)
    k = rms_norm(p[..., hw:2 * hw].reshape(b, rows, GRID_W, nh, dh), k_g)
    v = p[..., 2 * hw:].reshape(b, rows, GRID_W, nh, dh)
    ck = rms_norm((uc @ w_in[:, hw:2 * hw]).reshape(b, n_ctx, nh, dh), k_g)
    cv = (uc @ w_in[:, 2 * hw:]).reshape(b, n_ctx, nh, dh)
    col = np.arange(GRID_W)
    col_start = np.clip(col - kc_n // 2, 0, GRID_W - kc_n)
    col_idx = col_start[:, None] + np.arange(kc_n)[None, :]
    col_off = col_idx - col[:, None] + (NA_COLS - 1)
    n_loc = kr * kc_n

    def row_block(args):
        qr, r = args
        r0 = jnp.clip(r - kr // 2, 0, rows - kr)
        k_win = lax.dynamic_slice_in_dim(k, r0, kr, axis=1)[:, :, col_idx]
        v_win = lax.dynamic_slice_in_dim(v, r0, kr, axis=1)[:, :, col_idx]
        row_off = r0 + jnp.arange(kr) - r + (NA_ROWS - 1)
        bias = rpb[:, row_off[:, None, None], col_off[None, :, :]].astype(F32)
        s_loc = (jnp.einsum('bwhd,brwchd->bhwrc', qr, k_win).astype(F32) * scale
                 + bias.transpose(0, 2, 1, 3)[None])
        s_ctx = jnp.einsum('bwhd,bkhd->bhwk', qr, ck).astype(F32) * scale
        prob = jax.nn.softmax(jnp.concatenate([s_loc.reshape(b, nh, GRID_W, n_loc), s_ctx], axis=-1), axis=-1)
        p_loc = prob[..., :n_loc].reshape(b, nh, GRID_W, kr, kc_n).astype(v.dtype)
        p_ctx = prob[..., n_loc:].astype(v.dtype)
        return (jnp.einsum('bhwrc,brwchd->bwhd', p_loc, v_win)
                + jnp.einsum('bhwk,bkhd->bwhd', p_ctx, cv))

    o = lax.map(row_block, (q.transpose(1, 0, 2, 3, 4), jnp.arange(rows)))
    y = o.transpose(1, 0, 2, 3, 4).reshape(b, n, hw) @ w_out
    yc = None
    if need_ctx:
        qc = rms_norm((uc @ w_in[:, :hw]).reshape(b, n_ctx, nh, dh), q_g)
        s = jnp.einsum('bqhd,bkhd->bhqk', qc, ck).astype(F32) * scale
        prob = jax.nn.softmax(s, axis=-1)
        oc = jnp.einsum('bhqk,bkhd->bqhd', prob.astype(cv.dtype), cv)
        yc = oc.reshape(b, n_ctx, hw) @ w_out
    return y, yc


def expert_choice_ffn(u, router_w, w_gate, w_up, w_down):
    n, d = u.shape[1], u.shape[2]
    cap = EC_CAPACITY * n // N_EXPERTS
    aff = jax.nn.softmax((u @ router_w).astype(F32), axis=-1)

    def one_set(args):
        ub, ab = args
        g, idx = lax.top_k(ab.T, cap)
        xs = ub[idx]
        hid = (jax.nn.silu(jnp.einsum('ecd,edf->ecf', xs, w_gate))
               * jnp.einsum('ecd,edf->ecf', xs, w_up))
        ye = jnp.einsum('ecf,efd->ecd', hid, w_down) * g[..., None].astype(ub.dtype)
        return jnp.zeros_like(ub).at[idx.reshape(-1)].add(ye.reshape(-1, d))

    return lax.map(one_set, (u, aff))


def setup_inputs(seed: int = 0) -> dict:
    key = jax.random.key(seed)
    ks = jax.random.split(key, 34)

    def nrm(i, shape, s):
        return jax.random.normal(ks[i], shape, F32) * s

    d = D_MODEL
    qkv_a = (A_Q_HEADS + 2 * A_KV_HEADS) * HEAD_DIM
    qw_a = A_Q_HEADS * HEAD_DIM
    hw_c = C_HEADS * HEAD_DIM
    return {
        'x': nrm(0, (BATCH, SEQ, d), 1.0),
        'c': nrm(1, (BATCH, d), 1.0),
        'ctx': nrm(2, (BATCH, CTX_LEN, d), 1.0),
        'c_ctx': nrm(3, (d,), 1.0),
        'ada_w': nrm(4, (DEPTH, d, 6 * d), 0.5 * d ** -0.5),
        'ada_b': nrm(5, (DEPTH, 6 * d), 0.02),
        'norm_mix_g': 1.0 + nrm(6, (DEPTH, d), 0.05),
        'norm_ffn_g': 1.0 + nrm(7, (DEPTH, d), 0.05),
        'router_w': nrm(8, (DEPTH, d, N_EXPERTS), d ** -0.5),
        'exp_w_gate': nrm(9, (DEPTH, N_EXPERTS, d, EXPERT_FF), d ** -0.5),
        'exp_w_up': nrm(10, (DEPTH, N_EXPERTS, d, EXPERT_FF), d ** -0.5),
        'exp_w_down': nrm(11, (DEPTH, N_EXPERTS, EXPERT_FF, d), EXPERT_FF ** -0.5),
        'a_w_in': nrm(12, (N_LAYERS_A, d, qkv_a), d ** -0.5),
        'a_w_out': nrm(13, (N_LAYERS_A, qw_a, d), qw_a ** -0.5),
        'a_q_g': 1.0 + nrm(14, (N_LAYERS_A, HEAD_DIM), 0.05),
        'a_k_g': 1.0 + nrm(15, (N_LAYERS_A, HEAD_DIM), 0.05),
        'a_sink': nrm(16, (N_LAYERS_A, A_Q_HEADS), 0.5),
        'b_w_in': nrm(17, (N_LAYERS_B, d, 3 * HY_WIDTH), d ** -0.5),
        'b_short_w': nrm(18, (N_LAYERS_B, HY_SHORT, 3 * HY_WIDTH), HY_SHORT ** -0.5),
        'b_short_b': nrm(19, (N_LAYERS_B, 3 * HY_WIDTH), 0.02),
        'b_w1': nrm(20, (N_LAYERS_B, HY_EMB_DIM, HY_FILTER_DIM), HY_EMB_DIM ** -0.5),
        'b_b1': nrm(21, (N_LAYERS_B, HY_FILTER_DIM), 0.1),
        'b_f1': 1.0 + nrm(22, (N_LAYERS_B, HY_FILTER_DIM), 0.1),
        'b_w2': nrm(23, (N_LAYERS_B, HY_FILTER_DIM, HY_FILTER_DIM), HY_FILTER_DIM ** -0.5),
        'b_b2': nrm(24, (N_LAYERS_B, HY_FILTER_DIM), 0.1),
        'b_f2': 1.0 + nrm(25, (N_LAYERS_B, HY_FILTER_DIM), 0.1),
        'b_w3': nrm(26, (N_LAYERS_B, HY_FILTER_DIM, HY_ORDER * 2 * HY_WIDTH), HY_FILTER_DIM ** -0.5),
        'b_bias': nrm(27, (N_LAYERS_B, HY_ORDER, HY_WIDTH), 0.5),
        'b_w_out': nrm(28, (N_LAYERS_B, HY_WIDTH, d), HY_WIDTH ** -0.5),
        'c_w_in': nrm(29, (N_LAYERS_C, d, 3 * hw_c), d ** -0.5),
        'c_w_out': nrm(30, (N_LAYERS_C, hw_c, d), hw_c ** -0.5),
        'c_q_g': 1.0 + nrm(31, (N_LAYERS_C, HEAD_DIM), 0.05),
        'c_k_g': 1.0 + nrm(32, (N_LAYERS_C, HEAD_DIM), 0.05),
        'c_rpb': nrm(33, (N_LAYERS_C, C_HEADS, 2 * NA_ROWS - 1, 2 * NA_COLS - 1), 0.5),
    }


def reference(x, c, ctx, c_ctx, ada_w, ada_b, norm_mix_g, norm_ffn_g, router_w, exp_w_gate, exp_w_up,
              exp_w_down, a_w_in, a_w_out, a_q_g, a_k_g, a_sink, b_w_in, b_short_w, b_short_b, b_w1, b_b1,
              b_f1, b_w2, b_b2, b_f2, b_w3, b_bias, b_w_out, c_w_in, c_w_out, c_q_g, c_k_g, c_rpb):
    h, hc = x, ctx
    silu_c = jax.nn.silu(c)
    silu_cc = jax.nn.silu(c_ctx)
    for i in range(DEPTH):
        last = i == DEPTH - 1
        mod = silu_c @ ada_w[i] + ada_b[i]
        modc = silu_cc @ ada_w[i] + ada_b[i]
        sh1, sc1, g1, sh2, sc2, g2 = jnp.split(mod[:, None, :], 6, axis=-1)
        csh1, csc1, cg1, csh2, csc2, cg2 = jnp.split(modc, 6, axis=-1)
        u = modulate(h, norm_mix_g[i], sh1, sc1)
        uc = modulate(hc, norm_mix_g[i], csh1, csc1)
        kind, j = i % N_MIXERS, i // N_MIXERS
        if kind == 0:
            y, yc = mixer_window_gqa(u, uc, a_w_in[j], a_w_out[j], a_q_g[j], a_k_g[j], a_sink[j], not last)
        elif kind == 1:
            y, yc = mixer_hyena(u, uc, b_w_in[j], b_short_w[j], b_short_b[j], b_w1[j], b_b1[j], b_f1[j],
                                b_w2[j], b_b2[j], b_f2[j], b_w3[j], b_bias[j], b_w_out[j], not last)
        else:
            y, yc = mixer_neighbourhood(u, uc, c_w_in[j], c_w_out[j], c_q_g[j], c_k_g[j], c_rpb[j], not last)
        h = h + g1 * y
        u2 = modulate(h, norm_ffn_g[i], sh2, sc2)
        h = h + g2 * expert_choice_ffn(u2, router_w[i], exp_w_gate[i], exp_w_up[i], exp_w_down[i])
        if not last:
            hc = hc + cg1 * yc
            uc2 = modulate(hc, norm_ffn_g[i], csh2, csc2)
            hc = hc + cg2 * expert_choice_ffn(uc2, router_w[i], exp_w_gate[i], exp_w_up[i], exp_w_down[i])
    return h
```

```python
import functools
import math

import numpy as np
import jax
import jax.numpy as jnp
from jax import lax
from jax.experimental import pallas as pl
from jax.experimental.pallas import tpu as pltpu

F32 = jnp.float32
BF16 = jnp.bfloat16
HI = lax.Precision.HIGHEST

GRID_W = 64
HEAD_DIM = 64
NORM_EPS = 1e-6
N_MIXERS = 3
A_KV_HEADS = 4
A_BLOCK = 128
ROPE_BASE = 10000.0
HY_ORDER = 2
HY_EMB_DIM = 33
HY_SHORT = 3
HY_FAST_DECAY = 0.3
HY_SLOW_DECAY = 1.5
HY_TARGET = 1e-2
NA_ROWS = 8
NA_COLS = 16
C_BLOCK = 256
N_EXPERTS = 16
EC_CAPACITY = 2
NEG = -1e30

LANES = 128
VMEM_LIMIT = 56 * 1024 * 1024


def _cparams(sem):
    return pltpu.CompilerParams(dimension_semantics=sem, vmem_limit_bytes=VMEM_LIMIT)


def _sigmoid(x):
    return 1.0 / (1.0 + jnp.exp(-x))


def _modulate(h, g, shift, scale):
    ms = jnp.mean(h * h, axis=-1, keepdims=True)
    y = h * lax.rsqrt(ms + NORM_EPS)
    return (y * g) * (1.0 + scale) + shift


def _mod_body(s_ref, w_ref, b_ref, o_ref):
    s = s_ref[...]
    s = s * _sigmoid(s)
    o_ref[0] = jnp.dot(s, w_ref[0], precision=HI, preferred_element_type=F32) + b_ref[0]


def _ada_mod(rows, ada_w, ada_b):
    depth, d, n6 = ada_w.shape
    r = rows.shape[0]
    tn = 1536
    return pl.pallas_call(
        _mod_body,
        grid=(depth, n6 // tn),
        in_specs=[pl.BlockSpec((r, d), lambda l, j: (0, 0)),
                  pl.BlockSpec((1, d, tn), lambda l, j: (l, 0, j)),
                  pl.BlockSpec((1, 1, tn), lambda l, j: (l, 0, j))],
        out_specs=pl.BlockSpec((1, r, tn), lambda l, j: (l, 0, j)),
        out_shape=jax.ShapeDtypeStruct((depth, r, n6), F32),
        compiler_params=_cparams(("arbitrary", "arbitrary")),
        name="ada_mod",
    )(rows, ada_w, ada_b.reshape(depth, 1, n6))


def _head_norm(x, gsum, gexp, gain):
    ss = jnp.dot(x * x, gsum, precision=HI, preferred_element_type=F32)
    r = lax.rsqrt(ss * (1.0 / HEAD_DIM) + NORM_EPS)
    rb = jnp.dot(r, gexp, precision=HI, preferred_element_type=F32)
    return (x * rb) * gain


def _rope(x, cos, sin_signed):
    rows, w = x.shape
    lane = lax.broadcasted_iota(jnp.int32, (rows, LANES), 1)
    first = (lane % 32) < 16
    outs = []
    for c in range(w // LANES):
        xc = x[:, c * LANES:(c + 1) * LANES]
        partner = jnp.where(first, pltpu.roll(xc, LANES - 16, 1), pltpu.roll(xc, 16, 1))
        outs.append(xc * cos + partner * sin_signed)
    return jnp.concatenate(outs, axis=1)


def _proj_attn_body(*refs, qw, kvw, rope):
    h_ref, sh_ref, sc_ref, g_ref, w_ref, qg_ref, kg_ref, gsq_ref, geq_ref, gsk_ref, gek_ref = refs[:11]
    u = _modulate(h_ref[0], g_ref[...], sh_ref[0], sc_ref[0])
    p = jnp.dot(u.astype(BF16), w_ref[...], preferred_element_type=F32)
    q = _head_norm(p[:, :qw], gsq_ref[...], geq_ref[...], qg_ref[...]) * (HEAD_DIM ** -0.5)
    k = _head_norm(p[:, qw:qw + kvw], gsk_ref[...], gek_ref[...], kg_ref[...])
    v = p[:, qw + kvw:]
    if rope:
        cos_ref, sin_ref, qr_out, qp_out, kr_out, v_out = refs[11:]
        cos, sin = cos_ref[...], sin_ref[...]
        qr_out[0] = _rope(q, cos, sin).astype(BF16)
        qp_out[0] = q.astype(BF16)
        kr_out[0] = _rope(k, cos, sin).astype(BF16)
        v_out[0] = v.astype(BF16)
    else:
        q_out, k_out, v_out = refs[11:]
        q_out[0] = q.astype(BF16)
        k_out[0] = k.astype(BF16)
        v_out[0] = v.astype(BF16)


def _group_mats(w):
    nh = w // HEAD_DIM
    gs = np.zeros((w, LANES), np.float32)
    gs[np.arange(w), np.arange(w) // HEAD_DIM] = 1.0
    return jnp.asarray(gs), jnp.asarray(gs.T.copy())


def _rope_tables(n):
    t = jnp.arange(n)
    row = (t // GRID_W).astype(F32)
    col = (t % GRID_W).astype(F32)
    axis_dim = HEAD_DIM // 2
    inv_freq = 1.0 / (ROPE_BASE ** (jnp.arange(0, axis_dim, 2, dtype=F32) / axis_dim))
    ang_r = row[:, None] * inv_freq
    ang_c = col[:, None] * inv_freq
    cos64 = jnp.concatenate([jnp.cos(ang_r), jnp.cos(ang_r), jnp.cos(ang_c), jnp.cos(ang_c)], axis=-1)
    sin64 = jnp.concatenate([-jnp.sin(ang_r), jnp.sin(ang_r), -jnp.sin(ang_c), jnp.sin(ang_c)], axis=-1)
    return jnp.tile(cos64, (1, 2)), jnp.tile(sin64, (1, 2))


def _proj_attn(h, shift, scale, g, w_bf, q_g, k_g, qw, kvw, rope, tm):
    b, n, d = h.shape
    nout = w_bf.shape[1]
    tm = min(tm, n)
    gsq, geq = _group_mats(qw)
    gsk, gek = _group_mats(kvw)
    qg = jnp.tile(q_g, qw // HEAD_DIM)[None]
    kg = jnp.tile(k_g, kvw // HEAD_DIM)[None]
    full = lambda a: pl.BlockSpec(a.shape, lambda bi, i: (0,) * a.ndim)
    vec = pl.BlockSpec((1, 1, d), lambda bi, i: (bi, 0, 0))
    args = [h, shift, scale, g[None], w_bf, qg, kg, gsq, geq, gsk, gek]
    in_specs = [pl.BlockSpec((1, tm, d), lambda bi, i: (bi, i, 0)), vec, vec, full(args[3]), full(w_bf),
                full(qg), full(kg), full(gsq), full(geq), full(gsk), full(gek)]
    ospec = lambda w: pl.BlockSpec((1, tm, w), lambda bi, i: (bi, i, 0))
    oshape = lambda w: jax.ShapeDtypeStruct((b, n, w), BF16)
    if rope:
        cos, sin = _rope_tables(n)
        args += [cos, sin]
        in_specs += [pl.BlockSpec((tm, LANES), lambda bi, i: (i, 0))] * 2
        out_specs = [ospec(qw), ospec(qw), ospec(kvw), ospec(kvw)]
        out_shape = [oshape(qw), oshape(qw), oshape(kvw), oshape(kvw)]
    else:
        out_specs = [ospec(qw), ospec(kvw), ospec(kvw)]
        out_shape = [oshape(qw), oshape(kvw), oshape(kvw)]
    return pl.pallas_call(
        functools.partial(_proj_attn_body, qw=qw, kvw=kvw, rope=rope),
        grid=(b, n // tm), in_specs=in_specs, out_specs=out_specs, out_shape=out_shape,
        compiler_params=_cparams(("parallel", "parallel")),
        name="proj_attn_rope" if rope else "proj_attn",
    )(*args)


def _proj_plain_body(h_ref, sh_ref, sc_ref, g_ref, w_ref, o_ref):
    u = _modulate(h_ref[0], g_ref[...], sh_ref[0], sc_ref[0])
    o_ref[0] = jnp.dot(u.astype(BF16), w_ref[...], preferred_element_type=F32)


def _proj_plain(h, shift, scale, g, w_bf, tm):
    b, n, d = h.shape
    nout = w_bf.shape[1]
    tm = min(tm, n)
    vec = pl.BlockSpec((1, 1, d), lambda bi, i: (bi, 0, 0))
    return pl.pallas_call(
        _proj_plain_body,
        grid=(b, n // tm),
        in_specs=[pl.BlockSpec((1, tm, d), lambda bi, i: (bi, i, 0)), vec, vec,
                  pl.BlockSpec((1, d), lambda bi, i: (0, 0)),
                  pl.BlockSpec((d, nout), lambda bi, i: (0, 0))],
        out_specs=pl.BlockSpec((1, tm, nout), lambda bi, i: (bi, i, 0)),
        out_shape=jax.ShapeDtypeStruct((b, n, nout), F32),
        compiler_params=_cparams(("parallel", "parallel")),
        name="proj_plain",
    )(h, shift, scale, g[None], w_bf)


def _local_attn_body(*refs, hkv, grp, tq, use_sink, head_bias):
    (ql_ref, qc_ref, kp_ref, kc_ref, kn_ref, vp_ref, vc_ref, vn_ref, kx_ref, vx_ref, bias_ref) = refs[:11]
    if use_sink:
        sink_ref, o_ref = refs[11:]
    else:
        (o_ref,) = refs[11:]
    dh = HEAD_DIM
    nt = (((1,), (1,)), ((), ()))
    for hh in range(hkv):
        ks = slice(hh * dh, (hh + 1) * dh)
        qcols = [slice((hh * grp + g) * dh, (hh * grp + g + 1) * dh) for g in range(grp)]
        q_l = jnp.concatenate([ql_ref[0, :, c] for c in qcols], axis=0)
        q_c = jnp.concatenate([qc_ref[0, :, c] for c in qcols], axis=0)
        kw = jnp.concatenate([kp_ref[0, :, ks], kc_ref[0, :, ks], kn_ref[0, :, ks]], axis=0)
        vw = jnp.concatenate([vp_ref[0, :, ks], vc_ref[0, :, ks], vn_ref[0, :, ks]], axis=0)
        s_loc = lax.dot_general(q_l, kw, nt, preferred_element_type=F32)
        s_loc = s_loc + bias_ref[0, hh if head_bias else 0]
        s_ctx = lax.dot_general(q_c, kx_ref[0, :, ks], nt, preferred_element_type=F32)
        m = jnp.maximum(jnp.max(s_loc, axis=-1, keepdims=True), jnp.max(s_ctx, axis=-1, keepdims=True))
        if use_sink:
            sk = sink_ref[hh]
            m = jnp.maximum(m, sk)
        e_loc = jnp.exp(s_loc - m)
        e_ctx = jnp.exp(s_ctx - m)
        den = jnp.sum(e_loc, axis=-1, keepdims=True) + jnp.sum(e_ctx, axis=-1, keepdims=True)
        if use_sink:
            den = den + jnp.exp(sk - m)
        inv = 1.0 / den
        o = (jnp.dot((e_loc * inv).astype(BF16), vw, preferred_element_type=F32)
             + jnp.dot((e_ctx * inv).astype(BF16), vx_ref[0, :, ks], preferred_element_type=F32))
        for g in range(grp):
            o_ref[0, :, qcols[g]] = o[g * tq:(g + 1) * tq].astype(BF16)


def _local_attn(q_loc, q_ctx, k, v, kx, vx, bias, sink_rows, hkv, grp, tq):
    b, n, qw = q_loc.shape
    kvw = k.shape[2]
    nb = n // tq
    nctx = kx.shape[1]
    head_bias = bias.shape[1] > 1
    use_sink = sink_rows is not None
    qspec = pl.BlockSpec((1, tq, qw), lambda bi, i: (bi, i, 0))
    prev = pl.BlockSpec((1, tq, kvw), lambda bi, i: (bi, jnp.maximum(i - 1, 0), 0))
    cur = pl.BlockSpec((1, tq, kvw), lambda bi, i: (bi, i, 0))
    nxt = pl.BlockSpec((1, tq, kvw), lambda bi, i: (bi, jnp.minimum(i + 1, nb - 1), 0))
    xspec = pl.BlockSpec((1, nctx, kvw), lambda bi, i: (bi, 0, 0))
    bspec = pl.BlockSpec((1,) + bias.shape[1:],
                         lambda bi, i: (jnp.where(i == 0, 0, jnp.where(i == nb - 1, 2, 1)), 0, 0, 0))
    args = [q_loc, q_ctx, k, k, k, v, v, v, kx, vx, bias]
    in_specs = [qspec, qspec, prev, cur, nxt, prev, cur, nxt, xspec, xspec, bspec]
    if use_sink:
        args.append(sink_rows)
        in_specs.append(pl.BlockSpec(sink_rows.shape, lambda bi, i: (0, 0, 0)))
    return pl.pallas_call(
        functools.partial(_local_attn_body, hkv=hkv, grp=grp, tq=tq, use_sink=use_sink, head_bias=head_bias),
        grid=(b, nb), in_specs=in_specs,
        out_specs=pl.BlockSpec((1, tq, qw), lambda bi, i: (bi, i, 0)),
        out_shape=jax.ShapeDtypeStruct((b, n, qw), BF16),
        compiler_params=_cparams(("parallel", "parallel")),
        name="local_attn_sink" if use_sink else "local_attn",
    )(*args)


def _window_bias(n):
    tq = A_BLOCK
    grp = 4
    qi = np.arange(tq)[:, None]
    kj = np.arange(3 * tq)[None, :] - tq
    band = np.abs(qi - kj) <= A_BLOCK
    nb = n // tq
    out = []
    for which in range(3):
        blk = {0: 0, 1: min(1, nb - 1), 2: nb - 1}[which]
        pos = blk * tq + kj
        ok = band & (pos >= 0) & (pos < n)
        out.append(np.tile(np.where(ok, 0.0, NEG).astype(np.float32), (grp, 1)))
    return jnp.asarray(np.stack(out)[:, None])


def _neighbourhood_bias(rpb, n):
    rows = n // GRID_W
    kr = min(NA_ROWS, rows)
    qa = np.arange(C_BLOCK) // GRID_W
    qc = np.arange(C_BLOCK) % GRID_W
    ka = np.arange(3 * C_BLOCK) // GRID_W - (C_BLOCK // GRID_W)
    kc = np.arange(3 * C_BLOCK) % GRID_W
    c0 = np.clip(qc - NA_COLS // 2, 0, GRID_W - NA_COLS)
    col_ok = (kc[None, :] >= c0[:, None]) & (kc[None, :] < c0[:, None] + NA_COLS)
    row_off = np.clip(ka[None, :] - qa[:, None] + (NA_ROWS - 1), 0, 2 * NA_ROWS - 2)
    col_off = np.clip(kc[None, :] - qc[:, None] + (NA_COLS - 1), 0, 2 * NA_COLS - 2)
    table = rpb.astype(F32)[:, row_off, col_off]
    nb = n // C_BLOCK
    rpb_rows = C_BLOCK // GRID_W
    out = []
    for which in range(3):
        blk = {0: 0, 1: min(1, nb - 1), 2: nb - 1}[which]
        qr = blk * rpb_rows + qa
        kr_abs = blk * rpb_rows + ka
        r0 = np.clip(qr - kr // 2, 0, rows - kr)
        row_ok = (kr_abs[None, :] >= r0[:, None]) & (kr_abs[None, :] < r0[:, None] + kr)
        ok = jnp.asarray(row_ok & col_ok)
        out.append(jnp.where(ok[None], table, NEG))
    return jnp.stack(out)


def _ctx_attn_body(*refs, hkv, grp, use_sink):
    q_ref, k_ref, v_ref = refs[:3]
    if use_sink:
        sink_ref, o_ref = refs[3:]
    else:
        (o_ref,) = refs[3:]
    dh = HEAD_DIM
    nq = q_ref.shape[1]
    nt = (((1,), (1,)), ((), ()))
    for hh in range(hkv):
        ks = slice(hh * dh, (hh + 1) * dh)
        qcols = [slice((hh * grp + g) * dh, (hh * grp + g + 1) * dh) for g in range(grp)]
        q = jnp.concatenate([q_ref[0, :, c] for c in qcols], axis=0)
        s = lax.dot_general(q, k_ref[0, :, ks], nt, preferred_element_type=F32)
        m = jnp.max(s, axis=-1, keepdims=True)
        if use_sink:
            sk = sink_ref[hh]
            m = jnp.maximum(m, sk)
        e = jnp.exp(s - m)
        den = jnp.sum(e, axis=-1, keepdims=True)
        if use_sink:
            den = den + jnp.exp(sk - m)
        o = jnp.dot((e * (1.0 / den)).astype(BF16), v_ref[0, :, ks], preferred_element_type=F32)
        for g in range(grp):
            o_ref[0, :, qcols[g]] = o[g * nq:(g + 1) * nq].astype(BF16)


def _ctx_attn(q, k, v, sink_rows, hkv, grp):
    b, nq, qw = q.shape
    kvw = k.shape[2]
    use_sink = sink_rows is not None
    args = [q, k, v]
    in_specs = [pl.BlockSpec((1, nq, qw), lambda bi: (bi, 0, 0)),
                pl.BlockSpec((1, nq, kvw), lambda bi: (bi, 0, 0)),
                pl.BlockSpec((1, nq, kvw), lambda bi: (bi, 0, 0))]
    if use_sink:
        args.append(sink_rows)
        in_specs.append(pl.BlockSpec(sink_rows.shape, lambda bi: (0, 0, 0)))
    return pl.pallas_call(
        functools.partial(_ctx_attn_body, hkv=hkv, grp=grp, use_sink=use_sink),
        grid=(b,), in_specs=in_specs,
        out_specs=pl.BlockSpec((1, nq, qw), lambda bi: (bi, 0, 0)),
        out_shape=jax.ShapeDtypeStruct((b, nq, qw), BF16),
        compiler_params=_cparams(("parallel",)),
        name="ctx_attn_sink" if use_sink else "ctx_attn",
    )(*args)


def _outproj_body(o_ref, w_ref, h_ref, g1_ref, gn_ref, sh2_ref, sc2_ref, rwt_ref, h1_out, u2_out, aff_out):
    y = jnp.dot(o_ref[0].astype(BF16), w_ref[...], preferred_element_type=F32)
    h1 = h_ref[0] + g1_ref[0] * y
    h1_out[0] = h1
    u2 = _modulate(h1, gn_ref[...], sh2_ref[0], sc2_ref[0])
    u2_out[0] = u2.astype(BF16)
    lt = lax.dot_general(rwt_ref[...], u2, (((1,), (1,)), ((), ())), precision=HI, preferred_element_type=F32)
    e = jnp.exp(lt - jnp.max(lt, axis=0, keepdims=True))
    aff_out[0] = e / jnp.sum(e, axis=0, keepdims=True)


def _outproj(o, w_bf, h, g1, gn, sh2, sc2, rwt, tm):
    b, n, d = h.shape
    kin = o.shape[2]
    ne = rwt.shape[0]
    tm = min(tm, n)
    vec = pl.BlockSpec((1, 1, d), lambda bi, i: (bi, 0, 0))
    return pl.pallas_call(
        _outproj_body,
        grid=(b, n // tm),
        in_specs=[pl.BlockSpec((1, tm, kin), lambda bi, i: (bi, i, 0)),
                  pl.BlockSpec((kin, d), lambda bi, i: (0, 0)),
                  pl.BlockSpec((1, tm, d), lambda bi, i: (bi, i, 0)),
                  vec, pl.BlockSpec((1, d), lambda bi, i: (0, 0)), vec, vec,
                  pl.BlockSpec((ne, d), lambda bi, i: (0, 0))],
        out_specs=[pl.BlockSpec((1, tm, d), lambda bi, i: (bi, i, 0)),
                   pl.BlockSpec((1, tm, d), lambda bi, i: (bi, i, 0)),
                   pl.BlockSpec((1, ne, tm), lambda bi, i: (bi, 0, i))],
        out_shape=[jax.ShapeDtypeStruct((b, n, d), F32),
                   jax.ShapeDtypeStruct((b, n, d), BF16),
                   jax.ShapeDtypeStruct((b, ne, n), F32)],
        compiler_params=_cparams(("parallel", "parallel")),
        name="outproj_router",
    )(o, w_bf, h, g1, gn[None], sh2, sc2, rwt)


def _expert_body(xs_ref, wg_ref, wu_ref, wd_ref, gate_ref, o_ref):
    f = pl.program_id(2)
    xs = xs_ref[0]
    a = jnp.dot(xs, wg_ref[0].astype(BF16), preferred_element_type=F32)
    bb = jnp.dot(xs, wu_ref[0].astype(BF16), preferred_element_type=F32)
    hid = (a * _sigmoid(a)) * bb
    y = jnp.dot(hid.astype(BF16), wd_ref[0].astype(BF16), preferred_element_type=F32)

    @pl.when(f == 0)
    def _():
        o_ref[0] = y

    @pl.when(f > 0)
    def _():
        o_ref[0] += y

    @pl.when(f == pl.num_programs(2) - 1)
    def _():
        o_ref[0] = o_ref[0] * gate_ref[0]


def _expert_ffn(xs, gates, w_gate, w_up, w_down, tr, tf):
    ne, r, d = xs.shape
    ff = w_gate.shape[2]
    return pl.pallas_call(
        _expert_body,
        grid=(ne, r // tr, ff // tf),
        in_specs=[pl.BlockSpec((1, tr, d), lambda e, c, f: (e, c, 0)),
                  pl.BlockSpec((1, d, tf), lambda e, c, f: (e, 0, f)),
                  pl.BlockSpec((1, d, tf), lambda e, c, f: (e, 0, f)),
                  pl.BlockSpec((1, tf, d), lambda e, c, f: (e, f, 0)),
                  pl.BlockSpec((1, tr, 1), lambda e, c, f: (e, c, 0))],
        out_specs=pl.BlockSpec((1, tr, d), lambda e, c, f: (e, c, 0)),
        out_shape=jax.ShapeDtypeStruct((ne, r, d), F32),
        compiler_params=_cparams(("parallel", "parallel", "arbitrary")),
        name="expert_ffn",
    )(xs, w_gate, w_up, w_down, gates)


def _hyena_filters(n, w1, b1, f1, w2, b2, f2, w3, width):
    t = jnp.linspace(0.0, 1.0, n, dtype=F32)[:, None]
    bands = (HY_EMB_DIM - 1) // 2
    w = 2.0 * math.pi * jnp.arange(n, dtype=F32)[:, None] / n
    f = jnp.linspace(1e-4, bands - 1, bands, dtype=F32)[None, :]
    z = jnp.concatenate([t, jnp.cos(f * w), -jnp.sin(f * w)], axis=-1)
    h = jnp.sin(f1.astype(F32) * (jnp.dot(z, w1.astype(F32), precision=HI) + b1.astype(F32)))
    h = jnp.sin(f2.astype(F32) * (jnp.dot(h, w2.astype(F32), precision=HI) + b2.astype(F32)))
    h = jnp.dot(h, w3.astype(F32), precision=HI)
    max_decay = math.log(HY_TARGET) / HY_FAST_DECAY
    min_decay = math.log(HY_TARGET) / HY_SLOW_DECAY
    deltas = jnp.linspace(min_decay, max_decay, width, dtype=F32)
    decay = jnp.exp(-t * jnp.abs(deltas)[None, :])
    return h.reshape(n, HY_ORDER, 2, width) * decay[:, None, None, :]


def _two_sided_filter(h_fwd, h_bwd):
    n, d = h_fwd.shape
    k = jnp.concatenate([h_fwd, jnp.zeros((1, d), F32), h_bwd[1:][::-1]], axis=0)
    return k * lax.rsqrt(jnp.sum(k * k, axis=0, keepdims=True) + NORM_EPS)


def _long_conv(z, k, bias):
    n = z.shape[1]
    zf = jnp.fft.rfft(z, n=2 * n, axis=1)
    kf = jnp.fft.rfft(k, n=2 * n, axis=0)
    y = jnp.fft.irfft(zf * kf[None], n=2 * n, axis=1)[:, :n]
    return y + z * bias.astype(F32)


def _hyena_core(p, short_w, short_b, w1, b1, f1, w2, b2, f2, w3, fbias):
    n = p.shape[1]
    width = p.shape[2] // 3
    half = HY_SHORT // 2
    pp = jnp.pad(p, ((0, 0), (half, half), (0, 0)))
    p = sum(pp[:, j:j + n] * short_w[j] for j in range(HY_SHORT)) + short_b
    v, x1, x2 = jnp.split(p, 3, axis=-1)
    filt = _hyena_filters(n, w1, b1, f1, w2, b2, f2, w3, width)
    z = v
    for o, gate in enumerate((x1, x2)):
        k = _two_sided_filter(filt[:, o, 0], filt[:, o, 1])
        z = gate * _long_conv(z, k, fbias[o])
    return z


def _sink_rows(sink, hkv, grp, tq):
    return jnp.repeat(sink.astype(F32).reshape(hkv, grp), tq, axis=1)[..., None]


def kernel(x, c, ctx, c_ctx, ada_w, ada_b, norm_mix_g, norm_ffn_g, router_w, exp_w_gate, exp_w_up, exp_w_down,
           a_w_in, a_w_out, a_q_g, a_k_g, a_sink, b_w_in, b_short_w, b_short_b, b_w1, b_b1, b_f1, b_w2, b_b2,
           b_f2, b_w3, b_bias, b_w_out, c_w_in, c_w_out, c_q_g, c_k_g, c_rpb):
    bsz, n, d = x.shape
    nctx = ctx.shape[1]
    depth = ada_w.shape[0]
    ne = router_w.shape[2]
    cap = EC_CAPACITY * n // ne
    cap_c = EC_CAPACITY * nctx // ne
    tm = 256

    pad_rows = (-(bsz + 1)) % 8
    rows = jnp.concatenate([c, c_ctx[None], jnp.zeros((pad_rows, d), F32)], axis=0)
    mod_all = _ada_mod(rows, ada_w, ada_b)

    h, hc = x, ctx
    for i in range(depth):
        last = i == depth - 1
        kind, j = i % N_MIXERS, i // N_MIXERS
        mod = mod_all[i, :bsz].reshape(bsz, 1, 6, d)
        sh1, sc1, g1, sh2, sc2, g2 = [mod[:, :, t] for t in range(6)]
        modc = jnp.broadcast_to(mod_all[i, bsz].reshape(1, 1, 6, d), (bsz, 1, 6, d))
        csh1, csc1, cg1, csh2, csc2, cg2 = [modc[:, :, t] for t in range(6)]
        yc = None
        if kind == 0:
            hkv, grp = A_KV_HEADS, a_w_in.shape[2] // HEAD_DIM // A_KV_HEADS - 2
            qw, kvw = hkv * grp * HEAD_DIM, hkv * HEAD_DIM
            w_in = a_w_in[j].astype(BF16)
            w_out = a_w_out[j].astype(BF16)
            q_rot, q_pl, k_rot, v = _proj_attn(h, sh1, sc1, norm_mix_g[i], w_in, a_q_g[j], a_k_g[j],
                                               qw, kvw, True, tm)
            qc, kc, vc = _proj_attn(hc, csh1, csc1, norm_mix_g[i], w_in, a_q_g[j], a_k_g[j],
                                    qw, kvw, False, tm)
            o = _local_attn(q_rot, q_pl, k_rot, v, kc, vc, _window_bias(n),
                            _sink_rows(a_sink[j], hkv, grp, A_BLOCK), hkv, grp, A_BLOCK)
            if not last:
                yc = _ctx_attn(qc, kc, vc, _sink_rows(a_sink[j], hkv, grp, nctx), hkv, grp)
        elif kind == 1:
            w_in = b_w_in[j].astype(BF16)
            w_out = b_w_out[j].astype(BF16)
            hy = (b_short_w[j], b_short_b[j], b_w1[j], b_b1[j], b_f1[j], b_w2[j], b_b2[j], b_f2[j], b_w3[j],
                  b_bias[j])
            o = _hyena_core(_proj_plain(h, sh1, sc1, norm_mix_g[i], w_in, tm), *hy)
            if not last:
                yc = _hyena_core(_proj_plain(hc, csh1, csc1, norm_mix_g[i], w_in, tm), *hy)
        else:
            nh = c_w_in.shape[2] // HEAD_DIM // 3
            hw = nh * HEAD_DIM
            w_in = c_w_in[j].astype(BF16)
            w_out = c_w_out[j].astype(BF16)
            q, k, v = _proj_attn(h, sh1, sc1, norm_mix_g[i], w_in, c_q_g[j], c_k_g[j], hw, hw, False, tm)
            qc, kc, vc = _proj_attn(hc, csh1, csc1, norm_mix_g[i], w_in, c_q_g[j], c_k_g[j], hw, hw, False, tm)
            o = _local_attn(q, q, k, v, kc, vc, _neighbourhood_bias(c_rpb[j], n), None, nh, 1, C_BLOCK)
            if not last:
                yc = _ctx_attn(qc, kc, vc, None, nh, 1)

        rwt = router_w[i].T
        h1, u2, aff = _outproj(o, w_out, h, g1, norm_ffn_g[i], sh2, sc2, rwt, tm)
        gl, il = lax.top_k(aff, cap)
        rows_l = il + (jnp.arange(bsz) * n)[:, None, None]
        rows_all = rows_l.transpose(1, 0, 2).reshape(ne, bsz * cap)
        gates_all = gl.transpose(1, 0, 2).reshape(ne, bsz * cap)
        u_all = u2.reshape(bsz * n, d)
        if not last:
            hc1, uc2, affc = _outproj(yc, w_out, hc, cg1, norm_ffn_g[i], csh2, csc2, rwt, tm)
            gcx, icx = lax.top_k(affc, cap_c)
            rows_c = icx + (bsz * n + jnp.arange(bsz) * nctx)[:, None, None]
            rows_all = jnp.concatenate([rows_all, rows_c.transpose(1, 0, 2).reshape(ne, bsz * cap_c)], axis=1)
            gates_all = jnp.concatenate([gates_all, gcx.transpose(1, 0, 2).reshape(ne, bsz * cap_c)], axis=1)
            u_all = jnp.concatenate([u_all, uc2.reshape(bsz * nctx, d)], axis=0)
        r = rows_all.shape[1]
        xs = u_all[rows_all]
        ye = _expert_ffn(xs, gates_all[..., None], exp_w_gate[i], exp_w_up[i], exp_w_down[i], r // 4, 512)
        moe = jnp.zeros((u_all.shape[0], d), F32).at[rows_all.reshape(-1)].add(ye.reshape(-1, d))
        h = h1 + g2 * moe[:bsz * n].reshape(bsz, n, d)
        if not last:
            hc = hc1 + cg2 * moe[bsz * n:].reshape(bsz, nctx, d)
    return h
```

```python
import functools
import math

import numpy as np
import jax
import jax.numpy as jnp
from jax import lax
from jax.experimental import pallas as pl
from jax.experimental.pallas import tpu as pltpu

F32 = jnp.float32
BF16 = jnp.bfloat16
HI = lax.Precision.HIGHEST

GRID_W = 64
HEAD_DIM = 64
NORM_EPS = 1e-6
N_MIXERS = 3
A_KV_HEADS = 4
A_BLOCK = 128
ROPE_BASE = 10000.0
HY_ORDER = 2
HY_EMB_DIM = 33
HY_SHORT = 3
HY_FAST_DECAY = 0.3
HY_SLOW_DECAY = 1.5
HY_TARGET = 1e-2
NA_ROWS = 8
NA_COLS = 16
C_BLOCK = 256
N_EXPERTS = 16
EC_CAPACITY = 2
NEG = -1e30

LANES = 128
VMEM_LIMIT = 56 * 1024 * 1024


def _cparams(sem):
    return pltpu.CompilerParams(dimension_semantics=sem, vmem_limit_bytes=VMEM_LIMIT)


def _sigmoid(x):
    return 1.0 / (1.0 + jnp.exp(-x))


def _modulate(h, g, shift, scale):
    ms = jnp.mean(h * h, axis=-1, keepdims=True)
    y = h * lax.rsqrt(ms + NORM_EPS)
    return (y * g) * (1.0 + scale) + shift


def _mod_body(s_ref, w_ref, b_ref, o_ref):
    s = s_ref[...]
    s = s * _sigmoid(s)
    o_ref[0] = jnp.dot(s, w_ref[0], precision=HI, preferred_element_type=F32) + b_ref[0]


def _ada_mod(rows, ada_w, ada_b):
    depth, d, n6 = ada_w.shape
    r = rows.shape[0]
    tn = 1536
    return pl.pallas_call(
        _mod_body,
        grid=(depth, n6 // tn),
        in_specs=[pl.BlockSpec((r, d), lambda l, j: (0, 0)),
                  pl.BlockSpec((1, d, tn), lambda l, j: (l, 0, j)),
                  pl.BlockSpec((1, 1, tn), lambda l, j: (l, 0, j))],
        out_specs=pl.BlockSpec((1, r, tn), lambda l, j: (l, 0, j)),
        out_shape=jax.ShapeDtypeStruct((depth, r, n6), F32),
        compiler_params=_cparams(("arbitrary", "arbitrary")),
        name="ada_mod",
    )(rows, ada_w, ada_b.reshape(depth, 1, n6))


def _head_norm(x, gsum, gexp, gain):
    ss = jnp.dot(x * x, gsum, precision=HI, preferred_element_type=F32)
    r = lax.rsqrt(ss * (1.0 / HEAD_DIM) + NORM_EPS)
    rb = jnp.dot(r, gexp, precision=HI, preferred_element_type=F32)
    return (x * rb) * gain


def _rope(x, cos, sin_signed):
    rows, w = x.shape
    lane = lax.broadcasted_iota(jnp.int32, (rows, LANES), 1)
    first = (lane % 32) < 16
    outs = []
    for c in range(w // LANES):
        xc = x[:, c * LANES:(c + 1) * LANES]
        partner = jnp.where(first, pltpu.roll(xc, LANES - 16, 1), pltpu.roll(xc, 16, 1))
        outs.append(xc * cos + partner * sin_signed)
    return jnp.concatenate(outs, axis=1)


def _proj_attn_body(*refs, qw, kvw, rope):
    h_ref, sh_ref, sc_ref, g_ref, w_ref, qg_ref, kg_ref, gsq_ref, geq_ref, gsk_ref, gek_ref = refs[:11]
    u = _modulate(h_ref[0], g_ref[...], sh_ref[0], sc_ref[0])
    p = jnp.dot(u.astype(BF16), w_ref[...], preferred_element_type=F32)
    q = _head_norm(p[:, :qw], gsq_ref[...], geq_ref[...], qg_ref[...]) * (HEAD_DIM ** -0.5)
    k = _head_norm(p[:, qw:qw + kvw], gsk_ref[...], gek_ref[...], kg_ref[...])
    v = p[:, qw + kvw:]
    if rope:
        cos_ref, sin_ref, qr_out, qp_out, kr_out, v_out = refs[11:]
        cos, sin = cos_ref[...], sin_ref[...]
        qr_out[0] = _rope(q, cos, sin).astype(BF16)
        qp_out[0] = q.astype(BF16)
        kr_out[0] = _rope(k, cos, sin).astype(BF16)
        v_out[0] = v.astype(BF16)
    else:
        q_out, k_out, v_out = refs[11:]
        q_out[0] = q.astype(BF16)
        k_out[0] = k.astype(BF16)
        v_out[0] = v.astype(BF16)


def _group_mats(w):
    nh = w // HEAD_DIM
    gs = np.zeros((w, LANES), np.float32)
    gs[np.arange(w), np.arange(w) // HEAD_DIM] = 1.0
    return jnp.asarray(gs), jnp.asarray(gs.T.copy())


def _rope_tables(n):
    t = jnp.arange(n)
    row = (t // GRID_W).astype(F32)
    col = (t % GRID_W).astype(F32)
    axis_dim = HEAD_DIM // 2
    inv_freq = 1.0 / (ROPE_BASE ** (jnp.arange(0, axis_dim, 2, dtype=F32) / axis_dim))
    ang_r = row[:, None] * inv_freq
    ang_c = col[:, None] * inv_freq
    cos64 = jnp.concatenate([jnp.cos(ang_r), jnp.cos(ang_r), jnp.cos(ang_c), jnp.cos(ang_c)], axis=-1)
    sin64 = jnp.concatenate([-jnp.sin(ang_r), jnp.sin(ang_r), -jnp.sin(ang_c), jnp.sin(ang_c)], axis=-1)
    return jnp.tile(cos64, (1, 2)), jnp.tile(sin64, (1, 2))


def _proj_attn(h, shift, scale, g, w_bf, q_g, k_g, qw, kvw, rope, tm):
    b, n, d = h.shape
    nout = w_bf.shape[1]
    tm = min(tm, n)
    gsq, geq = _group_mats(qw)
    gsk, gek = _group_mats(kvw)
    qg = jnp.tile(q_g, qw // HEAD_DIM)[None]
    kg = jnp.tile(k_g, kvw // HEAD_DIM)[None]
    full = lambda a: pl.BlockSpec(a.shape, lambda bi, i: (0,) * a.ndim)
    vec = pl.BlockSpec((1, 1, d), lambda bi, i: (bi, 0, 0))
    args = [h, shift, scale, g[None], w_bf, qg, kg, gsq, geq, gsk, gek]
    in_specs = [pl.BlockSpec((1, tm, d), lambda bi, i: (bi, i, 0)), vec, vec, full(args[3]), full(w_bf),
                full(qg), full(kg), full(gsq), full(geq), full(gsk), full(gek)]
    ospec = lambda w: pl.BlockSpec((1, tm, w), lambda bi, i: (bi, i, 0))
    oshape = lambda w: jax.ShapeDtypeStruct((b, n, w), BF16)
    if rope:
        cos, sin = _rope_tables(n)
        args += [cos, sin]
        in_specs += [pl.BlockSpec((tm, LANES), lambda bi, i: (i, 0))] * 2
        out_specs = [ospec(qw), ospec(qw), ospec(kvw), ospec(kvw)]
        out_shape = [oshape(qw), oshape(qw), oshape(kvw), oshape(kvw)]
    else:
        out_specs = [ospec(qw), ospec(kvw), ospec(kvw)]
        out_shape = [oshape(qw), oshape(kvw), oshape(kvw)]
    return pl.pallas_call(
        functools.partial(_proj_attn_body, qw=qw, kvw=kvw, rope=rope),
        grid=(b, n // tm), in_specs=in_specs, out_specs=out_specs, out_shape=out_shape,
        compiler_params=_cparams(("parallel", "parallel")),
        name="proj_attn_rope" if rope else "proj_attn",
    )(*args)


def _proj_plain_body(h_ref, sh_ref, sc_ref, g_ref, w_ref, o_ref):
    u = _modulate(h_ref[0], g_ref[...], sh_ref[0], sc_ref[0])
    o_ref[0] = jnp.dot(u.astype(BF16), w_ref[...], preferred_element_type=F32)


def _proj_plain(h, shift, scale, g, w_bf, tm):
    b, n, d = h.shape
    nout = w_bf.shape[1]
    tm = min(tm, n)
    vec = pl.BlockSpec((1, 1, d), lambda bi, i: (bi, 0, 0))
    return pl.pallas_call(
        _proj_plain_body,
        grid=(b, n // tm),
        in_specs=[pl.BlockSpec((1, tm, d), lambda bi, i: (bi, i, 0)), vec, vec,
                  pl.BlockSpec((1, d), lambda bi, i: (0, 0)),
                  pl.BlockSpec((d, nout), lambda bi, i: (0, 0))],
        out_specs=pl.BlockSpec((1, tm, nout), lambda bi, i: (bi, i, 0)),
        out_shape=jax.ShapeDtypeStruct((b, n, nout), F32),
        compiler_params=_cparams(("parallel", "parallel")),
        name="proj_plain",
    )(h, shift, scale, g[None], w_bf)


def _local_attn_body(*refs, hkv, grp, tq, use_sink, head_bias):
    (ql_ref, qc_ref, kp_ref, kc_ref, kn_ref, vp_ref, vc_ref, vn_ref, kx_ref, vx_ref, bias_ref) = refs[:11]
    if use_sink:
        sink_ref, o_ref = refs[11:]
    else:
        (o_ref,) = refs[11:]
    dh = HEAD_DIM
    nt = (((1,), (1,)), ((), ()))
    for hh in range(hkv):
        ks = slice(hh * dh, (hh + 1) * dh)
        qcols = [slice((hh * grp + g) * dh, (hh * grp + g + 1) * dh) for g in range(grp)]
        q_l = jnp.concatenate([ql_ref[0, :, c] for c in qcols], axis=0)
        q_c = jnp.concatenate([qc_ref[0, :, c] for c in qcols], axis=0)
        kw = jnp.concatenate([kp_ref[0, :, ks], kc_ref[0, :, ks], kn_ref[0, :, ks]], axis=0)
        vw = jnp.concatenate([vp_ref[0, :, ks], vc_ref[0, :, ks], vn_ref[0, :, ks]], axis=0)
        s_loc = lax.dot_general(q_l, kw, nt, preferred_element_type=F32)
        s_loc = s_loc + bias_ref[0, hh if head_bias else 0]
        s_ctx = lax.dot_general(q_c, kx_ref[0, :, ks], nt, preferred_element_type=F32)
        m = jnp.maximum(jnp.max(s_loc, axis=-1, keepdims=True), jnp.max(s_ctx, axis=-1, keepdims=True))
        if use_sink:
            sk = sink_ref[hh]
            m = jnp.maximum(m, sk)
        e_loc = jnp.exp(s_loc - m)
        e_ctx = jnp.exp(s_ctx - m)
        den = jnp.sum(e_loc, axis=-1, keepdims=True) + jnp.sum(e_ctx, axis=-1, keepdims=True)
        if use_sink:
            den = den + jnp.exp(sk - m)
        inv = 1.0 / den
        o = (jnp.dot((e_loc * inv).astype(BF16), vw, preferred_element_type=F32)
             + jnp.dot((e_ctx * inv).astype(BF16), vx_ref[0, :, ks], preferred_element_type=F32))
        for g in range(grp):
            o_ref[0, :, qcols[g]] = o[g * tq:(g + 1) * tq].astype(BF16)


def _local_attn(q_loc, q_ctx, k, v, kx, vx, bias, sink_rows, hkv, grp, tq):
    b, n, qw = q_loc.shape
    kvw = k.shape[2]
    nb = n // tq
    nctx = kx.shape[1]
    head_bias = bias.shape[1] > 1
    use_sink = sink_rows is not None
    qspec = pl.BlockSpec((1, tq, qw), lambda bi, i: (bi, i, 0))
    prev = pl.BlockSpec((1, tq, kvw), lambda bi, i: (bi, jnp.maximum(i - 1, 0), 0))
    cur = pl.BlockSpec((1, tq, kvw), lambda bi, i: (bi, i, 0))
    nxt = pl.BlockSpec((1, tq, kvw), lambda bi, i: (bi, jnp.minimum(i + 1, nb - 1), 0))
    xspec = pl.BlockSpec((1, nctx, kvw), lambda bi, i: (bi, 0, 0))
    bspec = pl.BlockSpec((1,) + bias.shape[1:],
                         lambda bi, i: (jnp.where(i == 0, 0, jnp.where(i == nb - 1, 2, 1)), 0, 0, 0))
    args = [q_loc, q_ctx, k, k, k, v, v, v, kx, vx, bias]
    in_specs = [qspec, qspec, prev, cur, nxt, prev, cur, nxt, xspec, xspec, bspec]
    if use_sink:
        args.append(sink_rows)
        in_specs.append(pl.BlockSpec(sink_rows.shape, lambda bi, i: (0, 0, 0)))
    return pl.pallas_call(
        functools.partial(_local_attn_body, hkv=hkv, grp=grp, tq=tq, use_sink=use_sink, head_bias=head_bias),
        grid=(b, nb), in_specs=in_specs,
        out_specs=pl.BlockSpec((1, tq, qw), lambda bi, i: (bi, i, 0)),
        out_shape=jax.ShapeDtypeStruct((b, n, qw), BF16),
        compiler_params=_cparams(("parallel", "parallel")),
        name="local_attn_sink" if use_sink else "local_attn",
    )(*args)


def _window_bias(n):
    tq = A_BLOCK
    grp = 4
    qi = np.arange(tq)[:, None]
    kj = np.arange(3 * tq)[None, :] - tq
    band = np.abs(qi - kj) <= A_BLOCK
    nb = n // tq
    out = []
    for which in range(3):
        blk = {0: 0, 1: min(1, nb - 1), 2: nb - 1}[which]
        pos = blk * tq + kj
        ok = band & (pos >= 0) & (pos < n)
        out.append(np.tile(np.where(ok, 0.0, NEG).astype(np.float32), (grp, 1)))
    return jnp.asarray(np.stack(out)[:, None])


def _neighbourhood_bias(rpb, n):
    rows = n // GRID_W
    kr = min(NA_ROWS, rows)
    qa = np.arange(C_BLOCK) // GRID_W
    qc = np.arange(C_BLOCK) % GRID_W
    ka = np.arange(3 * C_BLOCK) // GRID_W - (C_BLOCK // GRID_W)
    kc = np.arange(3 * C_BLOCK) % GRID_W
    c0 = np.clip(qc - NA_COLS // 2, 0, GRID_W - NA_COLS)
    col_ok = (kc[None, :] >= c0[:, None]) & (kc[None, :] < c0[:, None] + NA_COLS)
    row_off = np.clip(ka[None, :] - qa[:, None] + (NA_ROWS - 1), 0, 2 * NA_ROWS - 2)
    col_off = np.clip(kc[None, :] - qc[:, None] + (NA_COLS - 1), 0, 2 * NA_COLS - 2)
    table = rpb.astype(F32)[:, row_off, col_off]
    nb = n // C_BLOCK
    rpb_rows = C_BLOCK // GRID_W
    out = []
    for which in range(3):
        blk = {0: 0, 1: min(1, nb - 1), 2: nb - 1}[which]
        qr = blk * rpb_rows + qa
        kr_abs = blk * rpb_rows + ka
        r0 = np.clip(qr - kr // 2, 0, rows - kr)
        row_ok = (kr_abs[None, :] >= r0[:, None]) & (kr_abs[None, :] < r0[:, None] + kr)
        ok = jnp.asarray(row_ok & col_ok)
        out.append(jnp.where(ok[None], table, NEG))
    return jnp.stack(out)


def _ctx_attn_body(*refs, hkv, grp, use_sink):
    q_ref, k_ref, v_ref = refs[:3]
    if use_sink:
        sink_ref, o_ref = refs[3:]
    else:
        (o_ref,) = refs[3:]
    dh = HEAD_DIM
    nq = q_ref.shape[1]
    nt = (((1,), (1,)), ((), ()))
    for hh in range(hkv):
        ks = slice(hh * dh, (hh + 1) * dh)
        qcols = [slice((hh * grp + g) * dh, (hh * grp + g + 1) * dh) for g in range(grp)]
        q = jnp.concatenate([q_ref[0, :, c] for c in qcols], axis=0)
        s = lax.dot_general(q, k_ref[0, :, ks], nt, preferred_element_type=F32)
        m = jnp.max(s, axis=-1, keepdims=True)
        if use_sink:
            sk = sink_ref[hh]
            m = jnp.maximum(m, sk)
        e = jnp.exp(s - m)
        den = jnp.sum(e, axis=-1, keepdims=True)
        if use_sink:
            den = den + jnp.exp(sk - m)
        o = jnp.dot((e * (1.0 / den)).astype(BF16), v_ref[0, :, ks], preferred_element_type=F32)
        for g in range(grp):
            o_ref[0, :, qcols[g]] = o[g * nq:(g + 1) * nq].astype(BF16)


def _ctx_attn(q, k, v, sink_rows, hkv, grp):
    b, nq, qw = q.shape
    kvw = k.shape[2]
    use_sink = sink_rows is not None
    args = [q, k, v]
    in_specs = [pl.BlockSpec((1, nq, qw), lambda bi: (bi, 0, 0)),
                pl.BlockSpec((1, nq, kvw), lambda bi: (bi, 0, 0)),
                pl.BlockSpec((1, nq, kvw), lambda bi: (bi, 0, 0))]
    if use_sink:
        args.append(sink_rows)
        in_specs.append(pl.BlockSpec(sink_rows.shape, lambda bi: (0, 0, 0)))
    return pl.pallas_call(
        functools.partial(_ctx_attn_body, hkv=hkv, grp=grp, use_sink=use_sink),
        grid=(b,), in_specs=in_specs,
        out_specs=pl.BlockSpec((1, nq, qw), lambda bi: (bi, 0, 0)),
        out_shape=jax.ShapeDtypeStruct((b, nq, qw), BF16),
        compiler_params=_cparams(("parallel",)),
        name="ctx_attn_sink" if use_sink else "ctx_attn",
    )(*args)


def _outproj_body(*refs, gated):
    if gated:
        o_ref, og_ref, w_ref, h_ref, g1_ref, gn_ref, sh2_ref, sc2_ref, rwt_ref, h1_out, u2_out, aff_out = refs
        o = o_ref[0] * og_ref[0]
    else:
        o_ref, w_ref, h_ref, g1_ref, gn_ref, sh2_ref, sc2_ref, rwt_ref, h1_out, u2_out, aff_out = refs
        o = o_ref[0]
    y = jnp.dot(o.astype(BF16), w_ref[...], preferred_element_type=F32)
    h1 = h_ref[0] + g1_ref[0] * y
    h1_out[0] = h1
    u2 = _modulate(h1, gn_ref[...], sh2_ref[0], sc2_ref[0])
    u2_out[0] = u2.astype(BF16)
    lt = lax.dot_general(rwt_ref[...], u2, (((1,), (1,)), ((), ())), precision=HI, preferred_element_type=F32)
    e = jnp.exp(lt - jnp.max(lt, axis=0, keepdims=True))
    aff_out[0] = e / jnp.sum(e, axis=0, keepdims=True)


def _outproj(o, w_bf, h, g1, gn, sh2, sc2, rwt, tm, o_gate=None):
    b, n, d = h.shape
    kin = o.shape[2]
    ne = rwt.shape[0]
    tm = min(tm, n)
    vec = pl.BlockSpec((1, 1, d), lambda bi, i: (bi, 0, 0))
    ospec = pl.BlockSpec((1, tm, kin), lambda bi, i: (bi, i, 0))
    gated = o_gate is not None
    return pl.pallas_call(
        functools.partial(_outproj_body, gated=gated),
        grid=(b, n // tm),
        in_specs=([ospec, ospec] if gated else [ospec]) + [
                  pl.BlockSpec((kin, d), lambda bi, i: (0, 0)),
                  pl.BlockSpec((1, tm, d), lambda bi, i: (bi, i, 0)),
                  vec, pl.BlockSpec((1, d), lambda bi, i: (0, 0)), vec, vec,
                  pl.BlockSpec((ne, d), lambda bi, i: (0, 0))],
        out_specs=[pl.BlockSpec((1, tm, d), lambda bi, i: (bi, i, 0)),
                   pl.BlockSpec((1, tm, d), lambda bi, i: (bi, i, 0)),
                   pl.BlockSpec((1, ne, tm), lambda bi, i: (bi, 0, i))],
        out_shape=[jax.ShapeDtypeStruct((b, n, d), F32),
                   jax.ShapeDtypeStruct((b, n, d), BF16),
                   jax.ShapeDtypeStruct((b, ne, n), F32)],
        compiler_params=_cparams(("parallel", "parallel")),
        name="outproj_router",
    )(*((o, o_gate) if gated else (o,)), w_bf, h, g1, gn[None], sh2, sc2, rwt)


def _expert_body(xs_ref, wg_ref, wu_ref, wd_ref, gate_ref, o_ref):
    f = pl.program_id(2)
    xs = xs_ref[0]
    a = jnp.dot(xs, wg_ref[0].astype(BF16), preferred_element_type=F32)
    bb = jnp.dot(xs, wu_ref[0].astype(BF16), preferred_element_type=F32)
    hid = (a * _sigmoid(a)) * bb
    y = jnp.dot(hid.astype(BF16), wd_ref[0].astype(BF16), preferred_element_type=F32)

    @pl.when(f == 0)
    def _():
        o_ref[0] = y

    @pl.when(f > 0)
    def _():
        o_ref[0] += y

    @pl.when(f == pl.num_programs(2) - 1)
    def _():
        o_ref[0] = o_ref[0] * gate_ref[0]


def _expert_ffn(xs, gates, w_gate, w_up, w_down, tr, tf):
    ne, r, d = xs.shape
    ff = w_gate.shape[2]
    return pl.pallas_call(
        _expert_body,
        grid=(ne, r // tr, ff // tf),
        in_specs=[pl.BlockSpec((1, tr, d), lambda e, c, f: (e, c, 0)),
                  pl.BlockSpec((1, d, tf), lambda e, c, f: (e, 0, f)),
                  pl.BlockSpec((1, d, tf), lambda e, c, f: (e, 0, f)),
                  pl.BlockSpec((1, tf, d), lambda e, c, f: (e, f, 0)),
                  pl.BlockSpec((1, tr, 1), lambda e, c, f: (e, c, 0))],
        out_specs=pl.BlockSpec((1, tr, d), lambda e, c, f: (e, c, 0)),
        out_shape=jax.ShapeDtypeStruct((ne, r, d), F32),
        compiler_params=_cparams(("parallel", "parallel", "arbitrary")),
        name="expert_ffn",
    )(xs, w_gate, w_up, w_down, gates)


FFT_SLAB = 128


def _stack_complex(m):
    return np.block([[m.real, -m.imag], [m.imag, m.real]])


def _dft_consts(n_total, n_nonzero):
    n1 = n_total // FFT_SLAB
    na = n_nonzero // FFT_SLAB
    idx1 = np.arange(n1)
    f1 = np.exp(-2j * np.pi * np.outer(idx1, idx1) / n1)
    idx2 = np.arange(FFT_SLAB)
    f2 = np.exp(-2j * np.pi * np.outer(idx2, idx2) / FFT_SLAB)
    tw = np.exp(-2j * np.pi * np.outer(idx1, idx2) / n_total)
    return dict(
        m1=_stack_complex(f1[:, :na]),
        m1_real=np.concatenate([f1.real, f1.imag], axis=0),
        m2=_stack_complex(f2),
        m2i=_stack_complex(np.conj(f2).T / n_total),
        m1i=_stack_complex(np.conj(f1).T[:na, :]),
        twr=tw.real, twi=tw.imag)


def _lane_replicated(row_ref, c, ct):
    t = jnp.broadcast_to(row_ref[pl.ds(c, 1), :], (FFT_SLAB, FFT_SLAB)).T
    return t if ct == FFT_SLAB else jnp.tile(t, (1, ct // FFT_SLAB))


def _spectrum_body(k_ref, m1_ref, m2_ref, twr_ref, twi_ref, or_ref, oi_ref, *, n1):
    ct = k_ref.shape[1]

    def first(b, carry):
        a = jnp.dot(m1_ref[...], k_ref[pl.ds(b, n1, stride=FFT_SLAB), :], precision=HI,
                    preferred_element_type=F32)
        or_ref[pl.ds(b, n1, stride=FFT_SLAB), :] = a[:n1]
        oi_ref[pl.ds(b, n1, stride=FFT_SLAB), :] = a[n1:]
        return carry

    lax.fori_loop(0, FFT_SLAB, first, 0)

    def slab(c, carry):
        r0 = pl.multiple_of(c * FFT_SLAB, FFT_SLAB)
        ar, ai = or_ref[pl.ds(r0, FFT_SLAB), :], oi_ref[pl.ds(r0, FFT_SLAB), :]
        tr, ti = _lane_replicated(twr_ref, c, ct), _lane_replicated(twi_ref, c, ct)
        p = jnp.concatenate([ar * tr - ai * ti, ar * ti + ai * tr], axis=0)
        x = jnp.dot(m2_ref[...], p, precision=HI, preferred_element_type=F32)
        or_ref[pl.ds(r0, FFT_SLAB), :] = x[:FFT_SLAB]
        oi_ref[pl.ds(r0, FFT_SLAB), :] = x[FFT_SLAB:]
        return carry

    lax.fori_loop(0, n1, slab, 0)


def _filter_spectrum(k, ct=128):
    n_total, ch = k.shape
    n1 = n_total // FFT_SLAB
    cs = _dft_consts(n_total, n_total)
    consts = [jnp.asarray(cs[name], F32) for name in ("m1_real", "m2", "twr", "twi")]
    full = lambda a: pl.BlockSpec(a.shape, lambda j: (0,) * a.ndim)
    col = pl.BlockSpec((n_total, ct), lambda j: (0, j))
    return pl.pallas_call(
        functools.partial(_spectrum_body, n1=n1),
        grid=(ch // ct,),
        in_specs=[col] + [full(a) for a in consts],
        out_specs=[col, col],
        out_shape=[jax.ShapeDtypeStruct((n_total, ch), F32)] * 2,
        compiler_params=_cparams(("parallel",)),
        name="filter_spectrum",
    )(k, *consts)


def _fftconv_body(z_ref, kr_ref, ki_ref, bias_ref, m1_ref, m2_ref, m2i_ref, m1i_ref, twr_ref, twi_ref,
                  o_ref, wr_ref, wi_ref, *, n1, na):
    ct = z_ref.shape[2]

    def first(b, carry):
        rows = pl.ds(b, na, stride=FFT_SLAB)
        s = jnp.concatenate([z_ref[0, rows, :], z_ref[1, rows, :]], axis=0).astype(BF16)
        a = jnp.dot(m1_ref[...], s, preferred_element_type=F32)
        wr_ref[pl.ds(b, n1, stride=FFT_SLAB), :] = a[:n1]
        wi_ref[pl.ds(b, n1, stride=FFT_SLAB), :] = a[n1:]
        return carry

    lax.fori_loop(0, FFT_SLAB, first, 0)

    def slab(c, carry):
        r0 = pl.multiple_of(c * FFT_SLAB, FFT_SLAB)
        rows = pl.ds(r0, FFT_SLAB)
        ar, ai = wr_ref[rows, :], wi_ref[rows, :]
        tr, ti = _lane_replicated(twr_ref, c, ct), _lane_replicated(twi_ref, c, ct)
        p = jnp.concatenate([ar * tr - ai * ti, ar * ti + ai * tr], axis=0).astype(BF16)
        x = jnp.dot(m2_ref[...], p, preferred_element_type=F32)
        xr, xi = x[:FFT_SLAB], x[FFT_SLAB:]
        kr, ki = kr_ref[rows, :], ki_ref[rows, :]
        y = jnp.concatenate([xr * kr - xi * ki, xr * ki + xi * kr], axis=0).astype(BF16)
        bb = jnp.dot(m2i_ref[...], y, preferred_element_type=F32)
        br, bi = bb[:FFT_SLAB], bb[FFT_SLAB:]
        wr_ref[rows, :] = br * tr + bi * ti
        wi_ref[rows, :] = bi * tr - br * ti
        return carry

    lax.fori_loop(0, n1, slab, 0)

    def last(b, carry):
        rows = pl.ds(b, n1, stride=FFT_SLAB)
        s = jnp.concatenate([wr_ref[rows, :], wi_ref[rows, :]], axis=0).astype(BF16)
        y = jnp.dot(m1i_ref[...], s, preferred_element_type=F32)
        orow = pl.ds(b, na, stride=FFT_SLAB)
        bias = bias_ref[...]
        o_ref[0, orow, :] = y[:na] + z_ref[0, orow, :] * bias
        o_ref[1, orow, :] = y[na:] + z_ref[1, orow, :] * bias
        return carry

    lax.fori_loop(0, FFT_SLAB, last, 0)


def _fftconv(z, kf_re, kf_im, bias, ct=128):
    bsz, n, ch = z.shape
    n_total = 2 * n
    n1, na = n_total // FFT_SLAB, n // FFT_SLAB
    cs = _dft_consts(n_total, n)
    mats = [jnp.asarray(cs[name], BF16) for name in ("m1", "m2", "m2i", "m1i")]
    tws = [jnp.asarray(cs[name], F32) for name in ("twr", "twi")]
    full = lambda a: pl.BlockSpec(a.shape, lambda p, j: (0,) * a.ndim)
    single = pl.Buffered(1)
    zspec = pl.BlockSpec((2, n, ct), lambda p, j: (p, 0, j), pipeline_mode=single)
    kspec = pl.BlockSpec((n_total, ct), lambda p, j: (0, j), pipeline_mode=single)
    return pl.pallas_call(
        functools.partial(_fftconv_body, n1=n1, na=na),
        grid=(bsz // 2, ch // ct),
        in_specs=[zspec, kspec, kspec, pl.BlockSpec((1, ct), lambda p, j: (0, j))]
                 + [full(a) for a in mats] + [full(a) for a in tws],
        out_specs=pl.BlockSpec((2, n, ct), lambda p, j: (p, 0, j), pipeline_mode=single),
        out_shape=jax.ShapeDtypeStruct((bsz, n, ch), F32),
        scratch_shapes=[pltpu.VMEM((n_total, ct), F32), pltpu.VMEM((n_total, ct), F32)],
        compiler_params=_cparams(("parallel", "parallel")),
        name="fftconv",
    )(z, kf_re, kf_im, bias[None].astype(F32), *mats, *tws)


def _hyena_filters(n, w1, b1, f1, w2, b2, f2, w3, width):
    t = jnp.linspace(0.0, 1.0, n, dtype=F32)[:, None]
    bands = (HY_EMB_DIM - 1) // 2
    w = 2.0 * math.pi * jnp.arange(n, dtype=F32)[:, None] / n
    f = jnp.linspace(1e-4, bands - 1, bands, dtype=F32)[None, :]
    z = jnp.concatenate([t, jnp.cos(f * w), -jnp.sin(f * w)], axis=-1)
    h = jnp.sin(f1.astype(F32) * (jnp.dot(z, w1.astype(F32), precision=HI) + b1.astype(F32)))
    h = jnp.sin(f2.astype(F32) * (jnp.dot(h, w2.astype(F32), precision=HI) + b2.astype(F32)))
    h = jnp.dot(h, w3.astype(F32), precision=HI)
    max_decay = math.log(HY_TARGET) / HY_FAST_DECAY
    min_decay = math.log(HY_TARGET) / HY_SLOW_DECAY
    deltas = jnp.linspace(min_decay, max_decay, width, dtype=F32)
    decay = jnp.exp(-t * jnp.abs(deltas)[None, :])
    return h.reshape(n, HY_ORDER, 2, width) * decay[:, None, None, :]


def _two_sided_filter(h_fwd, h_bwd):
    n, d = h_fwd.shape
    k = jnp.concatenate([h_fwd, jnp.zeros((1, d), F32), h_bwd[1:][::-1]], axis=0)
    return k * lax.rsqrt(jnp.sum(k * k, axis=0, keepdims=True) + NORM_EPS)


def _small_conv_body(z_ref, k_ref, bias_ref, mfk_ref, mf_ref, mi_ref, o_ref):
    bsz, n, _ = z_ref.shape
    nt = 2 * n
    kf = jnp.dot(mfk_ref[...], k_ref[...], precision=HI, preferred_element_type=F32)
    kr, ki = kf[:nt], kf[nt:]
    bias = bias_ref[...]
    for p in range(bsz // 2):
        z0, z1 = z_ref[2 * p], z_ref[2 * p + 1]
        x = jnp.dot(mf_ref[...], jnp.concatenate([z0, z1], axis=0).astype(BF16), preferred_element_type=F32)
        xr, xi = x[:nt], x[nt:]
        y = jnp.concatenate([xr * kr - xi * ki, xr * ki + xi * kr], axis=0).astype(BF16)
        w = jnp.dot(mi_ref[...], y, preferred_element_type=F32)
        o_ref[2 * p] = w[:n] + z0 * bias
        o_ref[2 * p + 1] = w[n:] + z1 * bias


def _small_conv(z, k, bias, ct=256):
    bsz, n, ch = z.shape
    nt = 2 * n
    idx = np.arange(nt)
    f = np.exp(-2j * np.pi * np.outer(idx, idx) / nt)
    mfk = jnp.asarray(np.concatenate([f.real, f.imag], axis=0), F32)
    mf = jnp.asarray(_stack_complex(f[:, :n]), BF16)
    mi = jnp.asarray(_stack_complex(np.conj(f).T[:n, :] / nt), BF16)
    full = lambda a: pl.BlockSpec(a.shape, lambda j: (0,) * a.ndim)
    return pl.pallas_call(
        _small_conv_body,
        grid=(ch // ct,),
        in_specs=[pl.BlockSpec((bsz, n, ct), lambda j: (0, 0, j)), pl.BlockSpec((nt, ct), lambda j: (0, j)),
                  pl.BlockSpec((1, ct), lambda j: (0, j)), full(mfk), full(mf), full(mi)],
        out_specs=pl.BlockSpec((bsz, n, ct), lambda j: (0, 0, j)),
        out_shape=jax.ShapeDtypeStruct((bsz, n, ch), F32),
        compiler_params=_cparams(("parallel",)),
        name="small_conv",
    )(z, k, bias[None].astype(F32), mfk, mf, mi)


def _short_conv_body(pc_ref, pp_ref, pn_ref, w_ref, b_ref, v_out, x1_out, x2_out, *, width):
    i = pl.program_id(1)
    cur = pc_ref[0]
    tm = cur.shape[0]
    up = jnp.where(i == 0, 0.0, pp_ref[0, 7:8, :])
    dn = jnp.where(i == pl.num_programs(1) - 1, 0.0, pn_ref[0, 0:1, :])
    prev = jnp.concatenate([up, cur[:tm - 1]], axis=0)
    nxt = jnp.concatenate([cur[1:], dn], axis=0)
    w = w_ref[...]
    y = prev * w[0:1] + cur * w[1:2] + nxt * w[2:3] + b_ref[...]
    v_out[0] = y[:, :width]
    x1_out[0] = y[:, width:2 * width]
    x2_out[0] = y[:, 2 * width:]


def _short_conv(p, short_w, short_b, tm):
    b, n, w3 = p.shape
    width = w3 // 3
    tm = min(tm, n)
    g = tm // 8
    ng = n // 8
    ospec = pl.BlockSpec((1, tm, width), lambda bi, i: (bi, i, 0))
    oshape = jax.ShapeDtypeStruct((b, n, width), F32)
    return pl.pallas_call(
        functools.partial(_short_conv_body, width=width),
        grid=(b, n // tm),
        in_specs=[pl.BlockSpec((1, tm, w3), lambda bi, i: (bi, i, 0)),
                  pl.BlockSpec((1, 8, w3), lambda bi, i: (bi, jnp.maximum(i * g - 1, 0), 0)),
                  pl.BlockSpec((1, 8, w3), lambda bi, i: (bi, jnp.minimum((i + 1) * g, ng - 1), 0)),
                  pl.BlockSpec((HY_SHORT, w3), lambda bi, i: (0, 0)),
                  pl.BlockSpec((1, w3), lambda bi, i: (0, 0))],
        out_specs=[ospec, ospec, ospec], out_shape=[oshape, oshape, oshape],
        compiler_params=_cparams(("parallel", "parallel")),
        name="short_conv",
    )(p, p, p, short_w, short_b[None])


def _mul_body(a_ref, b_ref, o_ref):
    o_ref[...] = a_ref[...] * b_ref[...]


def _mul(a, b, tm):
    bsz, n, d = a.shape
    tm = min(tm, n)
    spec = pl.BlockSpec((1, tm, d), lambda bi, i: (bi, i, 0))
    return pl.pallas_call(
        _mul_body, grid=(bsz, n // tm), in_specs=[spec, spec], out_specs=spec,
        out_shape=jax.ShapeDtypeStruct(a.shape, a.dtype),
        compiler_params=_cparams(("parallel", "parallel")), name="gate_mul",
    )(a, b)


def _hyena_core(p, short_w, short_b, w1, b1, f1, w2, b2, f2, w3, fbias, tm):
    n = p.shape[1]
    width = p.shape[2] // 3
    v, x1, x2 = _short_conv(p, short_w, short_b, tm)
    filt = _hyena_filters(n, w1, b1, f1, w2, b2, f2, w3, width)
    ks = [_two_sided_filter(filt[:, o, 0], filt[:, o, 1]) for o in range(HY_ORDER)]
    if 2 * n // FFT_SLAB >= FFT_SLAB:
        conv = lambda z, o: _fftconv(z, *_filter_spectrum(ks[o]), fbias[o])
    else:
        conv = lambda z, o: _small_conv(z, ks[o], fbias[o])
    w0 = conv(v, 0)
    w1_ = conv(_mul(x1, w0, tm), 1)
    return w1_, x2


def _sink_rows(sink, hkv, grp, tq):
    return jnp.repeat(sink.astype(F32).reshape(hkv, grp), tq, axis=1)[..., None]


def kernel(x, c, ctx, c_ctx, ada_w, ada_b, norm_mix_g, norm_ffn_g, router_w, exp_w_gate, exp_w_up, exp_w_down,
           a_w_in, a_w_out, a_q_g, a_k_g, a_sink, b_w_in, b_short_w, b_short_b, b_w1, b_b1, b_f1, b_w2, b_b2,
           b_f2, b_w3, b_bias, b_w_out, c_w_in, c_w_out, c_q_g, c_k_g, c_rpb):
    bsz, n, d = x.shape
    nctx = ctx.shape[1]
    depth = ada_w.shape[0]
    ne = router_w.shape[2]
    cap = EC_CAPACITY * n // ne
    cap_c = EC_CAPACITY * nctx // ne
    tm = 256

    pad_rows = (-(bsz + 1)) % 8
    rows = jnp.concatenate([c, c_ctx[None], jnp.zeros((pad_rows, d), F32)], axis=0)
    mod_all = _ada_mod(rows, ada_w, ada_b)

    h, hc = x, ctx
    for i in range(depth):
        last = i == depth - 1
        kind, j = i % N_MIXERS, i // N_MIXERS
        mod = mod_all[i, :bsz].reshape(bsz, 1, 6, d)
        sh1, sc1, g1, sh2, sc2, g2 = [mod[:, :, t] for t in range(6)]
        modc = jnp.broadcast_to(mod_all[i, bsz].reshape(1, 1, 6, d), (bsz, 1, 6, d))
        csh1, csc1, cg1, csh2, csc2, cg2 = [modc[:, :, t] for t in range(6)]
        yc = o_gate = yc_gate = None
        if kind == 0:
            hkv, grp = A_KV_HEADS, a_w_in.shape[2] // HEAD_DIM // A_KV_HEADS - 2
            qw, kvw = hkv * grp * HEAD_DIM, hkv * HEAD_DIM
            w_in = a_w_in[j].astype(BF16)
            w_out = a_w_out[j].astype(BF16)
            q_rot, q_pl, k_rot, v = _proj_attn(h, sh1, sc1, norm_mix_g[i], w_in, a_q_g[j], a_k_g[j],
                                               qw, kvw, True, tm)
            qc, kc, vc = _proj_attn(hc, csh1, csc1, norm_mix_g[i], w_in, a_q_g[j], a_k_g[j],
                                    qw, kvw, False, tm)
            o = _local_attn(q_rot, q_pl, k_rot, v, kc, vc, _window_bias(n),
                            _sink_rows(a_sink[j], hkv, grp, A_BLOCK), hkv, grp, A_BLOCK)
            if not last:
                yc = _ctx_attn(qc, kc, vc, _sink_rows(a_sink[j], hkv, grp, nctx), hkv, grp)
        elif kind == 1:
            w_in = b_w_in[j].astype(BF16)
            w_out = b_w_out[j].astype(BF16)
            hy = (b_short_w[j], b_short_b[j], b_w1[j], b_b1[j], b_f1[j], b_w2[j], b_b2[j], b_f2[j], b_w3[j],
                  b_bias[j])
            o, o_gate = _hyena_core(_proj_plain(h, sh1, sc1, norm_mix_g[i], w_in, tm), *hy, tm)
            if not last:
                yc, yc_gate = _hyena_core(_proj_plain(hc, csh1, csc1, norm_mix_g[i], w_in, tm), *hy, tm)
        else:
            nh = c_w_in.shape[2] // HEAD_DIM // 3
            hw = nh * HEAD_DIM
            w_in = c_w_in[j].astype(BF16)
            w_out = c_w_out[j].astype(BF16)
            q, k, v = _proj_attn(h, sh1, sc1, norm_mix_g[i], w_in, c_q_g[j], c_k_g[j], hw, hw, False, tm)
            qc, kc, vc = _proj_attn(hc, csh1, csc1, norm_mix_g[i], w_in, c_q_g[j], c_k_g[j], hw, hw, False, tm)
            o = _local_attn(q, q, k, v, kc, vc, _neighbourhood_bias(c_rpb[j], n), None, nh, 1, C_BLOCK)
            if not last:
                yc = _ctx_attn(qc, kc, vc, None, nh, 1)

        rwt = router_w[i].T
        h1, u2, aff = _outproj(o, w_out, h, g1, norm_ffn_g[i], sh2, sc2, rwt, tm, o_gate)
        gl, il = lax.top_k(aff, cap)
        rows_l = il + (jnp.arange(bsz) * n)[:, None, None]
        rows_all = rows_l.transpose(1, 0, 2).reshape(ne, bsz * cap)
        gates_all = gl.transpose(1, 0, 2).reshape(ne, bsz * cap)
        u_all = u2.reshape(bsz * n, d)
        if not last:
            hc1, uc2, affc = _outproj(yc, w_out, hc, cg1, norm_ffn_g[i], csh2, csc2, rwt, tm, yc_gate)
            gcx, icx = lax.top_k(affc, cap_c)
            rows_c = icx + (bsz * n + jnp.arange(bsz) * nctx)[:, None, None]
            rows_all = jnp.concatenate([rows_all, rows_c.transpose(1, 0, 2).reshape(ne, bsz * cap_c)], axis=1)
            gates_all = jnp.concatenate([gates_all, gcx.transpose(1, 0, 2).reshape(ne, bsz * cap_c)], axis=1)
            u_all = jnp.concatenate([u_all, uc2.reshape(bsz * nctx, d)], axis=0)
        r = rows_all.shape[1]
        xs = u_all[rows_all]
        ye = _expert_ffn(xs, gates_all[..., None], exp_w_gate[i], exp_w_up[i], exp_w_down[i], r // 4, 512)
        moe = jnp.zeros((u_all.shape[0], d), F32).at[rows_all.reshape(-1)].add(ye.reshape(-1, d))
        h = h1 + g2 * moe[:bsz * n].reshape(bsz, n, d)
        if not last:
            hc = hc1 + cg2 * moe[bsz * n:].reshape(bsz, nctx, d)
    return h
```

```python
import functools
import math

import numpy as np
import jax
import jax.numpy as jnp
from jax import lax
from jax.experimental import pallas as pl
from jax.experimental.pallas import tpu as pltpu

F32 = jnp.float32
BF16 = jnp.bfloat16
HI = lax.Precision.HIGHEST

GRID_W = 64
HEAD_DIM = 64
NORM_EPS = 1e-6
N_MIXERS = 3
A_KV_HEADS = 4
A_BLOCK = 128
ROPE_BASE = 10000.0
HY_ORDER = 2
HY_EMB_DIM = 33
HY_SHORT = 3
HY_FAST_DECAY = 0.3
HY_SLOW_DECAY = 1.5
HY_TARGET = 1e-2
NA_ROWS = 8
NA_COLS = 16
HEAD_GROUP = 4
C_BLOCK = 256
N_EXPERTS = 16
EC_CAPACITY = 2
NEG = -1e30

LANES = 128
VMEM_LIMIT = 56 * 1024 * 1024


def _cparams(sem):
    return pltpu.CompilerParams(dimension_semantics=sem, vmem_limit_bytes=VMEM_LIMIT)


def _sigmoid(x):
    return 1.0 / (1.0 + jnp.exp(-x))


def _modulate(h, g, shift, scale):
    ms = jnp.mean(h * h, axis=-1, keepdims=True)
    y = h * lax.rsqrt(ms + NORM_EPS)
    return (y * g) * (1.0 + scale) + shift


def _mod_body(s_ref, w_ref, b_ref, o_ref):
    s = s_ref[...]
    s = s * _sigmoid(s)
    o_ref[0] = jnp.dot(s, w_ref[0], precision=HI, preferred_element_type=F32) + b_ref[0]


def _ada_mod(rows, ada_w, ada_b):
    depth, d, n6 = ada_w.shape
    r = rows.shape[0]
    tn = 1536
    return pl.pallas_call(
        _mod_body,
        grid=(depth, n6 // tn),
        in_specs=[pl.BlockSpec((r, d), lambda l, j: (0, 0)),
                  pl.BlockSpec((1, d, tn), lambda l, j: (l, 0, j)),
                  pl.BlockSpec((1, 1, tn), lambda l, j: (l, 0, j))],
        out_specs=pl.BlockSpec((1, r, tn), lambda l, j: (l, 0, j)),
        out_shape=jax.ShapeDtypeStruct((depth, r, n6), F32),
        compiler_params=_cparams(("arbitrary", "arbitrary")),
        name="ada_mod",
    )(rows, ada_w, ada_b.reshape(depth, 1, n6))


def _head_norm(x, gsum, gexp, gain):
    ss = jnp.dot(x * x, gsum, precision=HI, preferred_element_type=F32)
    r = lax.rsqrt(ss * (1.0 / HEAD_DIM) + NORM_EPS)
    rb = jnp.dot(r, gexp, precision=HI, preferred_element_type=F32)
    return (x * rb) * gain


def _rope(x, cos, sin_signed):
    rows, w = x.shape
    lane = lax.broadcasted_iota(jnp.int32, (rows, LANES), 1)
    first = (lane % 32) < 16
    outs = []
    for c in range(w // LANES):
        xc = x[:, c * LANES:(c + 1) * LANES]
        partner = jnp.where(first, pltpu.roll(xc, LANES - 16, 1), pltpu.roll(xc, 16, 1))
        outs.append(xc * cos + partner * sin_signed)
    return jnp.concatenate(outs, axis=1)


def _proj_attn_body(*refs, qw, kvw, rope):
    h_ref, sh_ref, sc_ref, g_ref, w_ref, qg_ref, kg_ref, gsq_ref, geq_ref, gsk_ref, gek_ref = refs[:11]
    u = _modulate(h_ref[0], g_ref[...], sh_ref[0], sc_ref[0])
    p = jnp.dot(u.astype(BF16), w_ref[...], preferred_element_type=F32)
    q = _head_norm(p[:, :qw], gsq_ref[...], geq_ref[...], qg_ref[...]) * (HEAD_DIM ** -0.5)
    k = _head_norm(p[:, qw:qw + kvw], gsk_ref[...], gek_ref[...], kg_ref[...])
    v = p[:, qw + kvw:]
    if rope:
        cos_ref, sin_ref, qr_out, qp_out, kr_out, v_out = refs[11:]
        cos, sin = cos_ref[...], sin_ref[...]
        qr_out[0] = _rope(q, cos, sin).astype(BF16)
        qp_out[0] = q.astype(BF16)
        kr_out[0] = _rope(k, cos, sin).astype(BF16)
        v_out[0] = v.astype(BF16)
    else:
        q_out, k_out, v_out = refs[11:]
        q_out[0] = q.astype(BF16)
        k_out[0] = k.astype(BF16)
        v_out[0] = v.astype(BF16)


def _group_mats(w):
    nh = w // HEAD_DIM
    gs = np.zeros((w, LANES), np.float32)
    gs[np.arange(w), np.arange(w) // HEAD_DIM] = 1.0
    return jnp.asarray(gs), jnp.asarray(gs.T.copy())


def _rope_tables(n):
    t = jnp.arange(n)
    row = (t // GRID_W).astype(F32)
    col = (t % GRID_W).astype(F32)
    axis_dim = HEAD_DIM // 2
    inv_freq = 1.0 / (ROPE_BASE ** (jnp.arange(0, axis_dim, 2, dtype=F32) / axis_dim))
    ang_r = row[:, None] * inv_freq
    ang_c = col[:, None] * inv_freq
    cos64 = jnp.concatenate([jnp.cos(ang_r), jnp.cos(ang_r), jnp.cos(ang_c), jnp.cos(ang_c)], axis=-1)
    sin64 = jnp.concatenate([-jnp.sin(ang_r), jnp.sin(ang_r), -jnp.sin(ang_c), jnp.sin(ang_c)], axis=-1)
    return jnp.tile(cos64, (1, 2)), jnp.tile(sin64, (1, 2))


def _proj_attn(h, shift, scale, g, w_bf, q_g, k_g, qw, kvw, rope, tm):
    b, n, d = h.shape
    nout = w_bf.shape[1]
    tm = min(tm, n)
    gsq, geq = _group_mats(qw)
    gsk, gek = _group_mats(kvw)
    qg = jnp.tile(q_g, qw // HEAD_DIM)[None]
    kg = jnp.tile(k_g, kvw // HEAD_DIM)[None]
    full = lambda a: pl.BlockSpec(a.shape, lambda bi, i: (0,) * a.ndim)
    vec = pl.BlockSpec((1, 1, d), lambda bi, i: (bi, 0, 0))
    args = [h, shift, scale, g[None], w_bf, qg, kg, gsq, geq, gsk, gek]
    in_specs = [pl.BlockSpec((1, tm, d), lambda bi, i: (bi, i, 0)), vec, vec, full(args[3]), full(w_bf),
                full(qg), full(kg), full(gsq), full(geq), full(gsk), full(gek)]
    ospec = lambda w: pl.BlockSpec((1, tm, w), lambda bi, i: (bi, i, 0))
    oshape = lambda w: jax.ShapeDtypeStruct((b, n, w), BF16)
    if rope:
        cos, sin = _rope_tables(n)
        args += [cos, sin]
        in_specs += [pl.BlockSpec((tm, LANES), lambda bi, i: (i, 0))] * 2
        out_specs = [ospec(qw), ospec(qw), ospec(kvw), ospec(kvw)]
        out_shape = [oshape(qw), oshape(qw), oshape(kvw), oshape(kvw)]
    else:
        out_specs = [ospec(qw), ospec(kvw), ospec(kvw)]
        out_shape = [oshape(qw), oshape(kvw), oshape(kvw)]
    return pl.pallas_call(
        functools.partial(_proj_attn_body, qw=qw, kvw=kvw, rope=rope),
        grid=(b, n // tm), in_specs=in_specs, out_specs=out_specs, out_shape=out_shape,
        compiler_params=_cparams(("parallel", "parallel")),
        name="proj_attn_rope" if rope else "proj_attn",
    )(*args)


def _proj_plain_body(h_ref, sh_ref, sc_ref, g_ref, w_ref, o_ref):
    u = _modulate(h_ref[0], g_ref[...], sh_ref[0], sc_ref[0])
    o_ref[0] = jnp.dot(u.astype(BF16), w_ref[...], preferred_element_type=F32)


def _proj_plain(h, shift, scale, g, w_bf, tm):
    b, n, d = h.shape
    nout = w_bf.shape[1]
    tm = min(tm, n)
    vec = pl.BlockSpec((1, 1, d), lambda bi, i: (bi, 0, 0))
    return pl.pallas_call(
        _proj_plain_body,
        grid=(b, n // tm),
        in_specs=[pl.BlockSpec((1, tm, d), lambda bi, i: (bi, i, 0)), vec, vec,
                  pl.BlockSpec((1, d), lambda bi, i: (0, 0)),
                  pl.BlockSpec((d, nout), lambda bi, i: (0, 0))],
        out_specs=pl.BlockSpec((1, tm, nout), lambda bi, i: (bi, i, 0)),
        out_shape=jax.ShapeDtypeStruct((b, n, nout), F32),
        compiler_params=_cparams(("parallel", "parallel")),
        name="proj_plain",
    )(h, shift, scale, g[None], w_bf)


def _local_attn_body(*refs, hkv, grp, tq, use_sink, head_bias):
    (ql_ref, qc_ref, kp_ref, kc_ref, kn_ref, vp_ref, vc_ref, vn_ref, kx_ref, vx_ref, bias_ref) = refs[:11]
    if use_sink:
        sink_ref, o_ref = refs[11:]
    else:
        (o_ref,) = refs[11:]
    dh = HEAD_DIM
    nt = (((1,), (1,)), ((), ()))
    qcols = lambda hh: [slice((hh * grp + g) * dh, (hh * grp + g + 1) * dh) for g in range(grp)]
    for h0 in range(0, hkv, HEAD_GROUP):
        heads = range(h0, min(h0 + HEAD_GROUP, hkv))
        s, vall = [], []
        for hh in heads:
            ks = slice(hh * dh, (hh + 1) * dh)
            q_l = jnp.concatenate([ql_ref[0, :, c] for c in qcols(hh)], axis=0)
            q_c = jnp.concatenate([qc_ref[0, :, c] for c in qcols(hh)], axis=0)
            kw = jnp.concatenate([kp_ref[0, :, ks], kc_ref[0, :, ks], kn_ref[0, :, ks]], axis=0)
            s_loc = lax.dot_general(q_l, kw, nt, preferred_element_type=F32)
            s_loc = s_loc + bias_ref[0, hh if head_bias else 0]
            s_ctx = lax.dot_general(q_c, kx_ref[0, :, ks], nt, preferred_element_type=F32)
            s.append(jnp.concatenate([s_loc, s_ctx], axis=1))
            v = jnp.concatenate([vp_ref[0, :, ks], vc_ref[0, :, ks], vn_ref[0, :, ks], vx_ref[0, :, ks]], axis=0)
            vall.append(jnp.concatenate([v, jnp.ones_like(v)], axis=1))
        m = [jnp.max(x, axis=-1, keepdims=True) for x in s]
        if use_sink:
            sk = [sink_ref[hh] for hh in heads]
            m = [jnp.maximum(a, b) for a, b in zip(m, sk)]
        p = [jnp.exp(x - a).astype(BF16) for x, a in zip(s, m)]
        ox = [jnp.dot(x, v, preferred_element_type=F32) for x, v in zip(p, vall)]
        den = [x[:, dh:dh + 1] for x in ox]
        if use_sink:
            den = [d + jnp.exp(b - a) for d, a, b in zip(den, m, sk)]
        o = [x[:, :dh] * (1.0 / d) for x, d in zip(ox, den)]
        for hh, oh in zip(heads, o):
            for g, c in enumerate(qcols(hh)):
                o_ref[0, :, c] = oh[g * tq:(g + 1) * tq].astype(BF16)


def _local_attn(q_loc, q_ctx, k, v, kx, vx, bias, sink_rows, hkv, grp, tq):
    b, n, qw = q_loc.shape
    kvw = k.shape[2]
    nb = n // tq
    nctx = kx.shape[1]
    head_bias = bias.shape[1] > 1
    use_sink = sink_rows is not None
    qspec = pl.BlockSpec((1, tq, qw), lambda bi, i: (bi, i, 0))
    prev = pl.BlockSpec((1, tq, kvw), lambda bi, i: (bi, jnp.maximum(i - 1, 0), 0))
    cur = pl.BlockSpec((1, tq, kvw), lambda bi, i: (bi, i, 0))
    nxt = pl.BlockSpec((1, tq, kvw), lambda bi, i: (bi, jnp.minimum(i + 1, nb - 1), 0))
    xspec = pl.BlockSpec((1, nctx, kvw), lambda bi, i: (bi, 0, 0))
    bspec = pl.BlockSpec((1,) + bias.shape[1:],
                         lambda bi, i: (jnp.where(i == 0, 0, jnp.where(i == nb - 1, 2, 1)), 0, 0, 0))
    args = [q_loc, q_ctx, k, k, k, v, v, v, kx, vx, bias]
    in_specs = [qspec, qspec, prev, cur, nxt, prev, cur, nxt, xspec, xspec, bspec]
    if use_sink:
        args.append(sink_rows)
        in_specs.append(pl.BlockSpec(sink_rows.shape, lambda bi, i: (0, 0, 0)))
    return pl.pallas_call(
        functools.partial(_local_attn_body, hkv=hkv, grp=grp, tq=tq, use_sink=use_sink, head_bias=head_bias),
        grid=(b, nb), in_specs=in_specs,
        out_specs=pl.BlockSpec((1, tq, qw), lambda bi, i: (bi, i, 0)),
        out_shape=jax.ShapeDtypeStruct((b, n, qw), BF16),
        compiler_params=_cparams(("parallel", "parallel")),
        name="local_attn_sink" if use_sink else "local_attn",
    )(*args)


def _window_bias(n):
    tq = A_BLOCK
    grp = 4
    qi = np.arange(tq)[:, None]
    kj = np.arange(3 * tq)[None, :] - tq
    band = np.abs(qi - kj) <= A_BLOCK
    nb = n // tq
    out = []
    for which in range(3):
        blk = {0: 0, 1: min(1, nb - 1), 2: nb - 1}[which]
        pos = blk * tq + kj
        ok = band & (pos >= 0) & (pos < n)
        out.append(np.tile(np.where(ok, 0.0, NEG).astype(np.float32), (grp, 1)))
    return jnp.asarray(np.stack(out)[:, None])


def _nbr_bias_body(rpb_ref, colok_ref, o_ref, *, row_ok):
    gw = GRID_W
    col_ok = colok_ref[...] > 0.5
    neg = jnp.full((gw, gw), NEG, F32)
    toep = []
    for ro in range(2 * NA_ROWS - 1):
        x = jnp.broadcast_to(rpb_ref[0, ro:ro + 1, :], (gw, LANES))
        t = pltpu.roll(x, LANES - (NA_COLS - 1), 1, stride=1, stride_axis=0)[:, :gw]
        toep.append(jnp.where(col_ok, t, NEG))
    rq = C_BLOCK // gw
    for which in range(3):
        rows_out = []
        for qa in range(rq):
            blocks = [toep[(ka - rq) - qa + (NA_ROWS - 1)] if row_ok[which][qa][ka] else neg
                      for ka in range(3 * rq)]
            rows_out.append(jnp.concatenate(blocks, axis=1))
        o_ref[which, 0] = jnp.concatenate(rows_out, axis=0)


def _neighbourhood_bias(rpb, n):
    nh = rpb.shape[0]
    rows = n // GRID_W
    kr = min(NA_ROWS, rows)
    rq = C_BLOCK // GRID_W
    nb = n // C_BLOCK
    assert nb >= 3 and rpb.shape[1:] == (2 * NA_ROWS - 1, 2 * NA_COLS - 1)
    qc = np.arange(GRID_W)
    c0 = np.clip(qc - NA_COLS // 2, 0, GRID_W - NA_COLS)
    col_ok = ((qc[None, :] >= c0[:, None]) & (qc[None, :] < c0[:, None] + NA_COLS)).astype(np.float32)
    row_ok = []
    for which in range(3):
        blk = {0: 0, 1: 1, 2: nb - 1}[which]
        qr = blk * rq + np.arange(rq)
        kr_abs = blk * rq + np.arange(3 * rq) - rq
        r0 = np.clip(qr - kr // 2, 0, rows - kr)
        ok = (kr_abs[None, :] >= r0[:, None]) & (kr_abs[None, :] < r0[:, None] + kr)
        row_ok.append(tuple(tuple(bool(v) for v in row) for row in ok))
    rpb_pad = jnp.pad(rpb.astype(F32), ((0, 0), (0, 1), (0, LANES - rpb.shape[2])))
    return pl.pallas_call(
        functools.partial(_nbr_bias_body, row_ok=tuple(row_ok)),
        grid=(nh,),
        in_specs=[pl.BlockSpec((1, 2 * NA_ROWS, LANES), lambda hh: (hh, 0, 0)),
                  pl.BlockSpec((GRID_W, GRID_W), lambda hh: (0, 0))],
        out_specs=pl.BlockSpec((3, 1, C_BLOCK, 3 * C_BLOCK), lambda hh: (0, hh, 0, 0)),
        out_shape=jax.ShapeDtypeStruct((3, nh, C_BLOCK, 3 * C_BLOCK), F32),
        compiler_params=_cparams(("parallel",)),
        name="nbr_bias",
    )(rpb_pad, jnp.asarray(col_ok))


def _ctx_attn_body(*refs, hkv, grp, use_sink):
    q_ref, k_ref, v_ref = refs[:3]
    if use_sink:
        sink_ref, o_ref = refs[3:]
    else:
        (o_ref,) = refs[3:]
    dh = HEAD_DIM
    nq = q_ref.shape[1]
    nt = (((1,), (1,)), ((), ()))
    for hh in range(hkv):
        ks = slice(hh * dh, (hh + 1) * dh)
        qcols = [slice((hh * grp + g) * dh, (hh * grp + g + 1) * dh) for g in range(grp)]
        q = jnp.concatenate([q_ref[0, :, c] for c in qcols], axis=0)
        s = lax.dot_general(q, k_ref[0, :, ks], nt, preferred_element_type=F32)
        m = jnp.max(s, axis=-1, keepdims=True)
        if use_sink:
            sk = sink_ref[hh]
            m = jnp.maximum(m, sk)
        e = jnp.exp(s - m)
        den = jnp.sum(e, axis=-1, keepdims=True)
        if use_sink:
            den = den + jnp.exp(sk - m)
        o = jnp.dot((e * (1.0 / den)).astype(BF16), v_ref[0, :, ks], preferred_element_type=F32)
        for g in range(grp):
            o_ref[0, :, qcols[g]] = o[g * nq:(g + 1) * nq].astype(BF16)


def _ctx_attn(q, k, v, sink_rows, hkv, grp):
    b, nq, qw = q.shape
    kvw = k.shape[2]
    use_sink = sink_rows is not None
    args = [q, k, v]
    in_specs = [pl.BlockSpec((1, nq, qw), lambda bi: (bi, 0, 0)),
                pl.BlockSpec((1, nq, kvw), lambda bi: (bi, 0, 0)),
                pl.BlockSpec((1, nq, kvw), lambda bi: (bi, 0, 0))]
    if use_sink:
        args.append(sink_rows)
        in_specs.append(pl.BlockSpec(sink_rows.shape, lambda bi: (0, 0, 0)))
    return pl.pallas_call(
        functools.partial(_ctx_attn_body, hkv=hkv, grp=grp, use_sink=use_sink),
        grid=(b,), in_specs=in_specs,
        out_specs=pl.BlockSpec((1, nq, qw), lambda bi: (bi, 0, 0)),
        out_shape=jax.ShapeDtypeStruct((b, nq, qw), BF16),
        compiler_params=_cparams(("parallel",)),
        name="ctx_attn_sink" if use_sink else "ctx_attn",
    )(*args)


def _outproj_body(*refs, gated):
    if gated:
        o_ref, og_ref, w_ref, h_ref, g1_ref, gn_ref, sh2_ref, sc2_ref, rwt_ref, h1_out, u2_out, aff_out = refs
        o = o_ref[0] * og_ref[0]
    else:
        o_ref, w_ref, h_ref, g1_ref, gn_ref, sh2_ref, sc2_ref, rwt_ref, h1_out, u2_out, aff_out = refs
        o = o_ref[0]
    y = jnp.dot(o.astype(BF16), w_ref[...], preferred_element_type=F32)
    h1 = h_ref[0] + g1_ref[0] * y
    h1_out[0] = h1
    u2 = _modulate(h1, gn_ref[...], sh2_ref[0], sc2_ref[0])
    u2_out[0] = u2.astype(BF16)
    lt = lax.dot_general(rwt_ref[...], u2, (((1,), (1,)), ((), ())), precision=HI, preferred_element_type=F32)
    e = jnp.exp(lt - jnp.max(lt, axis=0, keepdims=True))
    aff_out[0] = e / jnp.sum(e, axis=0, keepdims=True)


def _outproj(o, w_bf, h, g1, gn, sh2, sc2, rwt, tm, o_gate=None):
    b, n, d = h.shape
    kin = o.shape[2]
    ne = rwt.shape[0]
    tm = min(tm, n)
    vec = pl.BlockSpec((1, 1, d), lambda bi, i: (bi, 0, 0))
    ospec = pl.BlockSpec((1, tm, kin), lambda bi, i: (bi, i, 0))
    gated = o_gate is not None
    return pl.pallas_call(
        functools.partial(_outproj_body, gated=gated),
        grid=(b, n // tm),
        in_specs=([ospec, ospec] if gated else [ospec]) + [
                  pl.BlockSpec((kin, d), lambda bi, i: (0, 0)),
                  pl.BlockSpec((1, tm, d), lambda bi, i: (bi, i, 0)),
                  vec, pl.BlockSpec((1, d), lambda bi, i: (0, 0)), vec, vec,
                  pl.BlockSpec((ne, d), lambda bi, i: (0, 0))],
        out_specs=[pl.BlockSpec((1, tm, d), lambda bi, i: (bi, i, 0)),
                   pl.BlockSpec((1, tm, d), lambda bi, i: (bi, i, 0)),
                   pl.BlockSpec((1, ne, tm), lambda bi, i: (bi, 0, i))],
        out_shape=[jax.ShapeDtypeStruct((b, n, d), F32),
                   jax.ShapeDtypeStruct((b, n, d), BF16),
                   jax.ShapeDtypeStruct((b, ne, n), F32)],
        compiler_params=_cparams(("parallel", "parallel")),
        name="outproj_router",
    )(*((o, o_gate) if gated else (o,)), w_bf, h, g1, gn[None], sh2, sc2, rwt)


def _expert_body(xs_ref, wg_ref, wu_ref, wd_ref, gate_ref, o_ref):
    f = pl.program_id(2)
    xs = xs_ref[0]
    a = jnp.dot(xs, wg_ref[...].astype(BF16), preferred_element_type=F32)
    bb = jnp.dot(xs, wu_ref[...].astype(BF16), preferred_element_type=F32)
    hid = (a * _sigmoid(a)) * bb
    y = jnp.dot(hid.astype(BF16), wd_ref[...].astype(BF16), preferred_element_type=F32)

    @pl.when(f == 0)
    def _():
        o_ref[0] = y

    @pl.when(f > 0)
    def _():
        o_ref[0] += y

    @pl.when(f == pl.num_programs(2) - 1)
    def _():
        o_ref[0] = o_ref[0] * gate_ref[0]


def _expert_ffn(xs, gates, w_gate, w_up, w_down, layer, tr, tf):
    ne, r, d = xs.shape
    ff = w_gate.shape[3]
    return pl.pallas_call(
        _expert_body,
        grid=(ne, r // tr, ff // tf),
        in_specs=[pl.BlockSpec((1, tr, d), lambda e, c, f: (e, c, 0)),
                  pl.BlockSpec((None, None, d, tf), lambda e, c, f: (layer, e, 0, f)),
                  pl.BlockSpec((None, None, d, tf), lambda e, c, f: (layer, e, 0, f)),
                  pl.BlockSpec((None, None, tf, d), lambda e, c, f: (layer, e, f, 0)),
                  pl.BlockSpec((1, tr, 1), lambda e, c, f: (e, c, 0))],
        out_specs=pl.BlockSpec((1, tr, d), lambda e, c, f: (e, c, 0)),
        out_shape=jax.ShapeDtypeStruct((ne, r, d), F32),
        compiler_params=_cparams(("parallel", "parallel", "arbitrary")),
        name="expert_ffn",
    )(xs, w_gate, w_up, w_down, gates)


FFT_SLAB = 128
FFT_UNROLL = 8


def _stack_complex(m):
    return np.block([[m.real, -m.imag], [m.imag, m.real]])


def _dft_consts(n_total, n_nonzero):
    n1 = n_total // FFT_SLAB
    na = n_nonzero // FFT_SLAB
    idx1 = np.arange(n1)
    f1 = np.exp(-2j * np.pi * np.outer(idx1, idx1) / n1)
    idx2 = np.arange(FFT_SLAB)
    f2 = np.exp(-2j * np.pi * np.outer(idx2, idx2) / FFT_SLAB)
    tw = np.exp(-2j * np.pi * np.outer(idx1, idx2) / n_total)
    return dict(
        m1=_stack_complex(f1[:, :na]),
        m1_real=np.concatenate([f1.real, f1.imag], axis=0),
        m2=_stack_complex(f2),
        m2i=_stack_complex(np.conj(f2).T / n_total),
        m1i=_stack_complex(np.conj(f1).T[:na, :]),
        twr=tw.real, twi=tw.imag)


def _lane_replicated(row_ref, c, ct):
    t = jnp.broadcast_to(row_ref[pl.ds(c, 1), :], (FFT_SLAB, FFT_SLAB)).T
    return t if ct == FFT_SLAB else jnp.tile(t, (1, ct // FFT_SLAB))


def _dot_split(mh_ref, ml_ref, x):
    xh = x.astype(BF16)
    xl = (x - xh.astype(F32)).astype(BF16)
    mh = mh_ref[...]
    return (jnp.dot(mh, xh, preferred_element_type=F32) + jnp.dot(mh, xl, preferred_element_type=F32)
            + jnp.dot(ml_ref[...], xh, preferred_element_type=F32))


def _spectrum_body(k_ref, m1h_ref, m1l_ref, m2h_ref, m2l_ref, twr_ref, twi_ref, or_ref, oi_ref, *, n1):
    ct = k_ref.shape[1]

    def first(b, carry):
        a = _dot_split(m1h_ref, m1l_ref, k_ref[pl.ds(b, n1, stride=FFT_SLAB), :])
        or_ref[pl.ds(b, n1, stride=FFT_SLAB), :] = a[:n1]
        oi_ref[pl.ds(b, n1, stride=FFT_SLAB), :] = a[n1:]
        return carry

    lax.fori_loop(0, FFT_SLAB, first, 0, unroll=FFT_UNROLL)

    def slab(c, carry):
        r0 = pl.multiple_of(c * FFT_SLAB, FFT_SLAB)
        ar, ai = or_ref[pl.ds(r0, FFT_SLAB), :], oi_ref[pl.ds(r0, FFT_SLAB), :]
        tr, ti = _lane_replicated(twr_ref, c, ct), _lane_replicated(twi_ref, c, ct)
        p = jnp.concatenate([ar * tr - ai * ti, ar * ti + ai * tr], axis=0)
        x = _dot_split(m2h_ref, m2l_ref, p)
        or_ref[pl.ds(r0, FFT_SLAB), :] = x[:FFT_SLAB]
        oi_ref[pl.ds(r0, FFT_SLAB), :] = x[FFT_SLAB:]
        return carry

    lax.fori_loop(0, n1, slab, 0, unroll=FFT_UNROLL)


def _split_bf16(m):
    hi = jnp.asarray(m, F32).astype(BF16)
    lo = (jnp.asarray(m, F32) - hi.astype(F32)).astype(BF16)
    return hi, lo


def _filter_spectrum(k, ct=128):
    n_total, ch = k.shape
    n1 = n_total // FFT_SLAB
    cs = _dft_consts(n_total, n_total)
    consts = [*_split_bf16(cs["m1_real"]), *_split_bf16(cs["m2"]),
              jnp.asarray(cs["twr"], F32), jnp.asarray(cs["twi"], F32)]
    full = lambda a: pl.BlockSpec(a.shape, lambda j: (0,) * a.ndim)
    col = pl.BlockSpec((n_total, ct), lambda j: (0, j))
    return pl.pallas_call(
        functools.partial(_spectrum_body, n1=n1),
        grid=(ch // ct,),
        in_specs=[col] + [full(a) for a in consts],
        out_specs=[col, col],
        out_shape=[jax.ShapeDtypeStruct((n_total, ch), F32)] * 2,
        compiler_params=_cparams(("parallel",)),
        name="filter_spectrum",
    )(k, *consts)


def _fftconv_body(z_ref, kr_ref, ki_ref, bias_ref, m1_ref, m2_ref, m2i_ref, m1i_ref, twr_ref, twi_ref,
                  o_ref, wr_ref, wi_ref, *, n1, na):
    ct = z_ref.shape[2]

    def first(b, carry):
        rows = pl.ds(b, na, stride=FFT_SLAB)
        s = jnp.concatenate([z_ref[0, rows, :], z_ref[1, rows, :]], axis=0).astype(BF16)
        a = jnp.dot(m1_ref[...], s, preferred_element_type=F32)
        wr_ref[pl.ds(b, n1, stride=FFT_SLAB), :] = a[:n1]
        wi_ref[pl.ds(b, n1, stride=FFT_SLAB), :] = a[n1:]
        return carry

    lax.fori_loop(0, FFT_SLAB, first, 0, unroll=FFT_UNROLL)

    def slab(c, carry):
        r0 = pl.multiple_of(c * FFT_SLAB, FFT_SLAB)
        rows = pl.ds(r0, FFT_SLAB)
        ar, ai = wr_ref[rows, :], wi_ref[rows, :]
        tr, ti = _lane_replicated(twr_ref, c, ct), _lane_replicated(twi_ref, c, ct)
        p = jnp.concatenate([ar * tr - ai * ti, ar * ti + ai * tr], axis=0).astype(BF16)
        x = jnp.dot(m2_ref[...], p, preferred_element_type=F32)
        xr, xi = x[:FFT_SLAB], x[FFT_SLAB:]
        kr, ki = kr_ref[rows, :], ki_ref[rows, :]
        y = jnp.concatenate([xr * kr - xi * ki, xr * ki + xi * kr], axis=0).astype(BF16)
        bb = jnp.dot(m2i_ref[...], y, preferred_element_type=F32)
        br, bi = bb[:FFT_SLAB], bb[FFT_SLAB:]
        wr_ref[rows, :] = br * tr + bi * ti
        wi_ref[rows, :] = bi * tr - br * ti
        return carry

    lax.fori_loop(0, n1, slab, 0, unroll=FFT_UNROLL)

    def last(b, carry):
        rows = pl.ds(b, n1, stride=FFT_SLAB)
        s = jnp.concatenate([wr_ref[rows, :], wi_ref[rows, :]], axis=0).astype(BF16)
        y = jnp.dot(m1i_ref[...], s, preferred_element_type=F32)
        orow = pl.ds(b, na, stride=FFT_SLAB)
        bias = bias_ref[...]
        o_ref[0, orow, :] = y[:na] + z_ref[0, orow, :] * bias
        o_ref[1, orow, :] = y[na:] + z_ref[1, orow, :] * bias
        return carry

    lax.fori_loop(0, FFT_SLAB, last, 0, unroll=FFT_UNROLL)


def _fftconv(z, kf_re, kf_im, bias, ct=128):
    bsz, n, ch = z.shape
    n_total = 2 * n
    n1, na = n_total // FFT_SLAB, n // FFT_SLAB
    cs = _dft_consts(n_total, n)
    mats = [jnp.asarray(cs[name], BF16) for name in ("m1", "m2", "m2i", "m1i")]
    tws = [jnp.asarray(cs[name], F32) for name in ("twr", "twi")]
    full = lambda a: pl.BlockSpec(a.shape, lambda p, j: (0,) * a.ndim)
    single = pl.Buffered(1)
    zspec = pl.BlockSpec((2, n, ct), lambda p, j: (p, 0, j), pipeline_mode=single)
    kspec = pl.BlockSpec((n_total, ct), lambda p, j: (0, j), pipeline_mode=single)
    return pl.pallas_call(
        functools.partial(_fftconv_body, n1=n1, na=na),
        grid=(bsz // 2, ch // ct),
        in_specs=[zspec, kspec, kspec, pl.BlockSpec((1, ct), lambda p, j: (0, j))]
                 + [full(a) for a in mats] + [full(a) for a in tws],
        out_specs=pl.BlockSpec((2, n, ct), lambda p, j: (p, 0, j), pipeline_mode=single),
        out_shape=jax.ShapeDtypeStruct((bsz, n, ch), F32),
        scratch_shapes=[pltpu.VMEM((n_total, ct), F32), pltpu.VMEM((n_total, ct), F32)],
        compiler_params=_cparams(("parallel", "parallel")),
        name="fftconv",
    )(z, kf_re, kf_im, bias[None].astype(F32), *mats, *tws)


def _hyena_kernels(n, w1, b1, f1, w2, b2, f2, w3, width):
    t = jnp.linspace(0.0, 1.0, n, dtype=F32)[:, None]
    bands = (HY_EMB_DIM - 1) // 2
    w = 2.0 * math.pi * jnp.arange(n, dtype=F32)[:, None] / n
    f = jnp.linspace(1e-4, bands - 1, bands, dtype=F32)[None, :]
    z = jnp.concatenate([t, jnp.cos(f * w), -jnp.sin(f * w)], axis=-1)
    h = jnp.sin(f1.astype(F32) * (jnp.dot(z, w1.astype(F32), precision=HI) + b1.astype(F32)))
    h = jnp.sin(f2.astype(F32) * (jnp.dot(h, w2.astype(F32), precision=HI) + b2.astype(F32)))
    max_decay = math.log(HY_TARGET) / HY_FAST_DECAY
    min_decay = math.log(HY_TARGET) / HY_SLOW_DECAY
    deltas = jnp.abs(jnp.linspace(min_decay, max_decay, width, dtype=F32))[None, :]
    h_rev, t_rev = h[1:][::-1], t[1:][::-1]
    w3r = w3.astype(F32).reshape(w3.shape[0], HY_ORDER, 2, width)
    out = []
    for o in range(HY_ORDER):
        fwd = jnp.dot(h, w3r[:, o, 0], precision=HI) * jnp.exp(-t * deltas)
        bwd = jnp.dot(h_rev, w3r[:, o, 1], precision=HI) * jnp.exp(-t_rev * deltas)
        k = jnp.concatenate([fwd, jnp.zeros((1, width), F32), bwd], axis=0)
        out.append(k * lax.rsqrt(jnp.sum(k * k, axis=0, keepdims=True) + NORM_EPS))
    return out


def _small_conv_body(z_ref, k_ref, bias_ref, mfk_ref, mf_ref, mi_ref, o_ref):
    bsz, n, _ = z_ref.shape
    nt = 2 * n
    kf = jnp.dot(mfk_ref[...], k_ref[...], precision=HI, preferred_element_type=F32)
    kr, ki = kf[:nt], kf[nt:]
    bias = bias_ref[...]
    for p in range(bsz // 2):
        z0, z1 = z_ref[2 * p], z_ref[2 * p + 1]
        x = jnp.dot(mf_ref[...], jnp.concatenate([z0, z1], axis=0).astype(BF16), preferred_element_type=F32)
        xr, xi = x[:nt], x[nt:]
        y = jnp.concatenate([xr * kr - xi * ki, xr * ki + xi * kr], axis=0).astype(BF16)
        w = jnp.dot(mi_ref[...], y, preferred_element_type=F32)
        o_ref[2 * p] = w[:n] + z0 * bias
        o_ref[2 * p + 1] = w[n:] + z1 * bias


def _small_conv(z, k, bias, ct=256):
    bsz, n, ch = z.shape
    nt = 2 * n
    idx = np.arange(nt)
    f = np.exp(-2j * np.pi * np.outer(idx, idx) / nt)
    mfk = jnp.asarray(np.concatenate([f.real, f.imag], axis=0), F32)
    mf = jnp.asarray(_stack_complex(f[:, :n]), BF16)
    mi = jnp.asarray(_stack_complex(np.conj(f).T[:n, :] / nt), BF16)
    full = lambda a: pl.BlockSpec(a.shape, lambda j: (0,) * a.ndim)
    return pl.pallas_call(
        _small_conv_body,
        grid=(ch // ct,),
        in_specs=[pl.BlockSpec((bsz, n, ct), lambda j: (0, 0, j)), pl.BlockSpec((nt, ct), lambda j: (0, j)),
                  pl.BlockSpec((1, ct), lambda j: (0, j)), full(mfk), full(mf), full(mi)],
        out_specs=pl.BlockSpec((bsz, n, ct), lambda j: (0, 0, j)),
        out_shape=jax.ShapeDtypeStruct((bsz, n, ch), F32),
        compiler_params=_cparams(("parallel",)),
        name="small_conv",
    )(z, k, bias[None].astype(F32), mfk, mf, mi)


def _short_conv_body(pc_ref, pp_ref, pn_ref, w_ref, b_ref, v_out, x1_out, x2_out, *, width):
    i = pl.program_id(1)
    cur = pc_ref[0]
    tm = cur.shape[0]
    up = jnp.where(i == 0, 0.0, pp_ref[0, 7:8, :])
    dn = jnp.where(i == pl.num_programs(1) - 1, 0.0, pn_ref[0, 0:1, :])
    prev = jnp.concatenate([up, cur[:tm - 1]], axis=0)
    nxt = jnp.concatenate([cur[1:], dn], axis=0)
    w = w_ref[...]
    y = prev * w[0:1] + cur * w[1:2] + nxt * w[2:3] + b_ref[...]
    v_out[0] = y[:, :width]
    x1_out[0] = y[:, width:2 * width]
    x2_out[0] = y[:, 2 * width:]


def _short_conv(p, short_w, short_b, tm):
    b, n, w3 = p.shape
    width = w3 // 3
    tm = min(tm, n)
    g = tm // 8
    ng = n // 8
    ospec = pl.BlockSpec((1, tm, width), lambda bi, i: (bi, i, 0))
    oshape = jax.ShapeDtypeStruct((b, n, width), F32)
    return pl.pallas_call(
        functools.partial(_short_conv_body, width=width),
        grid=(b, n // tm),
        in_specs=[pl.BlockSpec((1, tm, w3), lambda bi, i: (bi, i, 0)),
                  pl.BlockSpec((1, 8, w3), lambda bi, i: (bi, jnp.maximum(i * g - 1, 0), 0)),
                  pl.BlockSpec((1, 8, w3), lambda bi, i: (bi, jnp.minimum((i + 1) * g, ng - 1), 0)),
                  pl.BlockSpec((HY_SHORT, w3), lambda bi, i: (0, 0)),
                  pl.BlockSpec((1, w3), lambda bi, i: (0, 0))],
        out_specs=[ospec, ospec, ospec], out_shape=[oshape, oshape, oshape],
        compiler_params=_cparams(("parallel", "parallel")),
        name="short_conv",
    )(p, p, p, short_w, short_b[None])


def _mul_body(a_ref, b_ref, o_ref):
    o_ref[...] = a_ref[...] * b_ref[...]


def _mul(a, b, tm):
    bsz, n, d = a.shape
    tm = min(tm, n)
    spec = pl.BlockSpec((1, tm, d), lambda bi, i: (bi, i, 0))
    return pl.pallas_call(
        _mul_body, grid=(bsz, n // tm), in_specs=[spec, spec], out_specs=spec,
        out_shape=jax.ShapeDtypeStruct(a.shape, a.dtype),
        compiler_params=_cparams(("parallel", "parallel")), name="gate_mul",
    )(a, b)


def _hyena_core(p, short_w, short_b, w1, b1, f1, w2, b2, f2, w3, fbias, tm):
    n = p.shape[1]
    width = p.shape[2] // 3
    v, x1, x2 = _short_conv(p, short_w, short_b, tm)
    ks = _hyena_kernels(n, w1, b1, f1, w2, b2, f2, w3, width)
    if 2 * n // FFT_SLAB >= FFT_SLAB:
        conv = lambda z, o: _fftconv(z, *_filter_spectrum(ks[o]), fbias[o])
    else:
        conv = lambda z, o: _small_conv(z, ks[o], fbias[o])
    w0 = conv(v, 0)
    w1_ = conv(_mul(x1, w0, tm), 1)
    return w1_, x2


def _sink_rows(sink, hkv, grp, tq):
    return jnp.repeat(sink.astype(F32).reshape(hkv, grp), tq, axis=1)[..., None]


def kernel(x, c, ctx, c_ctx, ada_w, ada_b, norm_mix_g, norm_ffn_g, router_w, exp_w_gate, exp_w_up, exp_w_down,
           a_w_in, a_w_out, a_q_g, a_k_g, a_sink, b_w_in, b_short_w, b_short_b, b_w1, b_b1, b_f1, b_w2, b_b2,
           b_f2, b_w3, b_bias, b_w_out, c_w_in, c_w_out, c_q_g, c_k_g, c_rpb):
    bsz, n, d = x.shape
    nctx = ctx.shape[1]
    depth = ada_w.shape[0]
    ne = router_w.shape[2]
    cap = EC_CAPACITY * n // ne
    cap_c = EC_CAPACITY * nctx // ne
    tm = 256

    pad_rows = (-(bsz + 1)) % 8
    rows = jnp.concatenate([c, c_ctx[None], jnp.zeros((pad_rows, d), F32)], axis=0)
    mod_all = _ada_mod(rows, ada_w, ada_b)

    h, hc = x, ctx
    for i in range(depth):
        last = i == depth - 1
        kind, j = i % N_MIXERS, i // N_MIXERS
        mod = mod_all[i, :bsz].reshape(bsz, 1, 6, d)
        sh1, sc1, g1, sh2, sc2, g2 = [mod[:, :, t] for t in range(6)]
        modc = jnp.broadcast_to(mod_all[i, bsz].reshape(1, 1, 6, d), (bsz, 1, 6, d))
        csh1, csc1, cg1, csh2, csc2, cg2 = [modc[:, :, t] for t in range(6)]
        yc = o_gate = yc_gate = None
        if kind == 0:
            hkv, grp = A_KV_HEADS, a_w_in.shape[2] // HEAD_DIM // A_KV_HEADS - 2
            qw, kvw = hkv * grp * HEAD_DIM, hkv * HEAD_DIM
            w_in = a_w_in[j].astype(BF16)
            w_out = a_w_out[j].astype(BF16)
            q_rot, q_pl, k_rot, v = _proj_attn(h, sh1, sc1, norm_mix_g[i], w_in, a_q_g[j], a_k_g[j],
                                               qw, kvw, True, tm)
            qc, kc, vc = _proj_attn(hc, csh1, csc1, norm_mix_g[i], w_in, a_q_g[j], a_k_g[j],
                                    qw, kvw, False, tm)
            o = _local_attn(q_rot, q_pl, k_rot, v, kc, vc, _window_bias(n),
                            _sink_rows(a_sink[j], hkv, grp, A_BLOCK), hkv, grp, A_BLOCK)
            if not last:
                yc = _ctx_attn(qc, kc, vc, _sink_rows(a_sink[j], hkv, grp, nctx), hkv, grp)
        elif kind == 1:
            w_in = b_w_in[j].astype(BF16)
            w_out = b_w_out[j].astype(BF16)
            hy = (b_short_w[j], b_short_b[j], b_w1[j], b_b1[j], b_f1[j], b_w2[j], b_b2[j], b_f2[j], b_w3[j],
                  b_bias[j])
            o, o_gate = _hyena_core(_proj_plain(h, sh1, sc1, norm_mix_g[i], w_in, tm), *hy, tm)
            if not last:
                yc, yc_gate = _hyena_core(_proj_plain(hc, csh1, csc1, norm_mix_g[i], w_in, tm), *hy, tm)
        else:
            nh = c_w_in.shape[2] // HEAD_DIM // 3
            hw = nh * HEAD_DIM
            w_in = c_w_in[j].astype(BF16)
            w_out = c_w_out[j].astype(BF16)
            q, k, v = _proj_attn(h, sh1, sc1, norm_mix_g[i], w_in, c_q_g[j], c_k_g[j], hw, hw, False, tm)
            qc, kc, vc = _proj_attn(hc, csh1, csc1, norm_mix_g[i], w_in, c_q_g[j], c_k_g[j], hw, hw, False, tm)
            o = _local_attn(q, q, k, v, kc, vc, _neighbourhood_bias(c_rpb[j], n), None, nh, 1, C_BLOCK)
            if not last:
                yc = _ctx_attn(qc, kc, vc, None, nh, 1)

        rwt = router_w[i].T
        h1, u2, aff = _outproj(o, w_out, h, g1, norm_ffn_g[i], sh2, sc2, rwt, tm, o_gate)
        gl, il = lax.top_k(aff, cap)
        rows_l = il + (jnp.arange(bsz) * n)[:, None, None]
        rows_all = rows_l.transpose(1, 0, 2).reshape(ne, bsz * cap)
        gates_all = gl.transpose(1, 0, 2).reshape(ne, bsz * cap)
        u_all = u2.reshape(bsz * n, d)
        if not last:
            hc1, uc2, affc = _outproj(yc, w_out, hc, cg1, norm_ffn_g[i], csh2, csc2, rwt, tm, yc_gate)
            gcx, icx = lax.top_k(affc, cap_c)
            rows_c = icx + (bsz * n + jnp.arange(bsz) * nctx)[:, None, None]
            rows_all = jnp.concatenate([rows_all, rows_c.transpose(1, 0, 2).reshape(ne, bsz * cap_c)], axis=1)
            gates_all = jnp.concatenate([gates_all, gcx.transpose(1, 0, 2).reshape(ne, bsz * cap_c)], axis=1)
            u_all = jnp.concatenate([u_all, uc2.reshape(bsz * nctx, d)], axis=0)
        r = rows_all.shape[1]
        xs = u_all[rows_all]
        ye = _expert_ffn(xs, gates_all[..., None], exp_w_gate, exp_w_up, exp_w_down, i, r // 4, 512)
        moe = jnp.zeros((u_all.shape[0], d), F32).at[rows_all.reshape(-1)].add(ye.reshape(-1, d))
        h = h1 + g2 * moe[:bsz * n].reshape(bsz, n, d)
        if not last:
            hc = hc1 + cg2 * moe[bsz * n:].reshape(bsz, nctx, d)
    return h
```

```python
import functools
import math

import numpy as np
import jax
import jax.numpy as jnp
from jax import lax
from jax.experimental import pallas as pl
from jax.experimental.pallas import tpu as pltpu

F32 = jnp.float32
BF16 = jnp.bfloat16
HI = lax.Precision.HIGHEST

GRID_W = 64
HEAD_DIM = 64
NORM_EPS = 1e-6
N_MIXERS = 3
A_KV_HEADS = 4
A_BLOCK = 128
ROPE_BASE = 10000.0
HY_ORDER = 2
HY_EMB_DIM = 33
HY_SHORT = 3
HY_FAST_DECAY = 0.3
HY_SLOW_DECAY = 1.5
HY_TARGET = 1e-2
NA_ROWS = 8
NA_COLS = 16
HEAD_GROUP = 4
C_BLOCK = 256
N_EXPERTS = 16
EC_CAPACITY = 2
NEG = -1e30

LANES = 128
VMEM_LIMIT = 56 * 1024 * 1024


def _cparams(sem):
    return pltpu.CompilerParams(dimension_semantics=sem, vmem_limit_bytes=VMEM_LIMIT)


def _sigmoid(x):
    return 1.0 / (1.0 + jnp.exp(-x))


def _modulate(h, g, shift, scale):
    ms = jnp.mean(h * h, axis=-1, keepdims=True)
    y = h * lax.rsqrt(ms + NORM_EPS)
    return (y * g) * (1.0 + scale) + shift


def _mod_body(s_ref, w_ref, b_ref, o_ref):
    s = s_ref[...]
    s = s * _sigmoid(s)
    o_ref[0] = jnp.dot(s, w_ref[0], precision=HI, preferred_element_type=F32) + b_ref[0]


def _ada_mod(rows, ada_w, ada_b):
    depth, d, n6 = ada_w.shape
    r = rows.shape[0]
    tn = 1536
    return pl.pallas_call(
        _mod_body,
        grid=(depth, n6 // tn),
        in_specs=[pl.BlockSpec((r, d), lambda l, j: (0, 0)),
                  pl.BlockSpec((1, d, tn), lambda l, j: (l, 0, j)),
                  pl.BlockSpec((1, 1, tn), lambda l, j: (l, 0, j))],
        out_specs=pl.BlockSpec((1, r, tn), lambda l, j: (l, 0, j)),
        out_shape=jax.ShapeDtypeStruct((depth, r, n6), F32),
        compiler_params=_cparams(("arbitrary", "arbitrary")),
        name="ada_mod",
    )(rows, ada_w, ada_b.reshape(depth, 1, n6))


def _dot_onehot(x, onehot_bf):
    x0 = x.astype(BF16)
    r1 = x - x0.astype(F32)
    x1 = r1.astype(BF16)
    x2 = (r1 - x1.astype(F32)).astype(BF16)
    return (jnp.dot(x0, onehot_bf, preferred_element_type=F32) + jnp.dot(x1, onehot_bf, preferred_element_type=F32)
            + jnp.dot(x2, onehot_bf, preferred_element_type=F32))


def _head_norm(x, gsum, gexp, gain):
    ss = _dot_onehot(x * x, gsum)
    r = lax.rsqrt(ss * (1.0 / HEAD_DIM) + NORM_EPS)
    rb = _dot_onehot(r, gexp)
    return (x * rb) * gain


def _rope(x, cos, sin_signed):
    rows, w = x.shape
    lane = lax.broadcasted_iota(jnp.int32, (rows, LANES), 1)
    first = (lane % 32) < 16
    outs = []
    for c in range(w // LANES):
        xc = x[:, c * LANES:(c + 1) * LANES]
        partner = jnp.where(first, pltpu.roll(xc, LANES - 16, 1), pltpu.roll(xc, 16, 1))
        outs.append(xc * cos + partner * sin_signed)
    return jnp.concatenate(outs, axis=1)


def _proj_attn_body(*refs, qw, kvw, rope):
    h_ref, sh_ref, sc_ref, g_ref, w_ref, qg_ref, kg_ref, gsq_ref, geq_ref, gsk_ref, gek_ref = refs[:11]
    u = _modulate(h_ref[0], g_ref[...], sh_ref[0], sc_ref[0])
    p = jnp.dot(u.astype(BF16), w_ref[...], preferred_element_type=F32)
    q = _head_norm(p[:, :qw], gsq_ref[...], geq_ref[...], qg_ref[...]) * (HEAD_DIM ** -0.5)
    k = _head_norm(p[:, qw:qw + kvw], gsk_ref[...], gek_ref[...], kg_ref[...])
    v = p[:, qw + kvw:]
    if rope:
        cos_ref, sin_ref, qr_out, qp_out, kr_out, v_out = refs[11:]
        cos, sin = cos_ref[...], sin_ref[...]
        qr_out[0] = _rope(q, cos, sin).astype(BF16)
        qp_out[0] = q.astype(BF16)
        kr_out[0] = _rope(k, cos, sin).astype(BF16)
        v_out[0] = v.astype(BF16)
    else:
        q_out, k_out, v_out = refs[11:]
        q_out[0] = q.astype(BF16)
        k_out[0] = k.astype(BF16)
        v_out[0] = v.astype(BF16)


def _group_mats(w):
    nh = w // HEAD_DIM
    gs = np.zeros((w, LANES), np.float32)
    gs[np.arange(w), np.arange(w) // HEAD_DIM] = 1.0
    return jnp.asarray(gs, BF16), jnp.asarray(gs.T.copy(), BF16)


def _rope_tables(n):
    t = jnp.arange(n)
    row = (t // GRID_W).astype(F32)
    col = (t % GRID_W).astype(F32)
    axis_dim = HEAD_DIM // 2
    inv_freq = 1.0 / (ROPE_BASE ** (jnp.arange(0, axis_dim, 2, dtype=F32) / axis_dim))
    ang_r = row[:, None] * inv_freq
    ang_c = col[:, None] * inv_freq
    cos64 = jnp.concatenate([jnp.cos(ang_r), jnp.cos(ang_r), jnp.cos(ang_c), jnp.cos(ang_c)], axis=-1)
    sin64 = jnp.concatenate([-jnp.sin(ang_r), jnp.sin(ang_r), -jnp.sin(ang_c), jnp.sin(ang_c)], axis=-1)
    return jnp.tile(cos64, (1, 2)), jnp.tile(sin64, (1, 2))


def _proj_attn(h, shift, scale, g, w_bf, q_g, k_g, qw, kvw, rope, tm):
    b, n, d = h.shape
    nout = w_bf.shape[1]
    tm = min(tm, n)
    gsq, geq = _group_mats(qw)
    gsk, gek = _group_mats(kvw)
    qg = jnp.tile(q_g, qw // HEAD_DIM)[None]
    kg = jnp.tile(k_g, kvw // HEAD_DIM)[None]
    full = lambda a: pl.BlockSpec(a.shape, lambda bi, i: (0,) * a.ndim)
    vec = pl.BlockSpec((1, 1, d), lambda bi, i: (bi, 0, 0))
    args = [h, shift, scale, g[None], w_bf, qg, kg, gsq, geq, gsk, gek]
    in_specs = [pl.BlockSpec((1, tm, d), lambda bi, i: (bi, i, 0)), vec, vec, full(args[3]), full(w_bf),
                full(qg), full(kg), full(gsq), full(geq), full(gsk), full(gek)]
    ospec = lambda w: pl.BlockSpec((1, tm, w), lambda bi, i: (bi, i, 0))
    oshape = lambda w: jax.ShapeDtypeStruct((b, n, w), BF16)
    if rope:
        cos, sin = _rope_tables(n)
        args += [cos, sin]
        in_specs += [pl.BlockSpec((tm, LANES), lambda bi, i: (i, 0))] * 2
        out_specs = [ospec(qw), ospec(qw), ospec(kvw), ospec(kvw)]
        out_shape = [oshape(qw), oshape(qw), oshape(kvw), oshape(kvw)]
    else:
        out_specs = [ospec(qw), ospec(kvw), ospec(kvw)]
        out_shape = [oshape(qw), oshape(kvw), oshape(kvw)]
    return pl.pallas_call(
        functools.partial(_proj_attn_body, qw=qw, kvw=kvw, rope=rope),
        grid=(b, n // tm), in_specs=in_specs, out_specs=out_specs, out_shape=out_shape,
        compiler_params=_cparams(("parallel", "parallel")),
        name="proj_attn_rope" if rope else "proj_attn",
    )(*args)


def _proj_plain_body(h_ref, sh_ref, sc_ref, g_ref, w_ref, o_ref):
    u = _modulate(h_ref[0], g_ref[...], sh_ref[0], sc_ref[0])
    o_ref[0] = jnp.dot(u.astype(BF16), w_ref[...], preferred_element_type=F32)


def _proj_plain(h, shift, scale, g, w_bf, tm):
    b, n, d = h.shape
    nout = w_bf.shape[1]
    tm = min(tm, n)
    vec = pl.BlockSpec((1, 1, d), lambda bi, i: (bi, 0, 0))
    return pl.pallas_call(
        _proj_plain_body,
        grid=(b, n // tm),
        in_specs=[pl.BlockSpec((1, tm, d), lambda bi, i: (bi, i, 0)), vec, vec,
                  pl.BlockSpec((1, d), lambda bi, i: (0, 0)),
                  pl.BlockSpec((d, nout), lambda bi, i: (0, 0))],
        out_specs=pl.BlockSpec((1, tm, nout), lambda bi, i: (bi, i, 0)),
        out_shape=jax.ShapeDtypeStruct((b, n, nout), F32),
        compiler_params=_cparams(("parallel", "parallel")),
        name="proj_plain",
    )(h, shift, scale, g[None], w_bf)


def _local_attn_body(*refs, hkv, grp, tq, use_sink, head_bias):
    (ql_ref, qc_ref, kp_ref, kc_ref, kn_ref, vp_ref, vc_ref, vn_ref, kx_ref, vx_ref, bias_ref) = refs[:11]
    if use_sink:
        sink_ref, o_ref = refs[11:]
    else:
        (o_ref,) = refs[11:]
    dh = HEAD_DIM
    nt = (((1,), (1,)), ((), ()))
    qcols = lambda hh: [slice((hh * grp + g) * dh, (hh * grp + g + 1) * dh) for g in range(grp)]
    for h0 in range(0, hkv, HEAD_GROUP):
        heads = range(h0, min(h0 + HEAD_GROUP, hkv))
        s, vall = [], []
        for hh in heads:
            ks = slice(hh * dh, (hh + 1) * dh)
            q_l = jnp.concatenate([ql_ref[0, :, c] for c in qcols(hh)], axis=0)
            q_c = jnp.concatenate([qc_ref[0, :, c] for c in qcols(hh)], axis=0)
            kw = jnp.concatenate([kp_ref[0, :, ks], kc_ref[0, :, ks], kn_ref[0, :, ks]], axis=0)
            s_loc = lax.dot_general(q_l, kw, nt, preferred_element_type=F32)
            s_loc = s_loc + bias_ref[0, hh if head_bias else 0]
            s_ctx = lax.dot_general(q_c, kx_ref[0, :, ks], nt, preferred_element_type=F32)
            s.append(jnp.concatenate([s_loc, s_ctx], axis=1))
            v = jnp.concatenate([vp_ref[0, :, ks], vc_ref[0, :, ks], vn_ref[0, :, ks], vx_ref[0, :, ks]], axis=0)
            vall.append(jnp.concatenate([v, jnp.ones_like(v)], axis=1))
        m = [jnp.max(x, axis=-1, keepdims=True) for x in s]
        if use_sink:
            sk = [sink_ref[hh] for hh in heads]
            m = [jnp.maximum(a, b) for a, b in zip(m, sk)]
        p = [jnp.exp(x - a).astype(BF16) for x, a in zip(s, m)]
        ox = [jnp.dot(x, v, preferred_element_type=F32) for x, v in zip(p, vall)]
        den = [x[:, dh:dh + 1] for x in ox]
        if use_sink:
            den = [d + jnp.exp(b - a) for d, a, b in zip(den, m, sk)]
        o = [x[:, :dh] * (1.0 / d) for x, d in zip(ox, den)]
        for hh, oh in zip(heads, o):
            for g, c in enumerate(qcols(hh)):
                o_ref[0, :, c] = oh[g * tq:(g + 1) * tq].astype(BF16)


def _local_attn(q_loc, q_ctx, k, v, kx, vx, bias, sink_rows, hkv, grp, tq):
    b, n, qw = q_loc.shape
    kvw = k.shape[2]
    nb = n // tq
    nctx = kx.shape[1]
    head_bias = bias.shape[1] > 1
    use_sink = sink_rows is not None
    qspec = pl.BlockSpec((1, tq, qw), lambda bi, i: (bi, i, 0))
    prev = pl.BlockSpec((1, tq, kvw), lambda bi, i: (bi, jnp.maximum(i - 1, 0), 0))
    cur = pl.BlockSpec((1, tq, kvw), lambda bi, i: (bi, i, 0))
    nxt = pl.BlockSpec((1, tq, kvw), lambda bi, i: (bi, jnp.minimum(i + 1, nb - 1), 0))
    xspec = pl.BlockSpec((1, nctx, kvw), lambda bi, i: (bi, 0, 0))
    bspec = pl.BlockSpec((1,) + bias.shape[1:],
                         lambda bi, i: (jnp.where(i == 0, 0, jnp.where(i == nb - 1, 2, 1)), 0, 0, 0))
    args = [q_loc, q_ctx, k, k, k, v, v, v, kx, vx, bias]
    in_specs = [qspec, qspec, prev, cur, nxt, prev, cur, nxt, xspec, xspec, bspec]
    if use_sink:
        args.append(sink_rows)
        in_specs.append(pl.BlockSpec(sink_rows.shape, lambda bi, i: (0, 0, 0)))
    return pl.pallas_call(
        functools.partial(_local_attn_body, hkv=hkv, grp=grp, tq=tq, use_sink=use_sink, head_bias=head_bias),
        grid=(b, nb), in_specs=in_specs,
        out_specs=pl.BlockSpec((1, tq, qw), lambda bi, i: (bi, i, 0)),
        out_shape=jax.ShapeDtypeStruct((b, n, qw), BF16),
        compiler_params=_cparams(("parallel", "parallel")),
        name="local_attn_sink" if use_sink else "local_attn",
    )(*args)


def _window_bias(n):
    tq = A_BLOCK
    grp = 4
    qi = np.arange(tq)[:, None]
    kj = np.arange(3 * tq)[None, :] - tq
    band = np.abs(qi - kj) <= A_BLOCK
    nb = n // tq
    out = []
    for which in range(3):
        blk = {0: 0, 1: min(1, nb - 1), 2: nb - 1}[which]
        pos = blk * tq + kj
        ok = band & (pos >= 0) & (pos < n)
        out.append(np.tile(np.where(ok, 0.0, NEG).astype(np.float32), (grp, 1)))
    return jnp.asarray(np.stack(out)[:, None])


def _nbr_bias_body(rpb_ref, colok_ref, o_ref, *, row_ok):
    gw = GRID_W
    col_ok = colok_ref[...] > 0.5
    neg = jnp.full((gw, gw), NEG, F32)
    toep = []
    for ro in range(2 * NA_ROWS - 1):
        x = jnp.broadcast_to(rpb_ref[0, ro:ro + 1, :], (gw, LANES))
        t = pltpu.roll(x, LANES - (NA_COLS - 1), 1, stride=1, stride_axis=0)[:, :gw]
        toep.append(jnp.where(col_ok, t, NEG))
    rq = C_BLOCK // gw
    for which in range(3):
        rows_out = []
        for qa in range(rq):
            blocks = [toep[(ka - rq) - qa + (NA_ROWS - 1)] if row_ok[which][qa][ka] else neg
                      for ka in range(3 * rq)]
            rows_out.append(jnp.concatenate(blocks, axis=1))
        o_ref[which, 0] = jnp.concatenate(rows_out, axis=0)


def _neighbourhood_bias(rpb, n):
    nh = rpb.shape[0]
    rows = n // GRID_W
    kr = min(NA_ROWS, rows)
    rq = C_BLOCK // GRID_W
    nb = n // C_BLOCK
    assert nb >= 3 and rpb.shape[1:] == (2 * NA_ROWS - 1, 2 * NA_COLS - 1)
    qc = np.arange(GRID_W)
    c0 = np.clip(qc - NA_COLS // 2, 0, GRID_W - NA_COLS)
    col_ok = ((qc[None, :] >= c0[:, None]) & (qc[None, :] < c0[:, None] + NA_COLS)).astype(np.float32)
    row_ok = []
    for which in range(3):
        blk = {0: 0, 1: 1, 2: nb - 1}[which]
        qr = blk * rq + np.arange(rq)
        kr_abs = blk * rq + np.arange(3 * rq) - rq
        r0 = np.clip(qr - kr // 2, 0, rows - kr)
        ok = (kr_abs[None, :] >= r0[:, None]) & (kr_abs[None, :] < r0[:, None] + kr)
        row_ok.append(tuple(tuple(bool(v) for v in row) for row in ok))
    rpb_pad = jnp.pad(rpb.astype(F32), ((0, 0), (0, 1), (0, LANES - rpb.shape[2])))
    return pl.pallas_call(
        functools.partial(_nbr_bias_body, row_ok=tuple(row_ok)),
        grid=(nh,),
        in_specs=[pl.BlockSpec((1, 2 * NA_ROWS, LANES), lambda hh: (hh, 0, 0)),
                  pl.BlockSpec((GRID_W, GRID_W), lambda hh: (0, 0))],
        out_specs=pl.BlockSpec((3, 1, C_BLOCK, 3 * C_BLOCK), lambda hh: (0, hh, 0, 0)),
        out_shape=jax.ShapeDtypeStruct((3, nh, C_BLOCK, 3 * C_BLOCK), F32),
        compiler_params=_cparams(("parallel",)),
        name="nbr_bias",
    )(rpb_pad, jnp.asarray(col_ok))


def _ctx_attn_body(*refs, hkv, grp, use_sink):
    q_ref, k_ref, v_ref = refs[:3]
    if use_sink:
        sink_ref, o_ref = refs[3:]
    else:
        (o_ref,) = refs[3:]
    dh = HEAD_DIM
    nq = q_ref.shape[1]
    nt = (((1,), (1,)), ((), ()))
    for hh in range(hkv):
        ks = slice(hh * dh, (hh + 1) * dh)
        qcols = [slice((hh * grp + g) * dh, (hh * grp + g + 1) * dh) for g in range(grp)]
        q = jnp.concatenate([q_ref[0, :, c] for c in qcols], axis=0)
        s = lax.dot_general(q, k_ref[0, :, ks], nt, preferred_element_type=F32)
        m = jnp.max(s, axis=-1, keepdims=True)
        if use_sink:
            sk = sink_ref[hh]
            m = jnp.maximum(m, sk)
        e = jnp.exp(s - m)
        den = jnp.sum(e, axis=-1, keepdims=True)
        if use_sink:
            den = den + jnp.exp(sk - m)
        o = jnp.dot((e * (1.0 / den)).astype(BF16), v_ref[0, :, ks], preferred_element_type=F32)
        for g in range(grp):
            o_ref[0, :, qcols[g]] = o[g * nq:(g + 1) * nq].astype(BF16)


def _ctx_attn(q, k, v, sink_rows, hkv, grp):
    b, nq, qw = q.shape
    kvw = k.shape[2]
    use_sink = sink_rows is not None
    args = [q, k, v]
    in_specs = [pl.BlockSpec((1, nq, qw), lambda bi: (bi, 0, 0)),
                pl.BlockSpec((1, nq, kvw), lambda bi: (bi, 0, 0)),
                pl.BlockSpec((1, nq, kvw), lambda bi: (bi, 0, 0))]
    if use_sink:
        args.append(sink_rows)
        in_specs.append(pl.BlockSpec(sink_rows.shape, lambda bi: (0, 0, 0)))
    return pl.pallas_call(
        functools.partial(_ctx_attn_body, hkv=hkv, grp=grp, use_sink=use_sink),
        grid=(b,), in_specs=in_specs,
        out_specs=pl.BlockSpec((1, nq, qw), lambda bi: (bi, 0, 0)),
        out_shape=jax.ShapeDtypeStruct((b, nq, qw), BF16),
        compiler_params=_cparams(("parallel",)),
        name="ctx_attn_sink" if use_sink else "ctx_attn",
    )(*args)


def _outproj_body(*refs, gated):
    if gated:
        o_ref, og_ref, w_ref, h_ref, g1_ref, gn_ref, sh2_ref, sc2_ref, rwt_ref, h1_out, u2_out, aff_out = refs
        o = o_ref[0] * og_ref[0]
    else:
        o_ref, w_ref, h_ref, g1_ref, gn_ref, sh2_ref, sc2_ref, rwt_ref, h1_out, u2_out, aff_out = refs
        o = o_ref[0]
    y = jnp.dot(o.astype(BF16), w_ref[...], preferred_element_type=F32)
    h1 = h_ref[0] + g1_ref[0] * y
    h1_out[0] = h1
    u2 = _modulate(h1, gn_ref[...], sh2_ref[0], sc2_ref[0])
    u2_out[0] = u2.astype(BF16)
    lt = lax.dot_general(rwt_ref[...], u2, (((1,), (1,)), ((), ())), precision=HI, preferred_element_type=F32)
    e = jnp.exp(lt - jnp.max(lt, axis=0, keepdims=True))
    aff_out[0] = e / jnp.sum(e, axis=0, keepdims=True)


def _outproj(o, w_bf, h, g1, gn, sh2, sc2, rwt, tm, o_gate=None):
    b, n, d = h.shape
    kin = o.shape[2]
    ne = rwt.shape[0]
    tm = min(tm, n)
    vec = pl.BlockSpec((1, 1, d), lambda bi, i: (bi, 0, 0))
    ospec = pl.BlockSpec((1, tm, kin), lambda bi, i: (bi, i, 0))
    gated = o_gate is not None
    return pl.pallas_call(
        functools.partial(_outproj_body, gated=gated),
        grid=(b, n // tm),
        in_specs=([ospec, ospec] if gated else [ospec]) + [
                  pl.BlockSpec((kin, d), lambda bi, i: (0, 0)),
                  pl.BlockSpec((1, tm, d), lambda bi, i: (bi, i, 0)),
                  vec, pl.BlockSpec((1, d), lambda bi, i: (0, 0)), vec, vec,
                  pl.BlockSpec((ne, d), lambda bi, i: (0, 0))],
        out_specs=[pl.BlockSpec((1, tm, d), lambda bi, i: (bi, i, 0)),
                   pl.BlockSpec((1, tm, d), lambda bi, i: (bi, i, 0)),
                   pl.BlockSpec((1, ne, tm), lambda bi, i: (bi, 0, i))],
        out_shape=[jax.ShapeDtypeStruct((b, n, d), F32),
                   jax.ShapeDtypeStruct((b, n, d), BF16),
                   jax.ShapeDtypeStruct((b, ne, n), F32)],
        compiler_params=_cparams(("parallel", "parallel")),
        name="outproj_router",
    )(*((o, o_gate) if gated else (o,)), w_bf, h, g1, gn[None], sh2, sc2, rwt)


SEL_BLOCK = 128


def _bcast_lane_sum(x):
    return jnp.broadcast_to(jnp.sum(x, axis=-1, keepdims=True), x.shape)


def _block_prefix(tot):
    run = jnp.zeros(tot.shape[1:], F32)
    out = []
    for j in range(tot.shape[0]):
        out.append(run)
        run = run + tot[j]
    return jnp.stack(out)


def _select_body(aff_ref, tri_ref, trib_ref, idx_ref, gate_ref, inc_ref, off_ref, mk_ref, *, cap, ne):
    nb = aff_ref.shape[1] // ne
    x = aff_ref[0].reshape(nb, ne, SEL_BLOCK)
    bits = pltpu.bitcast(x, jnp.int32)
    count = lambda m: _bcast_lane_sum(jnp.sum(m.astype(F32), axis=0))

    def step(i, prefix):
        cand = prefix | jnp.left_shift(jnp.int32(1), 30 - i)
        return jnp.where(count(bits >= cand[None]) >= cap, cand, prefix)

    thr = lax.fori_loop(0, 31, step, jnp.zeros((ne, SEL_BLOCK), jnp.int32))
    gt = bits > thr[None]
    eq = bits == thr[None]
    need = cap - count(gt)

    def prefix_counts(m):
        mf = m.astype(BF16).reshape(nb * ne, SEL_BLOCK)
        inc = jnp.dot(mf, tri_ref[...], preferred_element_type=F32).reshape(nb, ne, SEL_BLOCK)
        tot = jnp.broadcast_to(inc[:, :, SEL_BLOCK - 1:], inc.shape)
        return inc, tot, _block_prefix(tot)

    e_inc, _, e_off = prefix_counts(eq)
    rank_eq = e_inc + e_off - eq.astype(F32)
    mask = gt | (eq & (rank_eq < need[None]))
    inc, tot, off = prefix_counts(mask)
    inc_ref[...] = inc.reshape(nb * ne, SEL_BLOCK)
    off_ref[...] = off.reshape(nb * ne, SEL_BLOCK)
    mk_ref[...] = mask.astype(F32).reshape(nb * ne, SEL_BLOCK)

    slot = lax.broadcasted_iota(jnp.int32, (cap, 1), 0).astype(F32)
    lane_nb = lax.broadcasted_iota(jnp.int32, (cap, nb), 1).astype(F32)
    lane = lax.broadcasted_iota(jnp.int32, (cap, SEL_BLOCK), 1).astype(F32)
    ones = jnp.ones((8, SEL_BLOCK), BF16)
    for e in range(ne):
        rows = pl.ds(e, nb, stride=ne)
        mk = mk_ref[rows, :].astype(BF16)
        cnt_row = lax.dot_general(ones, mk, (((1,), (1,)), ((), ())), preferred_element_type=F32)
        pinc_row = jnp.dot(cnt_row.astype(BF16), trib_ref[...], preferred_element_type=F32)[0:1]
        blk = jnp.sum((pinc_row <= slot).astype(F32), axis=-1, keepdims=True)
        onehot = (lane_nb == blk).astype(BF16)
        rhs = jnp.concatenate([inc_ref[rows, :], off_ref[rows, :], aff_ref[0, rows, :]], axis=1)
        r0 = rhs.astype(BF16)
        d1 = rhs - r0.astype(F32)
        r1 = d1.astype(BF16)
        r2 = (d1 - r1.astype(F32)).astype(BF16)
        got = (jnp.dot(onehot, r0, preferred_element_type=F32) + jnp.dot(onehot, r1, preferred_element_type=F32)
               + jnp.dot(onehot, r2, preferred_element_type=F32))
        rank = slot - got[:, SEL_BLOCK:SEL_BLOCK + 1]
        local = jnp.sum((got[:, :SEL_BLOCK] <= rank).astype(F32), axis=-1, keepdims=True)
        gate = jnp.sum(jnp.where(lane == local, got[:, 2 * SEL_BLOCK:], 0.0), axis=-1, keepdims=True)
        idx_ref[0, :, e:e + 1] = (blk * SEL_BLOCK + local).astype(jnp.int32)
        gate_ref[0, :, e:e + 1] = gate


def _select(aff, cap, nb_pad):
    bsz, ne, n = aff.shape
    pad = nb_pad * SEL_BLOCK - n
    a = jnp.pad(aff, ((0, 0), (0, 0), (0, pad))).reshape(bsz, ne, nb_pad, SEL_BLOCK).transpose(0, 2, 1, 3)
    a = a.reshape(bsz, nb_pad * ne, SEL_BLOCK)
    tri = jnp.asarray(np.triu(np.ones((SEL_BLOCK, SEL_BLOCK), np.float32)), BF16)
    trib = jnp.asarray(np.triu(np.ones((nb_pad, nb_pad), np.float32)), BF16)
    idx, gates = pl.pallas_call(
        functools.partial(_select_body, cap=cap, ne=ne),
        grid=(bsz,),
        in_specs=[pl.BlockSpec((1, nb_pad * ne, SEL_BLOCK), lambda b: (b, 0, 0)),
                  pl.BlockSpec(tri.shape, lambda b: (0, 0)), pl.BlockSpec(trib.shape, lambda b: (0, 0))],
        out_specs=[pl.BlockSpec((1, cap, ne), lambda b: (b, 0, 0))] * 2,
        out_shape=[jax.ShapeDtypeStruct((bsz, cap, ne), jnp.int32), jax.ShapeDtypeStruct((bsz, cap, ne), F32)],
        scratch_shapes=[pltpu.VMEM((nb_pad * ne, SEL_BLOCK), F32)] * 3,
        compiler_params=_cparams(("parallel",)),
        name="expert_select",
    )(a, tri, trib)
    return idx.transpose(0, 2, 1), gates.transpose(0, 2, 1)


def _expert_body(xs_ref, wg_ref, wu_ref, wd_ref, gate_ref, o_ref):
    f = pl.program_id(2)
    xs = xs_ref[0]
    a = jnp.dot(xs, wg_ref[...].astype(BF16), preferred_element_type=F32)
    bb = jnp.dot(xs, wu_ref[...].astype(BF16), preferred_element_type=F32)
    hid = (a * _sigmoid(a)) * bb
    y = jnp.dot(hid.astype(BF16), wd_ref[...].astype(BF16), preferred_element_type=F32)

    @pl.when(f == 0)
    def _():
        o_ref[0] = y

    @pl.when(f > 0)
    def _():
        o_ref[0] += y

    @pl.when(f == pl.num_programs(2) - 1)
    def _():
        o_ref[0] = o_ref[0] * gate_ref[0]


def _expert_ffn(xs, gates, w_gate, w_up, w_down, layer, tr, tf):
    ne, r, d = xs.shape
    ff = w_gate.shape[3]
    return pl.pallas_call(
        _expert_body,
        grid=(ne, r // tr, ff // tf),
        in_specs=[pl.BlockSpec((1, tr, d), lambda e, c, f: (e, c, 0)),
                  pl.BlockSpec((None, None, d, tf), lambda e, c, f: (layer, e, 0, f)),
                  pl.BlockSpec((None, None, d, tf), lambda e, c, f: (layer, e, 0, f)),
                  pl.BlockSpec((None, None, tf, d), lambda e, c, f: (layer, e, f, 0)),
                  pl.BlockSpec((1, tr, 1), lambda e, c, f: (e, c, 0))],
        out_specs=pl.BlockSpec((1, tr, d), lambda e, c, f: (e, c, 0)),
        out_shape=jax.ShapeDtypeStruct((ne, r, d), F32),
        compiler_params=_cparams(("parallel", "parallel", "arbitrary")),
        name="expert_ffn",
    )(xs, w_gate, w_up, w_down, gates)


FFT_SLAB = 128
FFT_UNROLL = 8


def _stack_complex(m):
    return np.block([[m.real, -m.imag], [m.imag, m.real]])


def _dft_consts(n_total, n_nonzero):
    n1 = n_total // FFT_SLAB
    na = n_nonzero // FFT_SLAB
    idx1 = np.arange(n1)
    f1 = np.exp(-2j * np.pi * np.outer(idx1, idx1) / n1)
    idx2 = np.arange(FFT_SLAB)
    f2 = np.exp(-2j * np.pi * np.outer(idx2, idx2) / FFT_SLAB)
    tw = np.exp(-2j * np.pi * np.outer(idx1, idx2) / n_total)
    return dict(
        m1=_stack_complex(f1[:, :na]),
        m1_real=np.concatenate([f1.real, f1.imag], axis=0),
        m2=_stack_complex(f2),
        m2i=_stack_complex(np.conj(f2).T / n_total),
        m1i=_stack_complex(np.conj(f1).T[:na, :]),
        twr=tw.real, twi=tw.imag)


def _lane_replicated(row_ref, c, ct):
    t = jnp.broadcast_to(row_ref[pl.ds(c, 1), :], (FFT_SLAB, FFT_SLAB)).T
    return t if ct == FFT_SLAB else jnp.tile(t, (1, ct // FFT_SLAB))


def _dot_split(mh_ref, ml_ref, x):
    xh = x.astype(BF16)
    xl = (x - xh.astype(F32)).astype(BF16)
    mh = mh_ref[...]
    return (jnp.dot(mh, xh, preferred_element_type=F32) + jnp.dot(mh, xl, preferred_element_type=F32)
            + jnp.dot(ml_ref[...], xh, preferred_element_type=F32))


def _spectrum_body(k_ref, m1h_ref, m1l_ref, m2h_ref, m2l_ref, twr_ref, twi_ref, or_ref, oi_ref, *, n1):
    ct = k_ref.shape[1]

    def first(b, carry):
        a = _dot_split(m1h_ref, m1l_ref, k_ref[pl.ds(b, n1, stride=FFT_SLAB), :])
        or_ref[pl.ds(b, n1, stride=FFT_SLAB), :] = a[:n1]
        oi_ref[pl.ds(b, n1, stride=FFT_SLAB), :] = a[n1:]
        return carry

    lax.fori_loop(0, FFT_SLAB, first, 0, unroll=FFT_UNROLL)

    def slab(c, carry):
        r0 = pl.multiple_of(c * FFT_SLAB, FFT_SLAB)
        ar, ai = or_ref[pl.ds(r0, FFT_SLAB), :], oi_ref[pl.ds(r0, FFT_SLAB), :]
        tr, ti = _lane_replicated(twr_ref, c, ct), _lane_replicated(twi_ref, c, ct)
        p = jnp.concatenate([ar * tr - ai * ti, ar * ti + ai * tr], axis=0)
        x = _dot_split(m2h_ref, m2l_ref, p)
        or_ref[pl.ds(r0, FFT_SLAB), :] = x[:FFT_SLAB]
        oi_ref[pl.ds(r0, FFT_SLAB), :] = x[FFT_SLAB:]
        return carry

    lax.fori_loop(0, n1, slab, 0, unroll=FFT_UNROLL)


def _split_bf16(m):
    hi = jnp.asarray(m, F32).astype(BF16)
    lo = (jnp.asarray(m, F32) - hi.astype(F32)).astype(BF16)
    return hi, lo


def _filter_spectrum(k, ct=128):
    n_total, ch = k.shape
    n1 = n_total // FFT_SLAB
    cs = _dft_consts(n_total, n_total)
    consts = [*_split_bf16(cs["m1_real"]), *_split_bf16(cs["m2"]),
              jnp.asarray(cs["twr"], F32), jnp.asarray(cs["twi"], F32)]
    full = lambda a: pl.BlockSpec(a.shape, lambda j: (0,) * a.ndim)
    col = pl.BlockSpec((n_total, ct), lambda j: (0, j))
    return pl.pallas_call(
        functools.partial(_spectrum_body, n1=n1),
        grid=(ch // ct,),
        in_specs=[col] + [full(a) for a in consts],
        out_specs=[col, col],
        out_shape=[jax.ShapeDtypeStruct((n_total, ch), F32)] * 2,
        compiler_params=_cparams(("parallel",)),
        name="filter_spectrum",
    )(k, *consts)


def _fftconv_body(z_ref, kr_ref, ki_ref, bias_ref, m1_ref, m2_ref, m2i_ref, m1i_ref, twr_ref, twi_ref,
                  o_ref, wr_ref, wi_ref, *, n1, na):
    ct = z_ref.shape[2]

    def first(b, carry):
        rows = pl.ds(b, na, stride=FFT_SLAB)
        s = jnp.concatenate([z_ref[0, rows, :], z_ref[1, rows, :]], axis=0).astype(BF16)
        a = jnp.dot(m1_ref[...], s, preferred_element_type=F32)
        wr_ref[pl.ds(b, n1, stride=FFT_SLAB), :] = a[:n1]
        wi_ref[pl.ds(b, n1, stride=FFT_SLAB), :] = a[n1:]
        return carry

    lax.fori_loop(0, FFT_SLAB, first, 0, unroll=FFT_UNROLL)

    def slab(c, carry):
        r0 = pl.multiple_of(c * FFT_SLAB, FFT_SLAB)
        rows = pl.ds(r0, FFT_SLAB)
        ar, ai = wr_ref[rows, :], wi_ref[rows, :]
        tr, ti = _lane_replicated(twr_ref, c, ct), _lane_replicated(twi_ref, c, ct)
        p = jnp.concatenate([ar * tr - ai * ti, ar * ti + ai * tr], axis=0).astype(BF16)
        x = jnp.dot(m2_ref[...], p, preferred_element_type=F32)
        xr, xi = x[:FFT_SLAB], x[FFT_SLAB:]
        kr, ki = kr_ref[rows, :], ki_ref[rows, :]
        y = jnp.concatenate([xr * kr - xi * ki, xr * ki + xi * kr], axis=0).astype(BF16)
        bb = jnp.dot(m2i_ref[...], y, preferred_element_type=F32)
        br, bi = bb[:FFT_SLAB], bb[FFT_SLAB:]
        wr_ref[rows, :] = br * tr + bi * ti
        wi_ref[rows, :] = bi * tr - br * ti
        return carry

    lax.fori_loop(0, n1, slab, 0, unroll=FFT_UNROLL)

    def last(b, carry):
        rows = pl.ds(b, n1, stride=FFT_SLAB)
        s = jnp.concatenate([wr_ref[rows, :], wi_ref[rows, :]], axis=0).astype(BF16)
        y = jnp.dot(m1i_ref[...], s, preferred_element_type=F32)
        orow = pl.ds(b, na, stride=FFT_SLAB)
        bias = bias_ref[...]
        o_ref[0, orow, :] = y[:na] + z_ref[0, orow, :] * bias
        o_ref[1, orow, :] = y[na:] + z_ref[1, orow, :] * bias
        return carry

    lax.fori_loop(0, FFT_SLAB, last, 0, unroll=FFT_UNROLL)


def _fftconv(z, kf_re, kf_im, bias, ct=128):
    bsz, n, ch = z.shape
    n_total = 2 * n
    n1, na = n_total // FFT_SLAB, n // FFT_SLAB
    cs = _dft_consts(n_total, n)
    mats = [jnp.asarray(cs[name], BF16) for name in ("m1", "m2", "m2i", "m1i")]
    tws = [jnp.asarray(cs[name], F32) for name in ("twr", "twi")]
    full = lambda a: pl.BlockSpec(a.shape, lambda p, j: (0,) * a.ndim)
    single = pl.Buffered(1)
    zspec = pl.BlockSpec((2, n, ct), lambda p, j: (p, 0, j), pipeline_mode=single)
    kspec = pl.BlockSpec((n_total, ct), lambda p, j: (0, j), pipeline_mode=single)
    return pl.pallas_call(
        functools.partial(_fftconv_body, n1=n1, na=na),
        grid=(bsz // 2, ch // ct),
        in_specs=[zspec, kspec, kspec, pl.BlockSpec((1, ct), lambda p, j: (0, j))]
                 + [full(a) for a in mats] + [full(a) for a in tws],
        out_specs=pl.BlockSpec((2, n, ct), lambda p, j: (p, 0, j), pipeline_mode=single),
        out_shape=jax.ShapeDtypeStruct((bsz, n, ch), F32),
        scratch_shapes=[pltpu.VMEM((n_total, ct), F32), pltpu.VMEM((n_total, ct), F32)],
        compiler_params=_cparams(("parallel", "parallel")),
        name="fftconv",
    )(z, kf_re, kf_im, bias[None].astype(F32), *mats, *tws)


def _hyena_kernels(n, w1, b1, f1, w2, b2, f2, w3, width):
    t = jnp.linspace(0.0, 1.0, n, dtype=F32)[:, None]
    bands = (HY_EMB_DIM - 1) // 2
    w = 2.0 * math.pi * jnp.arange(n, dtype=F32)[:, None] / n
    f = jnp.linspace(1e-4, bands - 1, bands, dtype=F32)[None, :]
    z = jnp.concatenate([t, jnp.cos(f * w), -jnp.sin(f * w)], axis=-1)
    h = jnp.sin(f1.astype(F32) * (jnp.dot(z, w1.astype(F32), precision=HI) + b1.astype(F32)))
    h = jnp.sin(f2.astype(F32) * (jnp.dot(h, w2.astype(F32), precision=HI) + b2.astype(F32)))
    max_decay = math.log(HY_TARGET) / HY_FAST_DECAY
    min_decay = math.log(HY_TARGET) / HY_SLOW_DECAY
    deltas = jnp.abs(jnp.linspace(min_decay, max_decay, width, dtype=F32))[None, :]
    h_rev, t_rev = h[1:][::-1], t[1:][::-1]
    w3r = w3.astype(F32).reshape(w3.shape[0], HY_ORDER, 2, width)
    out = []
    for o in range(HY_ORDER):
        fwd = jnp.dot(h, w3r[:, o, 0], precision=HI) * jnp.exp(-t * deltas)
        bwd = jnp.dot(h_rev, w3r[:, o, 1], precision=HI) * jnp.exp(-t_rev * deltas)
        k = jnp.concatenate([fwd, jnp.zeros((1, width), F32), bwd], axis=0)
        out.append(k * lax.rsqrt(jnp.sum(k * k, axis=0, keepdims=True) + NORM_EPS))
    return out


def _small_conv_body(z_ref, k_ref, bias_ref, mfk_ref, mf_ref, mi_ref, o_ref):
    bsz, n, _ = z_ref.shape
    nt = 2 * n
    kf = jnp.dot(mfk_ref[...], k_ref[...], precision=HI, preferred_element_type=F32)
    kr, ki = kf[:nt], kf[nt:]
    bias = bias_ref[...]
    for p in range(bsz // 2):
        z0, z1 = z_ref[2 * p], z_ref[2 * p + 1]
        x = jnp.dot(mf_ref[...], jnp.concatenate([z0, z1], axis=0).astype(BF16), preferred_element_type=F32)
        xr, xi = x[:nt], x[nt:]
        y = jnp.concatenate([xr * kr - xi * ki, xr * ki + xi * kr], axis=0).astype(BF16)
        w = jnp.dot(mi_ref[...], y, preferred_element_type=F32)
        o_ref[2 * p] = w[:n] + z0 * bias
        o_ref[2 * p + 1] = w[n:] + z1 * bias


def _small_conv(z, k, bias, ct=256):
    bsz, n, ch = z.shape
    nt = 2 * n
    idx = np.arange(nt)
    f = np.exp(-2j * np.pi * np.outer(idx, idx) / nt)
    mfk = jnp.asarray(np.concatenate([f.real, f.imag], axis=0), F32)
    mf = jnp.asarray(_stack_complex(f[:, :n]), BF16)
    mi = jnp.asarray(_stack_complex(np.conj(f).T[:n, :] / nt), BF16)
    full = lambda a: pl.BlockSpec(a.shape, lambda j: (0,) * a.ndim)
    return pl.pallas_call(
        _small_conv_body,
        grid=(ch // ct,),
        in_specs=[pl.BlockSpec((bsz, n, ct), lambda j: (0, 0, j)), pl.BlockSpec((nt, ct), lambda j: (0, j)),
                  pl.BlockSpec((1, ct), lambda j: (0, j)), full(mfk), full(mf), full(mi)],
        out_specs=pl.BlockSpec((bsz, n, ct), lambda j: (0, 0, j)),
        out_shape=jax.ShapeDtypeStruct((bsz, n, ch), F32),
        compiler_params=_cparams(("parallel",)),
        name="small_conv",
    )(z, k, bias[None].astype(F32), mfk, mf, mi)


def _short_conv_body(pc_ref, pp_ref, pn_ref, w_ref, b_ref, v_out, x1_out, x2_out, *, width):
    i = pl.program_id(1)
    cur = pc_ref[0]
    tm = cur.shape[0]
    up = jnp.where(i == 0, 0.0, pp_ref[0, 7:8, :])
    dn = jnp.where(i == pl.num_programs(1) - 1, 0.0, pn_ref[0, 0:1, :])
    prev = jnp.concatenate([up, cur[:tm - 1]], axis=0)
    nxt = jnp.concatenate([cur[1:], dn], axis=0)
    w = w_ref[...]
    y = prev * w[0:1] + cur * w[1:2] + nxt * w[2:3] + b_ref[...]
    v_out[0] = y[:, :width]
    x1_out[0] = y[:, width:2 * width]
    x2_out[0] = y[:, 2 * width:]


def _short_conv(p, short_w, short_b, tm):
    b, n, w3 = p.shape
    width = w3 // 3
    tm = min(tm, n)
    g = tm // 8
    ng = n // 8
    ospec = pl.BlockSpec((1, tm, width), lambda bi, i: (bi, i, 0))
    oshape = jax.ShapeDtypeStruct((b, n, width), F32)
    return pl.pallas_call(
        functools.partial(_short_conv_body, width=width),
        grid=(b, n // tm),
        in_specs=[pl.BlockSpec((1, tm, w3), lambda bi, i: (bi, i, 0)),
                  pl.BlockSpec((1, 8, w3), lambda bi, i: (bi, jnp.maximum(i * g - 1, 0), 0)),
                  pl.BlockSpec((1, 8, w3), lambda bi, i: (bi, jnp.minimum((i + 1) * g, ng - 1), 0)),
                  pl.BlockSpec((HY_SHORT, w3), lambda bi, i: (0, 0)),
                  pl.BlockSpec((1, w3), lambda bi, i: (0, 0))],
        out_specs=[ospec, ospec, ospec], out_shape=[oshape, oshape, oshape],
        compiler_params=_cparams(("parallel", "parallel")),
        name="short_conv",
    )(p, p, p, short_w, short_b[None])


def _mul_body(a_ref, b_ref, o_ref):
    o_ref[...] = a_ref[...] * b_ref[...]


def _mul(a, b, tm):
    bsz, n, d = a.shape
    tm = min(tm, n)
    spec = pl.BlockSpec((1, tm, d), lambda bi, i: (bi, i, 0))
    return pl.pallas_call(
        _mul_body, grid=(bsz, n // tm), in_specs=[spec, spec], out_specs=spec,
        out_shape=jax.ShapeDtypeStruct(a.shape, a.dtype),
        compiler_params=_cparams(("parallel", "parallel")), name="gate_mul",
    )(a, b)


def _hyena_core(p, short_w, short_b, w1, b1, f1, w2, b2, f2, w3, fbias, tm):
    n = p.shape[1]
    width = p.shape[2] // 3
    v, x1, x2 = _short_conv(p, short_w, short_b, tm)
    ks = _hyena_kernels(n, w1, b1, f1, w2, b2, f2, w3, width)
    if 2 * n // FFT_SLAB >= FFT_SLAB:
        conv = lambda z, o: _fftconv(z, *_filter_spectrum(ks[o]), fbias[o])
    else:
        conv = lambda z, o: _small_conv(z, ks[o], fbias[o])
    w0 = conv(v, 0)
    w1_ = conv(_mul(x1, w0, tm), 1)
    return w1_, x2


def _sink_rows(sink, hkv, grp, tq):
    return jnp.repeat(sink.astype(F32).reshape(hkv, grp), tq, axis=1)[..., None]


def kernel(x, c, ctx, c_ctx, ada_w, ada_b, norm_mix_g, norm_ffn_g, router_w, exp_w_gate, exp_w_up, exp_w_down,
           a_w_in, a_w_out, a_q_g, a_k_g, a_sink, b_w_in, b_short_w, b_short_b, b_w1, b_b1, b_f1, b_w2, b_b2,
           b_f2, b_w3, b_bias, b_w_out, c_w_in, c_w_out, c_q_g, c_k_g, c_rpb):
    bsz, n, d = x.shape
    nctx = ctx.shape[1]
    depth = ada_w.shape[0]
    ne = router_w.shape[2]
    cap = EC_CAPACITY * n // ne
    cap_c = EC_CAPACITY * nctx // ne
    tm = 256

    pad_rows = (-(bsz + 1)) % 8
    rows = jnp.concatenate([c, c_ctx[None], jnp.zeros((pad_rows, d), F32)], axis=0)
    mod_all = _ada_mod(rows, ada_w, ada_b)

    h, hc = x, ctx
    for i in range(depth):
        last = i == depth - 1
        kind, j = i % N_MIXERS, i // N_MIXERS
        mod = mod_all[i, :bsz].reshape(bsz, 1, 6, d)
        sh1, sc1, g1, sh2, sc2, g2 = [mod[:, :, t] for t in range(6)]
        modc = jnp.broadcast_to(mod_all[i, bsz].reshape(1, 1, 6, d), (bsz, 1, 6, d))
        csh1, csc1, cg1, csh2, csc2, cg2 = [modc[:, :, t] for t in range(6)]
        yc = o_gate = yc_gate = None
        if kind == 0:
            hkv, grp = A_KV_HEADS, a_w_in.shape[2] // HEAD_DIM // A_KV_HEADS - 2
            qw, kvw = hkv * grp * HEAD_DIM, hkv * HEAD_DIM
            w_in = a_w_in[j].astype(BF16)
            w_out = a_w_out[j].astype(BF16)
            q_rot, q_pl, k_rot, v = _proj_attn(h, sh1, sc1, norm_mix_g[i], w_in, a_q_g[j], a_k_g[j],
                                               qw, kvw, True, tm)
            qc, kc, vc = _proj_attn(hc, csh1, csc1, norm_mix_g[i], w_in, a_q_g[j], a_k_g[j],
                                    qw, kvw, False, tm)
            o = _local_attn(q_rot, q_pl, k_rot, v, kc, vc, _window_bias(n),
                            _sink_rows(a_sink[j], hkv, grp, A_BLOCK), hkv, grp, A_BLOCK)
            if not last:
                yc = _ctx_attn(qc, kc, vc, _sink_rows(a_sink[j], hkv, grp, nctx), hkv, grp)
        elif kind == 1:
            w_in = b_w_in[j].astype(BF16)
            w_out = b_w_out[j].astype(BF16)
            hy = (b_short_w[j], b_short_b[j], b_w1[j], b_b1[j], b_f1[j], b_w2[j], b_b2[j], b_f2[j], b_w3[j],
                  b_bias[j])
            o, o_gate = _hyena_core(_proj_plain(h, sh1, sc1, norm_mix_g[i], w_in, tm), *hy, tm)
            if not last:
                yc, yc_gate = _hyena_core(_proj_plain(hc, csh1, csc1, norm_mix_g[i], w_in, tm), *hy, tm)
        else:
            nh = c_w_in.shape[2] // HEAD_DIM // 3
            hw = nh * HEAD_DIM
            w_in = c_w_in[j].astype(BF16)
            w_out = c_w_out[j].astype(BF16)
            q, k, v = _proj_attn(h, sh1, sc1, norm_mix_g[i], w_in, c_q_g[j], c_k_g[j], hw, hw, False, tm)
            qc, kc, vc = _proj_attn(hc, csh1, csc1, norm_mix_g[i], w_in, c_q_g[j], c_k_g[j], hw, hw, False, tm)
            o = _local_attn(q, q, k, v, kc, vc, _neighbourhood_bias(c_rpb[j], n), None, nh, 1, C_BLOCK)
            if not last:
                yc = _ctx_attn(qc, kc, vc, None, nh, 1)

        rwt = router_w[i].T
        h1, u2, aff = _outproj(o, w_out, h, g1, norm_ffn_g[i], sh2, sc2, rwt, 2 * tm, o_gate)
        il, gl = _select(aff, cap, n // SEL_BLOCK)
        rows_l = il + (jnp.arange(bsz) * n)[:, None, None]
        rows_all = rows_l.transpose(1, 0, 2).reshape(ne, bsz * cap)
        gates_all = gl.transpose(1, 0, 2).reshape(ne, bsz * cap)
        u_all = u2.reshape(bsz * n, d)
        if not last:
            hc1, uc2, affc = _outproj(yc, w_out, hc, cg1, norm_ffn_g[i], csh2, csc2, rwt, 2 * tm, yc_gate)
            icx, gcx = _select(affc, cap_c, n // SEL_BLOCK)
            rows_c = icx + (bsz * n + jnp.arange(bsz) * nctx)[:, None, None]
            rows_all = jnp.concatenate([rows_all, rows_c.transpose(1, 0, 2).reshape(ne, bsz * cap_c)], axis=1)
            gates_all = jnp.concatenate([gates_all, gcx.transpose(1, 0, 2).reshape(ne, bsz * cap_c)], axis=1)
            u_all = jnp.concatenate([u_all, uc2.reshape(bsz * nctx, d)], axis=0)
        r = rows_all.shape[1]
        xs = u_all[rows_all]
        ye = _expert_ffn(xs, gates_all[..., None], exp_w_gate, exp_w_up, exp_w_down, i, r // 4, 512)
        moe = jnp.zeros((u_all.shape[0], d), F32).at[rows_all.reshape(-1)].add(ye.reshape(-1, d))
        h = h1 + g2 * moe[:bsz * n].reshape(bsz, n, d)
        if not last:
            hc = hc1 + cg2 * moe[bsz * n:].reshape(bsz, nctx, d)
    return h
```

```python
import functools
import math

import numpy as np
import jax
import jax.numpy as jnp
from jax import lax
from jax.experimental import pallas as pl
from jax.experimental.pallas import tpu as pltpu

F32 = jnp.float32
BF16 = jnp.bfloat16
HI = lax.Precision.HIGHEST

GRID_W = 64
HEAD_DIM = 64
NORM_EPS = 1e-6
N_MIXERS = 3
A_KV_HEADS = 4
A_BLOCK = 128
ROPE_BASE = 10000.0
HY_ORDER = 2
HY_EMB_DIM = 33
HY_SHORT = 3
HY_FAST_DECAY = 0.3
HY_SLOW_DECAY = 1.5
HY_TARGET = 1e-2
NA_ROWS = 8
NA_COLS = 16
HEAD_GROUP = 4
C_BLOCK = 256
N_EXPERTS = 16
EC_CAPACITY = 2
NEG = -1e30

LANES = 128
VMEM_LIMIT = 56 * 1024 * 1024


def _cparams(sem):
    return pltpu.CompilerParams(dimension_semantics=sem, vmem_limit_bytes=VMEM_LIMIT)


def _sigmoid(x):
    return 1.0 / (1.0 + jnp.exp(-x))


def _modulate(h, g, shift, scale):
    ms = jnp.mean(h * h, axis=-1, keepdims=True)
    y = h * lax.rsqrt(ms + NORM_EPS)
    return (y * g) * (1.0 + scale) + shift


def _mod_body(s_ref, w_ref, b_ref, o_ref):
    s = s_ref[...]
    s = s * _sigmoid(s)
    o_ref[0] = jnp.dot(s, w_ref[0], precision=HI, preferred_element_type=F32) + b_ref[0]


def _ada_mod(rows, ada_w, ada_b):
    depth, d, n6 = ada_w.shape
    r = rows.shape[0]
    tn = 1536
    return pl.pallas_call(
        _mod_body,
        grid=(depth, n6 // tn),
        in_specs=[pl.BlockSpec((r, d), lambda l, j: (0, 0)),
                  pl.BlockSpec((1, d, tn), lambda l, j: (l, 0, j)),
                  pl.BlockSpec((1, 1, tn), lambda l, j: (l, 0, j))],
        out_specs=pl.BlockSpec((1, r, tn), lambda l, j: (l, 0, j)),
        out_shape=jax.ShapeDtypeStruct((depth, r, n6), F32),
        compiler_params=_cparams(("arbitrary", "arbitrary")),
        name="ada_mod",
    )(rows, ada_w, ada_b.reshape(depth, 1, n6))


def _dot_onehot(x, onehot_bf):
    x0 = x.astype(BF16)
    r1 = x - x0.astype(F32)
    x1 = r1.astype(BF16)
    x2 = (r1 - x1.astype(F32)).astype(BF16)
    return (jnp.dot(x0, onehot_bf, preferred_element_type=F32) + jnp.dot(x1, onehot_bf, preferred_element_type=F32)
            + jnp.dot(x2, onehot_bf, preferred_element_type=F32))


def _head_norm(x, gsum, gexp, gain):
    ss = _dot_onehot(x * x, gsum)
    r = lax.rsqrt(ss * (1.0 / HEAD_DIM) + NORM_EPS)
    rb = _dot_onehot(r, gexp)
    return (x * rb) * gain


def _rope(x, cos, sin_signed):
    rows, w = x.shape
    lane = lax.broadcasted_iota(jnp.int32, (rows, LANES), 1)
    first = (lane % 32) < 16
    outs = []
    for c in range(w // LANES):
        xc = x[:, c * LANES:(c + 1) * LANES]
        partner = jnp.where(first, pltpu.roll(xc, LANES - 16, 1), pltpu.roll(xc, 16, 1))
        outs.append(xc * cos + partner * sin_signed)
    return jnp.concatenate(outs, axis=1)


def _proj_attn_body(*refs, qw, kvw, rope):
    h_ref, sh_ref, sc_ref, g_ref, w_ref, qg_ref, kg_ref, gsq_ref, geq_ref, gsk_ref, gek_ref = refs[:11]
    u = _modulate(h_ref[0], g_ref[...], sh_ref[0], sc_ref[0])
    p = jnp.dot(u.astype(BF16), w_ref[...], preferred_element_type=F32)
    q = _head_norm(p[:, :qw], gsq_ref[...], geq_ref[...], qg_ref[...]) * (HEAD_DIM ** -0.5)
    k = _head_norm(p[:, qw:qw + kvw], gsk_ref[...], gek_ref[...], kg_ref[...])
    v = p[:, qw + kvw:]
    if rope:
        cos_ref, sin_ref, qr_out, qp_out, kr_out, v_out = refs[11:]
        cos, sin = cos_ref[...], sin_ref[...]
        qr_out[0] = _rope(q, cos, sin).astype(BF16)
        qp_out[0] = q.astype(BF16)
        kr_out[0] = _rope(k, cos, sin).astype(BF16)
        v_out[0] = v.astype(BF16)
    else:
        q_out, k_out, v_out = refs[11:]
        q_out[0] = q.astype(BF16)
        k_out[0] = k.astype(BF16)
        v_out[0] = v.astype(BF16)


def _group_mats(w):
    nh = w // HEAD_DIM
    gs = np.zeros((w, LANES), np.float32)
    gs[np.arange(w), np.arange(w) // HEAD_DIM] = 1.0
    return jnp.asarray(gs, BF16), jnp.asarray(gs.T.copy(), BF16)


def _rope_tables(n):
    t = jnp.arange(n)
    row = (t // GRID_W).astype(F32)
    col = (t % GRID_W).astype(F32)
    axis_dim = HEAD_DIM // 2
    inv_freq = 1.0 / (ROPE_BASE ** (jnp.arange(0, axis_dim, 2, dtype=F32) / axis_dim))
    ang_r = row[:, None] * inv_freq
    ang_c = col[:, None] * inv_freq
    cos64 = jnp.concatenate([jnp.cos(ang_r), jnp.cos(ang_r), jnp.cos(ang_c), jnp.cos(ang_c)], axis=-1)
    sin64 = jnp.concatenate([-jnp.sin(ang_r), jnp.sin(ang_r), -jnp.sin(ang_c), jnp.sin(ang_c)], axis=-1)
    return jnp.tile(cos64, (1, 2)), jnp.tile(sin64, (1, 2))


def _proj_attn(h, shift, scale, g, w_bf, q_g, k_g, qw, kvw, rope, tm):
    b, n, d = h.shape
    nout = w_bf.shape[1]
    tm = min(tm, n)
    gsq, geq = _group_mats(qw)
    gsk, gek = _group_mats(kvw)
    qg = jnp.tile(q_g, qw // HEAD_DIM)[None]
    kg = jnp.tile(k_g, kvw // HEAD_DIM)[None]
    full = lambda a: pl.BlockSpec(a.shape, lambda bi, i: (0,) * a.ndim)
    vec = pl.BlockSpec((1, 1, d), lambda bi, i: (bi, 0, 0))
    args = [h, shift, scale, g[None], w_bf, qg, kg, gsq, geq, gsk, gek]
    in_specs = [pl.BlockSpec((1, tm, d), lambda bi, i: (bi, i, 0)), vec, vec, full(args[3]), full(w_bf),
                full(qg), full(kg), full(gsq), full(geq), full(gsk), full(gek)]
    ospec = lambda w: pl.BlockSpec((1, tm, w), lambda bi, i: (bi, i, 0))
    oshape = lambda w: jax.ShapeDtypeStruct((b, n, w), BF16)
    if rope:
        cos, sin = _rope_tables(n)
        args += [cos, sin]
        in_specs += [pl.BlockSpec((tm, LANES), lambda bi, i: (i, 0))] * 2
        out_specs = [ospec(qw), ospec(qw), ospec(kvw), ospec(kvw)]
        out_shape = [oshape(qw), oshape(qw), oshape(kvw), oshape(kvw)]
    else:
        out_specs = [ospec(qw), ospec(kvw), ospec(kvw)]
        out_shape = [oshape(qw), oshape(kvw), oshape(kvw)]
    return pl.pallas_call(
        functools.partial(_proj_attn_body, qw=qw, kvw=kvw, rope=rope),
        grid=(b, n // tm), in_specs=in_specs, out_specs=out_specs, out_shape=out_shape,
        compiler_params=_cparams(("parallel", "parallel")),
        name="proj_attn_rope" if rope else "proj_attn",
    )(*args)


def _proj_plain_body(h_ref, sh_ref, sc_ref, g_ref, w_ref, o_ref):
    u = _modulate(h_ref[0], g_ref[...], sh_ref[0], sc_ref[0])
    o_ref[0] = jnp.dot(u.astype(BF16), w_ref[...], preferred_element_type=F32)


def _proj_plain(h, shift, scale, g, w_bf, tm):
    b, n, d = h.shape
    nout = w_bf.shape[1]
    tm = min(tm, n)
    vec = pl.BlockSpec((1, 1, d), lambda bi, i: (bi, 0, 0))
    return pl.pallas_call(
        _proj_plain_body,
        grid=(b, n // tm),
        in_specs=[pl.BlockSpec((1, tm, d), lambda bi, i: (bi, i, 0)), vec, vec,
                  pl.BlockSpec((1, d), lambda bi, i: (0, 0)),
                  pl.BlockSpec((d, nout), lambda bi, i: (0, 0))],
        out_specs=pl.BlockSpec((1, tm, nout), lambda bi, i: (bi, i, 0)),
        out_shape=jax.ShapeDtypeStruct((b, n, nout), F32),
        compiler_params=_cparams(("parallel", "parallel")),
        name="proj_plain",
    )(h, shift, scale, g[None], w_bf)


def _local_attn_body(*refs, hkv, grp, tq, use_sink, head_bias):
    (ql_ref, qc_ref, kp_ref, kc_ref, kn_ref, vp_ref, vc_ref, vn_ref, kx_ref, vx_ref, bias_ref) = refs[:11]
    if use_sink:
        sink_ref, o_ref = refs[11:]
    else:
        (o_ref,) = refs[11:]
    dh = HEAD_DIM
    nt = (((1,), (1,)), ((), ()))
    qcols = lambda hh: [slice((hh * grp + g) * dh, (hh * grp + g + 1) * dh) for g in range(grp)]
    for h0 in range(0, hkv, HEAD_GROUP):
        heads = range(h0, min(h0 + HEAD_GROUP, hkv))
        s, vall = [], []
        for hh in heads:
            ks = slice(hh * dh, (hh + 1) * dh)
            q_l = jnp.concatenate([ql_ref[0, :, c] for c in qcols(hh)], axis=0)
            q_c = jnp.concatenate([qc_ref[0, :, c] for c in qcols(hh)], axis=0)
            kw = jnp.concatenate([kp_ref[0, :, ks], kc_ref[0, :, ks], kn_ref[0, :, ks]], axis=0)
            s_loc = lax.dot_general(q_l, kw, nt, preferred_element_type=F32)
            s_loc = s_loc + bias_ref[0, hh if head_bias else 0]
            s_ctx = lax.dot_general(q_c, kx_ref[0, :, ks], nt, preferred_element_type=F32)
            s.append(jnp.concatenate([s_loc, s_ctx], axis=1))
            v = jnp.concatenate([vp_ref[0, :, ks], vc_ref[0, :, ks], vn_ref[0, :, ks], vx_ref[0, :, ks]], axis=0)
            vall.append(jnp.concatenate([v, jnp.ones_like(v)], axis=1))
        m = [jnp.max(x, axis=-1, keepdims=True) for x in s]
        if use_sink:
            sk = [sink_ref[hh] for hh in heads]
            m = [jnp.maximum(a, b) for a, b in zip(m, sk)]
        p = [jnp.exp(x - a).astype(BF16) for x, a in zip(s, m)]
        ox = [jnp.dot(x, v, preferred_element_type=F32) for x, v in zip(p, vall)]
        den = [x[:, dh:dh + 1] for x in ox]
        if use_sink:
            den = [d + jnp.exp(b - a) for d, a, b in zip(den, m, sk)]
        o = [x[:, :dh] * (1.0 / d) for x, d in zip(ox, den)]
        for hh, oh in zip(heads, o):
            for g, c in enumerate(qcols(hh)):
                o_ref[0, :, c] = oh[g * tq:(g + 1) * tq].astype(BF16)


def _local_attn(q_loc, q_ctx, k, v, kx, vx, bias, sink_rows, hkv, grp, tq):
    b, n, qw = q_loc.shape
    kvw = k.shape[2]
    nb = n // tq
    nctx = kx.shape[1]
    head_bias = bias.shape[1] > 1
    use_sink = sink_rows is not None
    qspec = pl.BlockSpec((1, tq, qw), lambda bi, i: (bi, i, 0))
    prev = pl.BlockSpec((1, tq, kvw), lambda bi, i: (bi, jnp.maximum(i - 1, 0), 0))
    cur = pl.BlockSpec((1, tq, kvw), lambda bi, i: (bi, i, 0))
    nxt = pl.BlockSpec((1, tq, kvw), lambda bi, i: (bi, jnp.minimum(i + 1, nb - 1), 0))
    xspec = pl.BlockSpec((1, nctx, kvw), lambda bi, i: (bi, 0, 0))
    bspec = pl.BlockSpec((1,) + bias.shape[1:],
                         lambda bi, i: (jnp.where(i == 0, 0, jnp.where(i == nb - 1, 2, 1)), 0, 0, 0))
    args = [q_loc, q_ctx, k, k, k, v, v, v, kx, vx, bias]
    in_specs = [qspec, qspec, prev, cur, nxt, prev, cur, nxt, xspec, xspec, bspec]
    if use_sink:
        args.append(sink_rows)
        in_specs.append(pl.BlockSpec(sink_rows.shape, lambda bi, i: (0, 0, 0)))
    return pl.pallas_call(
        functools.partial(_local_attn_body, hkv=hkv, grp=grp, tq=tq, use_sink=use_sink, head_bias=head_bias),
        grid=(b, nb), in_specs=in_specs,
        out_specs=pl.BlockSpec((1, tq, qw), lambda bi, i: (bi, i, 0)),
        out_shape=jax.ShapeDtypeStruct((b, n, qw), BF16),
        compiler_params=_cparams(("parallel", "parallel")),
        name="local_attn_sink" if use_sink else "local_attn",
    )(*args)


def _window_bias(n):
    tq = A_BLOCK
    grp = 4
    qi = np.arange(tq)[:, None]
    kj = np.arange(3 * tq)[None, :] - tq
    band = np.abs(qi - kj) <= A_BLOCK
    nb = n // tq
    out = []
    for which in range(3):
        blk = {0: 0, 1: min(1, nb - 1), 2: nb - 1}[which]
        pos = blk * tq + kj
        ok = band & (pos >= 0) & (pos < n)
        out.append(np.tile(np.where(ok, 0.0, NEG).astype(np.float32), (grp, 1)))
    return jnp.asarray(np.stack(out)[:, None])


def _nbr_bias_body(rpb_ref, colok_ref, o_ref, *, row_ok):
    gw = GRID_W
    col_ok = colok_ref[...] > 0.5
    neg = jnp.full((gw, gw), NEG, F32)
    toep = []
    for ro in range(2 * NA_ROWS - 1):
        x = jnp.broadcast_to(rpb_ref[0, ro:ro + 1, :], (gw, LANES))
        t = pltpu.roll(x, LANES - (NA_COLS - 1), 1, stride=1, stride_axis=0)[:, :gw]
        toep.append(jnp.where(col_ok, t, NEG))
    rq = C_BLOCK // gw
    for which in range(3):
        rows_out = []
        for qa in range(rq):
            blocks = [toep[(ka - rq) - qa + (NA_ROWS - 1)] if row_ok[which][qa][ka] else neg
                      for ka in range(3 * rq)]
            rows_out.append(jnp.concatenate(blocks, axis=1))
        o_ref[which, 0] = jnp.concatenate(rows_out, axis=0)


def _neighbourhood_bias(rpb, n):
    nh = rpb.shape[0]
    rows = n // GRID_W
    kr = min(NA_ROWS, rows)
    rq = C_BLOCK // GRID_W
    nb = n // C_BLOCK
    assert nb >= 3 and rpb.shape[1:] == (2 * NA_ROWS - 1, 2 * NA_COLS - 1)
    qc = np.arange(GRID_W)
    c0 = np.clip(qc - NA_COLS // 2, 0, GRID_W - NA_COLS)
    col_ok = ((qc[None, :] >= c0[:, None]) & (qc[None, :] < c0[:, None] + NA_COLS)).astype(np.float32)
    row_ok = []
    for which in range(3):
        blk = {0: 0, 1: 1, 2: nb - 1}[which]
        qr = blk * rq + np.arange(rq)
        kr_abs = blk * rq + np.arange(3 * rq) - rq
        r0 = np.clip(qr - kr // 2, 0, rows - kr)
        ok = (kr_abs[None, :] >= r0[:, None]) & (kr_abs[None, :] < r0[:, None] + kr)
        row_ok.append(tuple(tuple(bool(v) for v in row) for row in ok))
    rpb_pad = jnp.pad(rpb.astype(F32), ((0, 0), (0, 1), (0, LANES - rpb.shape[2])))
    return pl.pallas_call(
        functools.partial(_nbr_bias_body, row_ok=tuple(row_ok)),
        grid=(nh,),
        in_specs=[pl.BlockSpec((1, 2 * NA_ROWS, LANES), lambda hh: (hh, 0, 0)),
                  pl.BlockSpec((GRID_W, GRID_W), lambda hh: (0, 0))],
        out_specs=pl.BlockSpec((3, 1, C_BLOCK, 3 * C_BLOCK), lambda hh: (0, hh, 0, 0)),
        out_shape=jax.ShapeDtypeStruct((3, nh, C_BLOCK, 3 * C_BLOCK), F32),
        compiler_params=_cparams(("parallel",)),
        name="nbr_bias",
    )(rpb_pad, jnp.asarray(col_ok))


def _ctx_attn_body(*refs, hkv, grp, use_sink):
    q_ref, k_ref, v_ref = refs[:3]
    if use_sink:
        sink_ref, o_ref = refs[3:]
    else:
        (o_ref,) = refs[3:]
    dh = HEAD_DIM
    nq = q_ref.shape[1]
    nt = (((1,), (1,)), ((), ()))
    for hh in range(hkv):
        ks = slice(hh * dh, (hh + 1) * dh)
        qcols = [slice((hh * grp + g) * dh, (hh * grp + g + 1) * dh) for g in range(grp)]
        q = jnp.concatenate([q_ref[0, :, c] for c in qcols], axis=0)
        s = lax.dot_general(q, k_ref[0, :, ks], nt, preferred_element_type=F32)
        m = jnp.max(s, axis=-1, keepdims=True)
        if use_sink:
            sk = sink_ref[hh]
            m = jnp.maximum(m, sk)
        e = jnp.exp(s - m)
        den = jnp.sum(e, axis=-1, keepdims=True)
        if use_sink:
            den = den + jnp.exp(sk - m)
        o = jnp.dot((e * (1.0 / den)).astype(BF16), v_ref[0, :, ks], preferred_element_type=F32)
        for g in range(grp):
            o_ref[0, :, qcols[g]] = o[g * nq:(g + 1) * nq].astype(BF16)


def _ctx_attn(q, k, v, sink_rows, hkv, grp):
    b, nq, qw = q.shape
    kvw = k.shape[2]
    use_sink = sink_rows is not None
    args = [q, k, v]
    in_specs = [pl.BlockSpec((1, nq, qw), lambda bi: (bi, 0, 0)),
                pl.BlockSpec((1, nq, kvw), lambda bi: (bi, 0, 0)),
                pl.BlockSpec((1, nq, kvw), lambda bi: (bi, 0, 0))]
    if use_sink:
        args.append(sink_rows)
        in_specs.append(pl.BlockSpec(sink_rows.shape, lambda bi: (0, 0, 0)))
    return pl.pallas_call(
        functools.partial(_ctx_attn_body, hkv=hkv, grp=grp, use_sink=use_sink),
        grid=(b,), in_specs=in_specs,
        out_specs=pl.BlockSpec((1, nq, qw), lambda bi: (bi, 0, 0)),
        out_shape=jax.ShapeDtypeStruct((b, nq, qw), BF16),
        compiler_params=_cparams(("parallel",)),
        name="ctx_attn_sink" if use_sink else "ctx_attn",
    )(*args)


def _outproj_body(*refs, gated):
    if gated:
        o_ref, og_ref, w_ref, h_ref, g1_ref, gn_ref, sh2_ref, sc2_ref, rwt_ref, h1_out, u2_out, aff_out = refs
        o = o_ref[0] * og_ref[0]
    else:
        o_ref, w_ref, h_ref, g1_ref, gn_ref, sh2_ref, sc2_ref, rwt_ref, h1_out, u2_out, aff_out = refs
        o = o_ref[0]
    y = jnp.dot(o.astype(BF16), w_ref[...], preferred_element_type=F32)
    h1 = h_ref[0] + g1_ref[0] * y
    h1_out[0] = h1
    u2 = _modulate(h1, gn_ref[...], sh2_ref[0], sc2_ref[0])
    u2_out[0] = u2.astype(BF16)
    lt = lax.dot_general(rwt_ref[...], u2, (((1,), (1,)), ((), ())), precision=HI, preferred_element_type=F32)
    e = jnp.exp(lt - jnp.max(lt, axis=0, keepdims=True))
    aff_out[0] = e / jnp.sum(e, axis=0, keepdims=True)


def _outproj(o, w_bf, h, g1, gn, sh2, sc2, rwt, tm, o_gate=None):
    b, n, d = h.shape
    kin = o.shape[2]
    ne = rwt.shape[0]
    tm = min(tm, n)
    vec = pl.BlockSpec((1, 1, d), lambda bi, i: (bi, 0, 0))
    ospec = pl.BlockSpec((1, tm, kin), lambda bi, i: (bi, i, 0))
    gated = o_gate is not None
    return pl.pallas_call(
        functools.partial(_outproj_body, gated=gated),
        grid=(b, n // tm),
        in_specs=([ospec, ospec] if gated else [ospec]) + [
                  pl.BlockSpec((kin, d), lambda bi, i: (0, 0)),
                  pl.BlockSpec((1, tm, d), lambda bi, i: (bi, i, 0)),
                  vec, pl.BlockSpec((1, d), lambda bi, i: (0, 0)), vec, vec,
                  pl.BlockSpec((ne, d), lambda bi, i: (0, 0))],
        out_specs=[pl.BlockSpec((1, tm, d), lambda bi, i: (bi, i, 0)),
                   pl.BlockSpec((1, tm, d), lambda bi, i: (bi, i, 0)),
                   pl.BlockSpec((1, ne, tm), lambda bi, i: (bi, 0, i))],
        out_shape=[jax.ShapeDtypeStruct((b, n, d), F32),
                   jax.ShapeDtypeStruct((b, n, d), BF16),
                   jax.ShapeDtypeStruct((b, ne, n), F32)],
        compiler_params=_cparams(("parallel", "parallel")),
        name="outproj_router",
    )(*((o, o_gate) if gated else (o,)), w_bf, h, g1, gn[None], sh2, sc2, rwt)


SEL_BLOCK = 128


def _bcast_lane_sum(x):
    return jnp.broadcast_to(jnp.sum(x, axis=-1, keepdims=True), x.shape)


def _block_prefix(tot):
    run = jnp.zeros(tot.shape[1:], F32)
    out = []
    for j in range(tot.shape[0]):
        out.append(run)
        run = run + tot[j]
    return jnp.stack(out)


def _select_body(aff_ref, tri_ref, trib_ref, idx_ref, gate_ref, pos_ref, inc_ref, off_ref, mk_ref, *, cap, ne):
    nb = aff_ref.shape[1] // ne
    x = aff_ref[0].reshape(nb, ne, SEL_BLOCK)
    bits = pltpu.bitcast(x, jnp.int32)
    count = lambda m: _bcast_lane_sum(jnp.sum(m.astype(F32), axis=0))

    def step(i, prefix):
        cand = prefix | jnp.left_shift(jnp.int32(1), 30 - i)
        return jnp.where(count(bits >= cand[None]) >= cap, cand, prefix)

    thr = lax.fori_loop(0, 31, step, jnp.zeros((ne, SEL_BLOCK), jnp.int32))
    gt = bits > thr[None]
    eq = bits == thr[None]
    need = cap - count(gt)

    def prefix_counts(m):
        mf = m.astype(BF16).reshape(nb * ne, SEL_BLOCK)
        inc = jnp.dot(mf, tri_ref[...], preferred_element_type=F32).reshape(nb, ne, SEL_BLOCK)
        tot = jnp.broadcast_to(inc[:, :, SEL_BLOCK - 1:], inc.shape)
        return inc, tot, _block_prefix(tot)

    e_inc, _, e_off = prefix_counts(eq)
    rank_eq = e_inc + e_off - eq.astype(F32)
    mask = gt | (eq & (rank_eq < need[None]))
    inc, tot, off = prefix_counts(mask)
    inc_ref[...] = inc.reshape(nb * ne, SEL_BLOCK)
    off_ref[...] = off.reshape(nb * ne, SEL_BLOCK)
    mk_ref[...] = mask.astype(F32).reshape(nb * ne, SEL_BLOCK)
    pos_ref[0] = jnp.where(mask, inc + off - 1.0, -1.0).astype(jnp.int32).reshape(nb * ne, SEL_BLOCK)

    slot = lax.broadcasted_iota(jnp.int32, (cap, 1), 0).astype(F32)
    lane_nb = lax.broadcasted_iota(jnp.int32, (cap, nb), 1).astype(F32)
    lane = lax.broadcasted_iota(jnp.int32, (cap, SEL_BLOCK), 1).astype(F32)
    ones = jnp.ones((8, SEL_BLOCK), BF16)
    for e in range(ne):
        rows = pl.ds(e, nb, stride=ne)
        mk = mk_ref[rows, :].astype(BF16)
        cnt_row = lax.dot_general(ones, mk, (((1,), (1,)), ((), ())), preferred_element_type=F32)
        pinc_row = jnp.dot(cnt_row.astype(BF16), trib_ref[...], preferred_element_type=F32)[0:1]
        blk = jnp.sum((pinc_row <= slot).astype(F32), axis=-1, keepdims=True)
        onehot = (lane_nb == blk).astype(BF16)
        rhs = jnp.concatenate([inc_ref[rows, :], off_ref[rows, :], aff_ref[0, rows, :]], axis=1)
        r0 = rhs.astype(BF16)
        d1 = rhs - r0.astype(F32)
        r1 = d1.astype(BF16)
        r2 = (d1 - r1.astype(F32)).astype(BF16)
        got = (jnp.dot(onehot, r0, preferred_element_type=F32) + jnp.dot(onehot, r1, preferred_element_type=F32)
               + jnp.dot(onehot, r2, preferred_element_type=F32))
        rank = slot - got[:, SEL_BLOCK:SEL_BLOCK + 1]
        local = jnp.sum((got[:, :SEL_BLOCK] <= rank).astype(F32), axis=-1, keepdims=True)
        gate = jnp.sum(jnp.where(lane == local, got[:, 2 * SEL_BLOCK:], 0.0), axis=-1, keepdims=True)
        idx_ref[0, :, e:e + 1] = (blk * SEL_BLOCK + local).astype(jnp.int32)
        gate_ref[0, :, e:e + 1] = gate


def _select(aff, cap, nb_pad):
    bsz, ne, n = aff.shape
    pad = nb_pad * SEL_BLOCK - n
    a = jnp.pad(aff, ((0, 0), (0, 0), (0, pad))).reshape(bsz, ne, nb_pad, SEL_BLOCK).transpose(0, 2, 1, 3)
    a = a.reshape(bsz, nb_pad * ne, SEL_BLOCK)
    tri = jnp.asarray(np.triu(np.ones((SEL_BLOCK, SEL_BLOCK), np.float32)), BF16)
    trib = jnp.asarray(np.triu(np.ones((nb_pad, nb_pad), np.float32)), BF16)
    idx, gates, pos = pl.pallas_call(
        functools.partial(_select_body, cap=cap, ne=ne),
        grid=(bsz,),
        in_specs=[pl.BlockSpec((1, nb_pad * ne, SEL_BLOCK), lambda b: (b, 0, 0)),
                  pl.BlockSpec(tri.shape, lambda b: (0, 0)), pl.BlockSpec(trib.shape, lambda b: (0, 0))],
        out_specs=[pl.BlockSpec((1, cap, ne), lambda b: (b, 0, 0))] * 2
                  + [pl.BlockSpec((1, nb_pad * ne, SEL_BLOCK), lambda b: (b, 0, 0))],
        out_shape=[jax.ShapeDtypeStruct((bsz, cap, ne), jnp.int32), jax.ShapeDtypeStruct((bsz, cap, ne), F32),
                   jax.ShapeDtypeStruct((bsz, nb_pad * ne, SEL_BLOCK), jnp.int32)],
        scratch_shapes=[pltpu.VMEM((nb_pad * ne, SEL_BLOCK), F32)] * 3,
        compiler_params=_cparams(("parallel",)),
        name="expert_select",
    )(a, tri, trib)
    pos = pos.reshape(bsz, nb_pad, ne, SEL_BLOCK).transpose(0, 1, 3, 2).reshape(bsz, nb_pad * SEL_BLOCK, ne)
    return idx.transpose(0, 2, 1), gates.transpose(0, 2, 1), pos[:, :n]


def _expert_body(xs_ref, wg_ref, wu_ref, wd_ref, gate_ref, o_ref, acc_ref):
    f = pl.program_id(2)
    xs = xs_ref[0]
    a = jnp.dot(xs, wg_ref[...].astype(BF16), preferred_element_type=F32)
    bb = jnp.dot(xs, wu_ref[...].astype(BF16), preferred_element_type=F32)
    hid = (a * _sigmoid(a)) * bb
    y = jnp.dot(hid.astype(BF16), wd_ref[...].astype(BF16), preferred_element_type=F32)

    @pl.when(f == 0)
    def _():
        acc_ref[...] = y

    @pl.when(f > 0)
    def _():
        acc_ref[...] += y

    @pl.when(f == pl.num_programs(2) - 1)
    def _():
        o_ref[0] = (acc_ref[...] * gate_ref[0]).astype(o_ref.dtype)


def _expert_ffn(xs, gates, w_gate, w_up, w_down, layer, tr, tf):
    ne, r, d = xs.shape
    ff = w_gate.shape[3]
    return pl.pallas_call(
        _expert_body,
        grid=(ne, r // tr, ff // tf),
        in_specs=[pl.BlockSpec((1, tr, d), lambda e, c, f: (e, c, 0)),
                  pl.BlockSpec((None, None, d, tf), lambda e, c, f: (layer, e, 0, f)),
                  pl.BlockSpec((None, None, d, tf), lambda e, c, f: (layer, e, 0, f)),
                  pl.BlockSpec((None, None, tf, d), lambda e, c, f: (layer, e, f, 0)),
                  pl.BlockSpec((1, tr, 1), lambda e, c, f: (e, c, 0))],
        out_specs=pl.BlockSpec((1, tr, d), lambda e, c, f: (e, c, 0)),
        out_shape=jax.ShapeDtypeStruct((ne, r, d), BF16),
        scratch_shapes=[pltpu.VMEM((tr, d), F32)],
        compiler_params=_cparams(("parallel", "parallel", "arbitrary")),
        name="expert_ffn",
    )(xs, w_gate, w_up, w_down, gates)


COMBINE_TILE = 256
COMBINE_WIN = 128
BF16_ROWS = 16


def _combine_body(ws_ref, cnt_ref, ye_hbm, pos_ref, h_ref, g_ref, o_ref, buf, xbuf, acc_ref, sem, xsem,
                  *, ne, nt, rows_total):
    win = COMBINE_WIN
    step = pl.program_id(0) * nt + pl.program_id(1)
    nsteps = pl.num_programs(0) * nt
    slot = step % 2

    def clamp(st):
        return pl.multiple_of(jnp.minimum(st, rows_total - win), BF16_ROWS)

    def first_start(s_idx, e):
        return clamp((ws_ref[s_idx * ne + e] // BF16_ROWS) * BF16_ROWS)

    def window_copy(s_idx, sl, e):
        return pltpu.make_async_copy(ye_hbm.at[e, pl.ds(first_start(s_idx, e), win), :], buf.at[sl, e],
                                     sem.at[sl, e])

    @pl.when(step == 0)
    def _():
        for e in range(ne):
            window_copy(0, 0, e).start()

    @pl.when(step + 1 < nsteps)
    def _():
        for e in range(ne):
            window_copy(step + 1, 1 - slot, e).start()

    tm = pos_ref.shape[1]
    d = h_ref.shape[2]
    lane = lax.broadcasted_iota(jnp.int32, (tm, win), 1)
    pos = [pos_ref[0, :, e:e + 1] for e in range(ne)]
    st0 = [first_start(step, e) for e in range(ne)]
    onehot = jnp.concatenate([((pos[e] - st0[e]) == lane).astype(BF16) for e in range(ne)], axis=1)
    for e in range(ne):
        window_copy(step, slot, e).wait()
    acc_ref[...] = jnp.dot(onehot, buf[slot].reshape(ne * win, d), preferred_element_type=F32)
    nwin = [(ws_ref[step * ne + e] + cnt_ref[step * ne + e] - st0[e] + win - 1) // win for e in range(ne)]
    extra = nwin[0]
    for e in range(1, ne):
        extra = jnp.maximum(extra, nwin[e])

    @pl.when(extra > 1)
    def _():
        for e in range(ne):
            def more(k, carry, e=e):
                lo = st0[e] + k * win
                st = clamp(lo)
                cp = pltpu.make_async_copy(ye_hbm.at[e, pl.ds(st, win), :], xbuf, xsem)
                cp.start()
                cp.wait()
                oh = (((pos[e] - st) == lane) & (pos[e] >= lo)).astype(BF16)
                acc_ref[...] += jnp.dot(oh, xbuf[...], preferred_element_type=F32)
                return carry

            lax.fori_loop(1, nwin[e], more, 0)

    o_ref[0] = h_ref[0] + g_ref[0] * acc_ref[...]


def _combine(ye, pos_global, h, g, ne):
    bsz, n, d = h.shape
    tm = min(COMBINE_TILE, n)
    nt = n // tm
    rows_total = ye.shape[1]
    taken = (pos_global >= 0).reshape(bsz, nt, tm, ne)
    cnt = jnp.sum(taken, axis=2).astype(jnp.int32)
    first = jnp.min(jnp.where(taken, pos_global.reshape(bsz, nt, tm, ne), rows_total), axis=2)
    ws = jnp.where(cnt > 0, first, 0).astype(jnp.int32)
    grid_spec = pltpu.PrefetchScalarGridSpec(
        num_scalar_prefetch=2,
        grid=(bsz, nt),
        in_specs=[pl.BlockSpec(memory_space=pl.ANY),
                  pl.BlockSpec((1, tm, ne), lambda b, t, ws, cnt: (b, t, 0)),
                  pl.BlockSpec((1, tm, d), lambda b, t, ws, cnt: (b, t, 0)),
                  pl.BlockSpec((1, 1, d), lambda b, t, ws, cnt: (b, 0, 0))],
        out_specs=pl.BlockSpec((1, tm, d), lambda b, t, ws, cnt: (b, t, 0)),
        scratch_shapes=[pltpu.VMEM((2, ne, COMBINE_WIN, d), BF16), pltpu.VMEM((COMBINE_WIN, d), BF16),
                        pltpu.VMEM((tm, d), F32), pltpu.SemaphoreType.DMA((2, ne)), pltpu.SemaphoreType.DMA(())])
    return pl.pallas_call(
        functools.partial(_combine_body, ne=ne, nt=nt, rows_total=rows_total),
        grid_spec=grid_spec,
        out_shape=jax.ShapeDtypeStruct((bsz, n, d), F32),
        compiler_params=_cparams(("arbitrary", "arbitrary")),
        name="moe_combine",
    )(ws.reshape(-1), cnt.reshape(-1), ye, pos_global, h, g)


FFT_SLAB = 128
FFT_UNROLL = 8


def _stack_complex(m):
    return np.block([[m.real, -m.imag], [m.imag, m.real]])


def _dft_consts(n_total, n_nonzero):
    n1 = n_total // FFT_SLAB
    na = n_nonzero // FFT_SLAB
    idx1 = np.arange(n1)
    f1 = np.exp(-2j * np.pi * np.outer(idx1, idx1) / n1)
    idx2 = np.arange(FFT_SLAB)
    f2 = np.exp(-2j * np.pi * np.outer(idx2, idx2) / FFT_SLAB)
    tw = np.exp(-2j * np.pi * np.outer(idx1, idx2) / n_total)
    return dict(
        m1=_stack_complex(f1[:, :na]),
        m1_real=np.concatenate([f1.real, f1.imag], axis=0),
        m2=_stack_complex(f2),
        m2i=_stack_complex(np.conj(f2).T / n_total),
        m1i=_stack_complex(np.conj(f1).T[:na, :]),
        twr=tw.real, twi=tw.imag)


def _lane_replicated(row_ref, c, ct):
    t = jnp.broadcast_to(row_ref[pl.ds(c, 1), :], (FFT_SLAB, FFT_SLAB)).T
    return t if ct == FFT_SLAB else jnp.tile(t, (1, ct // FFT_SLAB))


def _dot_split(mh_ref, ml_ref, x):
    xh = x.astype(BF16)
    xl = (x - xh.astype(F32)).astype(BF16)
    mh = mh_ref[...]
    return (jnp.dot(mh, xh, preferred_element_type=F32) + jnp.dot(mh, xl, preferred_element_type=F32)
            + jnp.dot(ml_ref[...], xh, preferred_element_type=F32))


def _spectrum_body(k_ref, m1h_ref, m1l_ref, m2h_ref, m2l_ref, twr_ref, twi_ref, or_ref, oi_ref, *, n1):
    ct = k_ref.shape[1]

    def first(b, carry):
        a = _dot_split(m1h_ref, m1l_ref, k_ref[pl.ds(b, n1, stride=FFT_SLAB), :])
        or_ref[pl.ds(b, n1, stride=FFT_SLAB), :] = a[:n1]
        oi_ref[pl.ds(b, n1, stride=FFT_SLAB), :] = a[n1:]
        return carry

    lax.fori_loop(0, FFT_SLAB, first, 0, unroll=FFT_UNROLL)

    def slab(c, carry):
        r0 = pl.multiple_of(c * FFT_SLAB, FFT_SLAB)
        ar, ai = or_ref[pl.ds(r0, FFT_SLAB), :], oi_ref[pl.ds(r0, FFT_SLAB), :]
        tr, ti = _lane_replicated(twr_ref, c, ct), _lane_replicated(twi_ref, c, ct)
        p = jnp.concatenate([ar * tr - ai * ti, ar * ti + ai * tr], axis=0)
        x = _dot_split(m2h_ref, m2l_ref, p)
        or_ref[pl.ds(r0, FFT_SLAB), :] = x[:FFT_SLAB]
        oi_ref[pl.ds(r0, FFT_SLAB), :] = x[FFT_SLAB:]
        return carry

    lax.fori_loop(0, n1, slab, 0, unroll=FFT_UNROLL)


def _split_bf16(m):
    hi = jnp.asarray(m, F32).astype(BF16)
    lo = (jnp.asarray(m, F32) - hi.astype(F32)).astype(BF16)
    return hi, lo


def _filter_spectrum(k, ct=128):
    n_total, ch = k.shape
    n1 = n_total // FFT_SLAB
    cs = _dft_consts(n_total, n_total)
    consts = [*_split_bf16(cs["m1_real"]), *_split_bf16(cs["m2"]),
              jnp.asarray(cs["twr"], F32), jnp.asarray(cs["twi"], F32)]
    full = lambda a: pl.BlockSpec(a.shape, lambda j: (0,) * a.ndim)
    col = pl.BlockSpec((n_total, ct), lambda j: (0, j))
    return pl.pallas_call(
        functools.partial(_spectrum_body, n1=n1),
        grid=(ch // ct,),
        in_specs=[col] + [full(a) for a in consts],
        out_specs=[col, col],
        out_shape=[jax.ShapeDtypeStruct((n_total, ch), F32)] * 2,
        compiler_params=_cparams(("parallel",)),
        name="filter_spectrum",
    )(k, *consts)


def _fftconv_body(z_ref, kr_ref, ki_ref, bias_ref, m1_ref, m2_ref, m2i_ref, m1i_ref, twr_ref, twi_ref,
                  o_ref, wr_ref, wi_ref, *, n1, na):
    ct = z_ref.shape[2]

    def first(b, carry):
        rows = pl.ds(b, na, stride=FFT_SLAB)
        s = jnp.concatenate([z_ref[0, rows, :], z_ref[1, rows, :]], axis=0).astype(BF16)
        a = jnp.dot(m1_ref[...], s, preferred_element_type=F32)
        wr_ref[pl.ds(b, n1, stride=FFT_SLAB), :] = a[:n1]
        wi_ref[pl.ds(b, n1, stride=FFT_SLAB), :] = a[n1:]
        return carry

    lax.fori_loop(0, FFT_SLAB, first, 0, unroll=FFT_UNROLL)

    def slab(c, carry):
        r0 = pl.multiple_of(c * FFT_SLAB, FFT_SLAB)
        rows = pl.ds(r0, FFT_SLAB)
        ar, ai = wr_ref[rows, :], wi_ref[rows, :]
        tr, ti = _lane_replicated(twr_ref, c, ct), _lane_replicated(twi_ref, c, ct)
        p = jnp.concatenate([ar * tr - ai * ti, ar * ti + ai * tr], axis=0).astype(BF16)
        x = jnp.dot(m2_ref[...], p, preferred_element_type=F32)
        xr, xi = x[:FFT_SLAB], x[FFT_SLAB:]
        kr, ki = kr_ref[rows, :], ki_ref[rows, :]
        y = jnp.concatenate([xr * kr - xi * ki, xr * ki + xi * kr], axis=0).astype(BF16)
        bb = jnp.dot(m2i_ref[...], y, preferred_element_type=F32)
        br, bi = bb[:FFT_SLAB], bb[FFT_SLAB:]
        wr_ref[rows, :] = br * tr + bi * ti
        wi_ref[rows, :] = bi * tr - br * ti
        return carry

    lax.fori_loop(0, n1, slab, 0, unroll=FFT_UNROLL)

    def last(b, carry):
        rows = pl.ds(b, n1, stride=FFT_SLAB)
        s = jnp.concatenate([wr_ref[rows, :], wi_ref[rows, :]], axis=0).astype(BF16)
        y = jnp.dot(m1i_ref[...], s, preferred_element_type=F32)
        orow = pl.ds(b, na, stride=FFT_SLAB)
        bias = bias_ref[...]
        o_ref[0, orow, :] = y[:na] + z_ref[0, orow, :] * bias
        o_ref[1, orow, :] = y[na:] + z_ref[1, orow, :] * bias
        return carry

    lax.fori_loop(0, FFT_SLAB, last, 0, unroll=FFT_UNROLL)


def _fftconv(z, kf_re, kf_im, bias, ct=128):
    bsz, n, ch = z.shape
    n_total = 2 * n
    n1, na = n_total // FFT_SLAB, n // FFT_SLAB
    cs = _dft_consts(n_total, n)
    mats = [jnp.asarray(cs[name], BF16) for name in ("m1", "m2", "m2i", "m1i")]
    tws = [jnp.asarray(cs[name], F32) for name in ("twr", "twi")]
    full = lambda a: pl.BlockSpec(a.shape, lambda p, j: (0,) * a.ndim)
    single = pl.Buffered(1)
    zspec = pl.BlockSpec((2, n, ct), lambda p, j: (p, 0, j), pipeline_mode=single)
    kspec = pl.BlockSpec((n_total, ct), lambda p, j: (0, j), pipeline_mode=single)
    return pl.pallas_call(
        functools.partial(_fftconv_body, n1=n1, na=na),
        grid=(bsz // 2, ch // ct),
        in_specs=[zspec, kspec, kspec, pl.BlockSpec((1, ct), lambda p, j: (0, j))]
                 + [full(a) for a in mats] + [full(a) for a in tws],
        out_specs=pl.BlockSpec((2, n, ct), lambda p, j: (p, 0, j), pipeline_mode=single),
        out_shape=jax.ShapeDtypeStruct((bsz, n, ch), F32),
        scratch_shapes=[pltpu.VMEM((n_total, ct), F32), pltpu.VMEM((n_total, ct), F32)],
        compiler_params=_cparams(("parallel", "parallel")),
        name="fftconv",
    )(z, kf_re, kf_im, bias[None].astype(F32), *mats, *tws)


def _hyena_kernels(n, w1, b1, f1, w2, b2, f2, w3, width):
    t = jnp.linspace(0.0, 1.0, n, dtype=F32)[:, None]
    bands = (HY_EMB_DIM - 1) // 2
    w = 2.0 * math.pi * jnp.arange(n, dtype=F32)[:, None] / n
    f = jnp.linspace(1e-4, bands - 1, bands, dtype=F32)[None, :]
    z = jnp.concatenate([t, jnp.cos(f * w), -jnp.sin(f * w)], axis=-1)
    h = jnp.sin(f1.astype(F32) * (jnp.dot(z, w1.astype(F32), precision=HI) + b1.astype(F32)))
    h = jnp.sin(f2.astype(F32) * (jnp.dot(h, w2.astype(F32), precision=HI) + b2.astype(F32)))
    max_decay = math.log(HY_TARGET) / HY_FAST_DECAY
    min_decay = math.log(HY_TARGET) / HY_SLOW_DECAY
    deltas = jnp.abs(jnp.linspace(min_decay, max_decay, width, dtype=F32))[None, :]
    h_rev, t_rev = h[1:][::-1], t[1:][::-1]
    w3r = w3.astype(F32).reshape(w3.shape[0], HY_ORDER, 2, width)
    out = []
    for o in range(HY_ORDER):
        fwd = jnp.dot(h, w3r[:, o, 0], precision=HI) * jnp.exp(-t * deltas)
        bwd = jnp.dot(h_rev, w3r[:, o, 1], precision=HI) * jnp.exp(-t_rev * deltas)
        k = jnp.concatenate([fwd, jnp.zeros((1, width), F32), bwd], axis=0)
        out.append(k * lax.rsqrt(jnp.sum(k * k, axis=0, keepdims=True) + NORM_EPS))
    return out


def _small_conv_body(z_ref, k_ref, bias_ref, mfk_ref, mf_ref, mi_ref, o_ref):
    bsz, n, _ = z_ref.shape
    nt = 2 * n
    kf = jnp.dot(mfk_ref[...], k_ref[...], precision=HI, preferred_element_type=F32)
    kr, ki = kf[:nt], kf[nt:]
    bias = bias_ref[...]
    for p in range(bsz // 2):
        z0, z1 = z_ref[2 * p], z_ref[2 * p + 1]
        x = jnp.dot(mf_ref[...], jnp.concatenate([z0, z1], axis=0).astype(BF16), preferred_element_type=F32)
        xr, xi = x[:nt], x[nt:]
        y = jnp.concatenate([xr * kr - xi * ki, xr * ki + xi * kr], axis=0).astype(BF16)
        w = jnp.dot(mi_ref[...], y, preferred_element_type=F32)
        o_ref[2 * p] = w[:n] + z0 * bias
        o_ref[2 * p + 1] = w[n:] + z1 * bias


def _small_conv(z, k, bias, ct=256):
    bsz, n, ch = z.shape
    nt = 2 * n
    idx = np.arange(nt)
    f = np.exp(-2j * np.pi * np.outer(idx, idx) / nt)
    mfk = jnp.asarray(np.concatenate([f.real, f.imag], axis=0), F32)
    mf = jnp.asarray(_stack_complex(f[:, :n]), BF16)
    mi = jnp.asarray(_stack_complex(np.conj(f).T[:n, :] / nt), BF16)
    full = lambda a: pl.BlockSpec(a.shape, lambda j: (0,) * a.ndim)
    return pl.pallas_call(
        _small_conv_body,
        grid=(ch // ct,),
        in_specs=[pl.BlockSpec((bsz, n, ct), lambda j: (0, 0, j)), pl.BlockSpec((nt, ct), lambda j: (0, j)),
                  pl.BlockSpec((1, ct), lambda j: (0, j)), full(mfk), full(mf), full(mi)],
        out_specs=pl.BlockSpec((bsz, n, ct), lambda j: (0, 0, j)),
        out_shape=jax.ShapeDtypeStruct((bsz, n, ch), F32),
        compiler_params=_cparams(("parallel",)),
        name="small_conv",
    )(z, k, bias[None].astype(F32), mfk, mf, mi)


def _short_conv_body(pc_ref, pp_ref, pn_ref, w_ref, b_ref, v_out, x1_out, x2_out, *, width):
    i = pl.program_id(1)
    cur = pc_ref[0]
    tm = cur.shape[0]
    up = jnp.where(i == 0, 0.0, pp_ref[0, 7:8, :])
    dn = jnp.where(i == pl.num_programs(1) - 1, 0.0, pn_ref[0, 0:1, :])
    prev = jnp.concatenate([up, cur[:tm - 1]], axis=0)
    nxt = jnp.concatenate([cur[1:], dn], axis=0)
    w = w_ref[...]
    y = prev * w[0:1] + cur * w[1:2] + nxt * w[2:3] + b_ref[...]
    v_out[0] = y[:, :width]
    x1_out[0] = y[:, width:2 * width]
    x2_out[0] = y[:, 2 * width:]


def _short_conv(p, short_w, short_b, tm):
    b, n, w3 = p.shape
    width = w3 // 3
    tm = min(tm, n)
    g = tm // 8
    ng = n // 8
    ospec = pl.BlockSpec((1, tm, width), lambda bi, i: (bi, i, 0))
    oshape = jax.ShapeDtypeStruct((b, n, width), F32)
    return pl.pallas_call(
        functools.partial(_short_conv_body, width=width),
        grid=(b, n // tm),
        in_specs=[pl.BlockSpec((1, tm, w3), lambda bi, i: (bi, i, 0)),
                  pl.BlockSpec((1, 8, w3), lambda bi, i: (bi, jnp.maximum(i * g - 1, 0), 0)),
                  pl.BlockSpec((1, 8, w3), lambda bi, i: (bi, jnp.minimum((i + 1) * g, ng - 1), 0)),
                  pl.BlockSpec((HY_SHORT, w3), lambda bi, i: (0, 0)),
                  pl.BlockSpec((1, w3), lambda bi, i: (0, 0))],
        out_specs=[ospec, ospec, ospec], out_shape=[oshape, oshape, oshape],
        compiler_params=_cparams(("parallel", "parallel")),
        name="short_conv",
    )(p, p, p, short_w, short_b[None])


def _mul_body(a_ref, b_ref, o_ref):
    o_ref[...] = a_ref[...] * b_ref[...]


def _mul(a, b, tm):
    bsz, n, d = a.shape
    tm = min(tm, n)
    spec = pl.BlockSpec((1, tm, d), lambda bi, i: (bi, i, 0))
    return pl.pallas_call(
        _mul_body, grid=(bsz, n // tm), in_specs=[spec, spec], out_specs=spec,
        out_shape=jax.ShapeDtypeStruct(a.shape, a.dtype),
        compiler_params=_cparams(("parallel", "parallel")), name="gate_mul",
    )(a, b)


def _hyena_core(p, short_w, short_b, w1, b1, f1, w2, b2, f2, w3, fbias, tm):
    n = p.shape[1]
    width = p.shape[2] // 3
    v, x1, x2 = _short_conv(p, short_w, short_b, tm)
    ks = _hyena_kernels(n, w1, b1, f1, w2, b2, f2, w3, width)
    if 2 * n // FFT_SLAB >= FFT_SLAB:
        conv = lambda z, o: _fftconv(z, *_filter_spectrum(ks[o]), fbias[o])
    else:
        conv = lambda z, o: _small_conv(z, ks[o], fbias[o])
    w0 = conv(v, 0)
    w1_ = conv(_mul(x1, w0, tm), 1)
    return w1_, x2


def _sink_rows(sink, hkv, grp, tq):
    return jnp.repeat(sink.astype(F32).reshape(hkv, grp), tq, axis=1)[..., None]


def kernel(x, c, ctx, c_ctx, ada_w, ada_b, norm_mix_g, norm_ffn_g, router_w, exp_w_gate, exp_w_up, exp_w_down,
           a_w_in, a_w_out, a_q_g, a_k_g, a_sink, b_w_in, b_short_w, b_short_b, b_w1, b_b1, b_f1, b_w2, b_b2,
           b_f2, b_w3, b_bias, b_w_out, c_w_in, c_w_out, c_q_g, c_k_g, c_rpb):
    bsz, n, d = x.shape
    nctx = ctx.shape[1]
    depth = ada_w.shape[0]
    ne = router_w.shape[2]
    cap = EC_CAPACITY * n // ne
    cap_c = EC_CAPACITY * nctx // ne
    tm = 256

    pad_rows = (-(bsz + 1)) % 8
    rows = jnp.concatenate([c, c_ctx[None], jnp.zeros((pad_rows, d), F32)], axis=0)
    mod_all = _ada_mod(rows, ada_w, ada_b)

    h, hc = x, ctx
    for i in range(depth):
        last = i == depth - 1
        kind, j = i % N_MIXERS, i // N_MIXERS
        mod = mod_all[i, :bsz].reshape(bsz, 1, 6, d)
        sh1, sc1, g1, sh2, sc2, g2 = [mod[:, :, t] for t in range(6)]
        modc = jnp.broadcast_to(mod_all[i, bsz].reshape(1, 1, 6, d), (bsz, 1, 6, d))
        csh1, csc1, cg1, csh2, csc2, cg2 = [modc[:, :, t] for t in range(6)]
        yc = o_gate = yc_gate = None
        if kind == 0:
            hkv, grp = A_KV_HEADS, a_w_in.shape[2] // HEAD_DIM // A_KV_HEADS - 2
            qw, kvw = hkv * grp * HEAD_DIM, hkv * HEAD_DIM
            w_in = a_w_in[j].astype(BF16)
            w_out = a_w_out[j].astype(BF16)
            q_rot, q_pl, k_rot, v = _proj_attn(h, sh1, sc1, norm_mix_g[i], w_in, a_q_g[j], a_k_g[j],
                                               qw, kvw, True, tm)
            qc, kc, vc = _proj_attn(hc, csh1, csc1, norm_mix_g[i], w_in, a_q_g[j], a_k_g[j],
                                    qw, kvw, False, tm)
            o = _local_attn(q_rot, q_pl, k_rot, v, kc, vc, _window_bias(n),
                            _sink_rows(a_sink[j], hkv, grp, A_BLOCK), hkv, grp, A_BLOCK)
            if not last:
                yc = _ctx_attn(qc, kc, vc, _sink_rows(a_sink[j], hkv, grp, nctx), hkv, grp)
        elif kind == 1:
            w_in = b_w_in[j].astype(BF16)
            w_out = b_w_out[j].astype(BF16)
            hy = (b_short_w[j], b_short_b[j], b_w1[j], b_b1[j], b_f1[j], b_w2[j], b_b2[j], b_f2[j], b_w3[j],
                  b_bias[j])
            o, o_gate = _hyena_core(_proj_plain(h, sh1, sc1, norm_mix_g[i], w_in, tm), *hy, tm)
            if not last:
                yc, yc_gate = _hyena_core(_proj_plain(hc, csh1, csc1, norm_mix_g[i], w_in, tm), *hy, tm)
        else:
            nh = c_w_in.shape[2] // HEAD_DIM // 3
            hw = nh * HEAD_DIM
            w_in = c_w_in[j].astype(BF16)
            w_out = c_w_out[j].astype(BF16)
            q, k, v = _proj_attn(h, sh1, sc1, norm_mix_g[i], w_in, c_q_g[j], c_k_g[j], hw, hw, False, tm)
            qc, kc, vc = _proj_attn(hc, csh1, csc1, norm_mix_g[i], w_in, c_q_g[j], c_k_g[j], hw, hw, False, tm)
            o = _local_attn(q, q, k, v, kc, vc, _neighbourhood_bias(c_rpb[j], n), None, nh, 1, C_BLOCK)
            if not last:
                yc = _ctx_attn(qc, kc, vc, None, nh, 1)

        rwt = router_w[i].T
        h1, u2, aff = _outproj(o, w_out, h, g1, norm_ffn_g[i], sh2, sc2, rwt, 2 * tm, o_gate)
        il, gl, pos_l = _select(aff, cap, n // SEL_BLOCK)
        base_l = (jnp.arange(bsz) * cap)[:, None, None]
        rows_all = (il + (jnp.arange(bsz) * n)[:, None, None]).transpose(1, 0, 2).reshape(ne, bsz * cap)
        gates_all = gl.transpose(1, 0, 2).reshape(ne, bsz * cap)
        u_all = u2.reshape(bsz * n, d)
        if not last:
            hc1, uc2, affc = _outproj(yc, w_out, hc, cg1, norm_ffn_g[i], csh2, csc2, rwt, 2 * tm, yc_gate)
            icx, gcx, pos_c = _select(affc, cap_c, n // SEL_BLOCK)
            rows_c = icx + (bsz * n + jnp.arange(bsz) * nctx)[:, None, None]
            rows_all = jnp.concatenate([rows_all, rows_c.transpose(1, 0, 2).reshape(ne, bsz * cap_c)], axis=1)
            gates_all = jnp.concatenate([gates_all, gcx.transpose(1, 0, 2).reshape(ne, bsz * cap_c)], axis=1)
            u_all = jnp.concatenate([u_all, uc2.reshape(bsz * nctx, d)], axis=0)
        r = rows_all.shape[1]
        xs = u_all[rows_all]
        ye = _expert_ffn(xs, gates_all[..., None], exp_w_gate, exp_w_up, exp_w_down, i, r // 4, 512)
        h = _combine(ye, jnp.where(pos_l >= 0, pos_l + base_l, -1), h1, g2, ne)
        if not last:
            base_c = (bsz * cap + jnp.arange(bsz) * cap_c)[:, None, None]
            hc = _combine(ye, jnp.where(pos_c >= 0, pos_c + base_c, -1), hc1, cg2, ne)
    return h
```

```python
import functools
import math

import numpy as np
import jax
import jax.numpy as jnp
from jax import lax
from jax.experimental import pallas as pl
from jax.experimental.pallas import tpu as pltpu

F32 = jnp.float32
BF16 = jnp.bfloat16
HI = lax.Precision.HIGHEST

GRID_W = 64
HEAD_DIM = 64
NORM_EPS = 1e-6
N_MIXERS = 3
A_KV_HEADS = 4
A_BLOCK = 128
ROPE_BASE = 10000.0
HY_ORDER = 2
HY_EMB_DIM = 33
HY_SHORT = 3
HY_FAST_DECAY = 0.3
HY_SLOW_DECAY = 1.5
HY_TARGET = 1e-2
NA_ROWS = 8
NA_COLS = 16
HEAD_GROUP = 4
C_BLOCK = 256
N_EXPERTS = 16
EC_CAPACITY = 2
NEG = -1e30

LANES = 128
VMEM_LIMIT = 56 * 1024 * 1024


def _cparams(sem):
    return pltpu.CompilerParams(dimension_semantics=sem, vmem_limit_bytes=VMEM_LIMIT)


def _sigmoid(x):
    return 1.0 / (1.0 + jnp.exp(-x))


def _modulate(h, g, shift, scale):
    ms = jnp.mean(h * h, axis=-1, keepdims=True)
    y = h * lax.rsqrt(ms + NORM_EPS)
    return (y * g) * (1.0 + scale) + shift


def _mod_body(s_ref, w_ref, b_ref, o_ref):
    s = s_ref[...]
    s = s * _sigmoid(s)
    o_ref[0] = jnp.dot(s, w_ref[0], precision=HI, preferred_element_type=F32) + b_ref[0]


def _ada_mod(rows, ada_w, ada_b):
    depth, d, n6 = ada_w.shape
    r = rows.shape[0]
    tn = 1536
    return pl.pallas_call(
        _mod_body,
        grid=(depth, n6 // tn),
        in_specs=[pl.BlockSpec((r, d), lambda l, j: (0, 0)),
                  pl.BlockSpec((1, d, tn), lambda l, j: (l, 0, j)),
                  pl.BlockSpec((1, 1, tn), lambda l, j: (l, 0, j))],
        out_specs=pl.BlockSpec((1, r, tn), lambda l, j: (l, 0, j)),
        out_shape=jax.ShapeDtypeStruct((depth, r, n6), F32),
        compiler_params=_cparams(("arbitrary", "arbitrary")),
        name="ada_mod",
    )(rows, ada_w, ada_b.reshape(depth, 1, n6))


def _dot_onehot(x, onehot_bf):
    x0 = x.astype(BF16)
    r1 = x - x0.astype(F32)
    x1 = r1.astype(BF16)
    x2 = (r1 - x1.astype(F32)).astype(BF16)
    return (jnp.dot(x0, onehot_bf, preferred_element_type=F32) + jnp.dot(x1, onehot_bf, preferred_element_type=F32)
            + jnp.dot(x2, onehot_bf, preferred_element_type=F32))


def _head_norm(x, gsum, gexp, gain):
    ss = _dot_onehot(x * x, gsum)
    r = lax.rsqrt(ss * (1.0 / HEAD_DIM) + NORM_EPS)
    rb = _dot_onehot(r, gexp)
    return (x * rb) * gain


def _rope(x, cos, sin_signed):
    rows, w = x.shape
    lane = lax.broadcasted_iota(jnp.int32, (rows, LANES), 1)
    first = (lane % 32) < 16
    outs = []
    for c in range(w // LANES):
        xc = x[:, c * LANES:(c + 1) * LANES]
        partner = jnp.where(first, pltpu.roll(xc, LANES - 16, 1), pltpu.roll(xc, 16, 1))
        outs.append(xc * cos + partner * sin_signed)
    return jnp.concatenate(outs, axis=1)


def _proj_attn_body(*refs, qw, kvw, rope):
    h_ref, sh_ref, sc_ref, g_ref, w_ref, qg_ref, kg_ref, gsq_ref, geq_ref, gsk_ref, gek_ref = refs[:11]
    u = _modulate(h_ref[0], g_ref[...], sh_ref[0], sc_ref[0])
    p = jnp.dot(u.astype(BF16), w_ref[...], preferred_element_type=F32)
    q = _head_norm(p[:, :qw], gsq_ref[...], geq_ref[...], qg_ref[...]) * (HEAD_DIM ** -0.5)
    k = _head_norm(p[:, qw:qw + kvw], gsk_ref[...], gek_ref[...], kg_ref[...])
    v = p[:, qw + kvw:]
    if rope:
        cos_ref, sin_ref, qr_out, qp_out, kr_out, v_out = refs[11:]
        cos, sin = cos_ref[...], sin_ref[...]
        qr_out[0] = _rope(q, cos, sin).astype(BF16)
        qp_out[0] = q.astype(BF16)
        kr_out[0] = _rope(k, cos, sin).astype(BF16)
        v_out[0] = v.astype(BF16)
    else:
        q_out, k_out, v_out = refs[11:]
        q_out[0] = q.astype(BF16)
        k_out[0] = k.astype(BF16)
        v_out[0] = v.astype(BF16)


def _group_mats(w):
    nh = w // HEAD_DIM
    gs = np.zeros((w, LANES), np.float32)
    gs[np.arange(w), np.arange(w) // HEAD_DIM] = 1.0
    return jnp.asarray(gs, BF16), jnp.asarray(gs.T.copy(), BF16)


def _rope_tables(n):
    t = jnp.arange(n)
    row = (t // GRID_W).astype(F32)
    col = (t % GRID_W).astype(F32)
    axis_dim = HEAD_DIM // 2
    inv_freq = 1.0 / (ROPE_BASE ** (jnp.arange(0, axis_dim, 2, dtype=F32) / axis_dim))
    ang_r = row[:, None] * inv_freq
    ang_c = col[:, None] * inv_freq
    cos64 = jnp.concatenate([jnp.cos(ang_r), jnp.cos(ang_r), jnp.cos(ang_c), jnp.cos(ang_c)], axis=-1)
    sin64 = jnp.concatenate([-jnp.sin(ang_r), jnp.sin(ang_r), -jnp.sin(ang_c), jnp.sin(ang_c)], axis=-1)
    return jnp.tile(cos64, (1, 2)), jnp.tile(sin64, (1, 2))


def _proj_attn(h, shift, scale, g, w_bf, q_g, k_g, qw, kvw, rope, tm):
    b, n, d = h.shape
    nout = w_bf.shape[1]
    tm = min(tm, n)
    gsq, geq = _group_mats(qw)
    gsk, gek = _group_mats(kvw)
    qg = jnp.tile(q_g, qw // HEAD_DIM)[None]
    kg = jnp.tile(k_g, kvw // HEAD_DIM)[None]
    full = lambda a: pl.BlockSpec(a.shape, lambda bi, i: (0,) * a.ndim)
    vec = pl.BlockSpec((1, 1, d), lambda bi, i: (bi, 0, 0))
    args = [h, shift, scale, g[None], w_bf, qg, kg, gsq, geq, gsk, gek]
    in_specs = [pl.BlockSpec((1, tm, d), lambda bi, i: (bi, i, 0)), vec, vec, full(args[3]), full(w_bf),
                full(qg), full(kg), full(gsq), full(geq), full(gsk), full(gek)]
    ospec = lambda w: pl.BlockSpec((1, tm, w), lambda bi, i: (bi, i, 0))
    oshape = lambda w: jax.ShapeDtypeStruct((b, n, w), BF16)
    if rope:
        cos, sin = _rope_tables(n)
        args += [cos, sin]
        in_specs += [pl.BlockSpec((tm, LANES), lambda bi, i: (i, 0))] * 2
        out_specs = [ospec(qw), ospec(qw), ospec(kvw), ospec(kvw)]
        out_shape = [oshape(qw), oshape(qw), oshape(kvw), oshape(kvw)]
    else:
        out_specs = [ospec(qw), ospec(kvw), ospec(kvw)]
        out_shape = [oshape(qw), oshape(kvw), oshape(kvw)]
    return pl.pallas_call(
        functools.partial(_proj_attn_body, qw=qw, kvw=kvw, rope=rope),
        grid=(b, n // tm), in_specs=in_specs, out_specs=out_specs, out_shape=out_shape,
        compiler_params=_cparams(("parallel", "parallel")),
        name="proj_attn_rope" if rope else "proj_attn",
    )(*args)


HALO = 8


def _proj_hyena_body(hc_ref, hp_ref, hn_ref, sh_ref, sc_ref, g_ref, w_ref, sw_ref, sb_ref, v_out, x1_out, x2_out,
                     *, width):
    i = pl.program_id(1)
    tm = hc_ref.shape[1]
    hall = jnp.concatenate([hp_ref[0], hc_ref[0], hn_ref[0]], axis=0)
    u = _modulate(hall, g_ref[...], sh_ref[0], sc_ref[0])
    p = jnp.dot(u.astype(BF16), w_ref[...], preferred_element_type=F32)
    cur = p[HALO:HALO + tm]
    up = jnp.where(i == 0, 0.0, p[HALO - 1:HALO])
    dn = jnp.where(i == pl.num_programs(1) - 1, 0.0, p[HALO + tm:HALO + tm + 1])
    prev = jnp.concatenate([up, cur[:tm - 1]], axis=0)
    nxt = jnp.concatenate([cur[1:], dn], axis=0)
    w = sw_ref[...]
    y = prev * w[0:1] + cur * w[1:2] + nxt * w[2:3] + sb_ref[...]
    v_out[0] = y[:, :width]
    x1_out[0] = y[:, width:2 * width]
    x2_out[0] = y[:, 2 * width:]


def _proj_hyena(h, shift, scale, g, w_bf, short_w, short_b, tm):
    b, n, d = h.shape
    w3 = w_bf.shape[1]
    width = w3 // 3
    tm = min(tm, n)
    grp, ngrp = tm // HALO, n // HALO
    vec = pl.BlockSpec((1, 1, d), lambda bi, i: (bi, 0, 0))
    ospec = pl.BlockSpec((1, tm, width), lambda bi, i: (bi, i, 0))
    oshape = jax.ShapeDtypeStruct((b, n, width), F32)
    return pl.pallas_call(
        functools.partial(_proj_hyena_body, width=width),
        grid=(b, n // tm),
        in_specs=[pl.BlockSpec((1, tm, d), lambda bi, i: (bi, i, 0)),
                  pl.BlockSpec((1, HALO, d), lambda bi, i: (bi, jnp.maximum(i * grp - 1, 0), 0)),
                  pl.BlockSpec((1, HALO, d), lambda bi, i: (bi, jnp.minimum((i + 1) * grp, ngrp - 1), 0)),
                  vec, vec,
                  pl.BlockSpec((1, d), lambda bi, i: (0, 0)),
                  pl.BlockSpec((d, w3), lambda bi, i: (0, 0)),
                  pl.BlockSpec((HY_SHORT, w3), lambda bi, i: (0, 0)),
                  pl.BlockSpec((1, w3), lambda bi, i: (0, 0))],
        out_specs=[ospec, ospec, ospec], out_shape=[oshape, oshape, oshape],
        compiler_params=_cparams(("parallel", "parallel")),
        name="proj_hyena",
    )(h, h, h, shift, scale, g[None], w_bf, short_w, short_b[None])


def _local_attn_body(*refs, hkv, grp, tq, use_sink, head_bias):
    (ql_ref, qc_ref, kp_ref, kc_ref, kn_ref, vp_ref, vc_ref, vn_ref, kx_ref, vx_ref, bias_ref) = refs[:11]
    if use_sink:
        sink_ref, o_ref = refs[11:]
    else:
        (o_ref,) = refs[11:]
    dh = HEAD_DIM
    nt = (((1,), (1,)), ((), ()))
    qcols = lambda hh: [slice((hh * grp + g) * dh, (hh * grp + g + 1) * dh) for g in range(grp)]
    for h0 in range(0, hkv, HEAD_GROUP):
        heads = range(h0, min(h0 + HEAD_GROUP, hkv))
        s, vall = [], []
        for hh in heads:
            ks = slice(hh * dh, (hh + 1) * dh)
            q_l = jnp.concatenate([ql_ref[0, :, c] for c in qcols(hh)], axis=0)
            q_c = jnp.concatenate([qc_ref[0, :, c] for c in qcols(hh)], axis=0)
            kw = jnp.concatenate([kp_ref[0, :, ks], kc_ref[0, :, ks], kn_ref[0, :, ks]], axis=0)
            s_loc = lax.dot_general(q_l, kw, nt, preferred_element_type=F32)
            s_loc = s_loc + bias_ref[0, hh if head_bias else 0]
            s_ctx = lax.dot_general(q_c, kx_ref[0, :, ks], nt, preferred_element_type=F32)
            s.append(jnp.concatenate([s_loc, s_ctx], axis=1))
            v = jnp.concatenate([vp_ref[0, :, ks], vc_ref[0, :, ks], vn_ref[0, :, ks], vx_ref[0, :, ks]], axis=0)
            vall.append(jnp.concatenate([v, jnp.ones_like(v)], axis=1))
        m = [jnp.max(x, axis=-1, keepdims=True) for x in s]
        if use_sink:
            sk = [sink_ref[hh] for hh in heads]
            m = [jnp.maximum(a, b) for a, b in zip(m, sk)]
        p = [jnp.exp(x - a).astype(BF16) for x, a in zip(s, m)]
        ox = [jnp.dot(x, v, preferred_element_type=F32) for x, v in zip(p, vall)]
        den = [x[:, dh:dh + 1] for x in ox]
        if use_sink:
            den = [d + jnp.exp(b - a) for d, a, b in zip(den, m, sk)]
        o = [x[:, :dh] * (1.0 / d) for x, d in zip(ox, den)]
        for hh, oh in zip(heads, o):
            for g, c in enumerate(qcols(hh)):
                o_ref[0, :, c] = oh[g * tq:(g + 1) * tq].astype(BF16)


def _local_attn(q_loc, q_ctx, k, v, kx, vx, bias, sink_rows, hkv, grp, tq):
    b, n, qw = q_loc.shape
    kvw = k.shape[2]
    nb = n // tq
    nctx = kx.shape[1]
    head_bias = bias.shape[1] > 1
    use_sink = sink_rows is not None
    qspec = pl.BlockSpec((1, tq, qw), lambda bi, i: (bi, i, 0))
    prev = pl.BlockSpec((1, tq, kvw), lambda bi, i: (bi, jnp.maximum(i - 1, 0), 0))
    cur = pl.BlockSpec((1, tq, kvw), lambda bi, i: (bi, i, 0))
    nxt = pl.BlockSpec((1, tq, kvw), lambda bi, i: (bi, jnp.minimum(i + 1, nb - 1), 0))
    xspec = pl.BlockSpec((1, nctx, kvw), lambda bi, i: (bi, 0, 0))
    bspec = pl.BlockSpec((1,) + bias.shape[1:],
                         lambda bi, i: (jnp.where(i == 0, 0, jnp.where(i == nb - 1, 2, 1)), 0, 0, 0))
    args = [q_loc, q_ctx, k, k, k, v, v, v, kx, vx, bias]
    in_specs = [qspec, qspec, prev, cur, nxt, prev, cur, nxt, xspec, xspec, bspec]
    if use_sink:
        args.append(sink_rows)
        in_specs.append(pl.BlockSpec(sink_rows.shape, lambda bi, i: (0, 0, 0)))
    return pl.pallas_call(
        functools.partial(_local_attn_body, hkv=hkv, grp=grp, tq=tq, use_sink=use_sink, head_bias=head_bias),
        grid=(b, nb), in_specs=in_specs,
        out_specs=pl.BlockSpec((1, tq, qw), lambda bi, i: (bi, i, 0)),
        out_shape=jax.ShapeDtypeStruct((b, n, qw), BF16),
        compiler_params=_cparams(("parallel", "parallel")),
        name="local_attn_sink" if use_sink else "local_attn",
    )(*args)


def _window_bias(n):
    tq = A_BLOCK
    grp = 4
    qi = np.arange(tq)[:, None]
    kj = np.arange(3 * tq)[None, :] - tq
    band = np.abs(qi - kj) <= A_BLOCK
    nb = n // tq
    out = []
    for which in range(3):
        blk = {0: 0, 1: min(1, nb - 1), 2: nb - 1}[which]
        pos = blk * tq + kj
        ok = band & (pos >= 0) & (pos < n)
        out.append(np.tile(np.where(ok, 0.0, NEG).astype(np.float32), (grp, 1)))
    return jnp.asarray(np.stack(out)[:, None])


def _nbr_bias_body(rpb_ref, colok_ref, o_ref, *, row_ok):
    gw = GRID_W
    col_ok = colok_ref[...] > 0.5
    neg = jnp.full((gw, gw), NEG, F32)
    toep = []
    for ro in range(2 * NA_ROWS - 1):
        x = jnp.broadcast_to(rpb_ref[0, ro:ro + 1, :], (gw, LANES))
        t = pltpu.roll(x, LANES - (NA_COLS - 1), 1, stride=1, stride_axis=0)[:, :gw]
        toep.append(jnp.where(col_ok, t, NEG))
    rq = C_BLOCK // gw
    for which in range(3):
        rows_out = []
        for qa in range(rq):
            blocks = [toep[(ka - rq) - qa + (NA_ROWS - 1)] if row_ok[which][qa][ka] else neg
                      for ka in range(3 * rq)]
            rows_out.append(jnp.concatenate(blocks, axis=1))
        o_ref[which, 0] = jnp.concatenate(rows_out, axis=0)


def _neighbourhood_bias(rpb, n):
    nh = rpb.shape[0]
    rows = n // GRID_W
    kr = min(NA_ROWS, rows)
    rq = C_BLOCK // GRID_W
    nb = n // C_BLOCK
    assert nb >= 3 and rpb.shape[1:] == (2 * NA_ROWS - 1, 2 * NA_COLS - 1)
    qc = np.arange(GRID_W)
    c0 = np.clip(qc - NA_COLS // 2, 0, GRID_W - NA_COLS)
    col_ok = ((qc[None, :] >= c0[:, None]) & (qc[None, :] < c0[:, None] + NA_COLS)).astype(np.float32)
    row_ok = []
    for which in range(3):
        blk = {0: 0, 1: 1, 2: nb - 1}[which]
        qr = blk * rq + np.arange(rq)
        kr_abs = blk * rq + np.arange(3 * rq) - rq
        r0 = np.clip(qr - kr // 2, 0, rows - kr)
        ok = (kr_abs[None, :] >= r0[:, None]) & (kr_abs[None, :] < r0[:, None] + kr)
        row_ok.append(tuple(tuple(bool(v) for v in row) for row in ok))
    rpb_pad = jnp.pad(rpb.astype(F32), ((0, 0), (0, 1), (0, LANES - rpb.shape[2])))
    return pl.pallas_call(
        functools.partial(_nbr_bias_body, row_ok=tuple(row_ok)),
        grid=(nh,),
        in_specs=[pl.BlockSpec((1, 2 * NA_ROWS, LANES), lambda hh: (hh, 0, 0)),
                  pl.BlockSpec((GRID_W, GRID_W), lambda hh: (0, 0))],
        out_specs=pl.BlockSpec((3, 1, C_BLOCK, 3 * C_BLOCK), lambda hh: (0, hh, 0, 0)),
        out_shape=jax.ShapeDtypeStruct((3, nh, C_BLOCK, 3 * C_BLOCK), F32),
        compiler_params=_cparams(("parallel",)),
        name="nbr_bias",
    )(rpb_pad, jnp.asarray(col_ok))


def _ctx_attn_body(*refs, hkv, grp, use_sink):
    q_ref, k_ref, v_ref = refs[:3]
    if use_sink:
        sink_ref, o_ref = refs[3:]
    else:
        (o_ref,) = refs[3:]
    dh = HEAD_DIM
    nq = q_ref.shape[1]
    nt = (((1,), (1,)), ((), ()))
    for hh in range(hkv):
        ks = slice(hh * dh, (hh + 1) * dh)
        qcols = [slice((hh * grp + g) * dh, (hh * grp + g + 1) * dh) for g in range(grp)]
        q = jnp.concatenate([q_ref[0, :, c] for c in qcols], axis=0)
        s = lax.dot_general(q, k_ref[0, :, ks], nt, preferred_element_type=F32)
        m = jnp.max(s, axis=-1, keepdims=True)
        if use_sink:
            sk = sink_ref[hh]
            m = jnp.maximum(m, sk)
        e = jnp.exp(s - m)
        den = jnp.sum(e, axis=-1, keepdims=True)
        if use_sink:
            den = den + jnp.exp(sk - m)
        o = jnp.dot((e * (1.0 / den)).astype(BF16), v_ref[0, :, ks], preferred_element_type=F32)
        for g in range(grp):
            o_ref[0, :, qcols[g]] = o[g * nq:(g + 1) * nq].astype(BF16)


def _ctx_attn(q, k, v, sink_rows, hkv, grp):
    b, nq, qw = q.shape
    kvw = k.shape[2]
    use_sink = sink_rows is not None
    args = [q, k, v]
    in_specs = [pl.BlockSpec((1, nq, qw), lambda bi: (bi, 0, 0)),
                pl.BlockSpec((1, nq, kvw), lambda bi: (bi, 0, 0)),
                pl.BlockSpec((1, nq, kvw), lambda bi: (bi, 0, 0))]
    if use_sink:
        args.append(sink_rows)
        in_specs.append(pl.BlockSpec(sink_rows.shape, lambda bi: (0, 0, 0)))
    return pl.pallas_call(
        functools.partial(_ctx_attn_body, hkv=hkv, grp=grp, use_sink=use_sink),
        grid=(b,), in_specs=in_specs,
        out_specs=pl.BlockSpec((1, nq, qw), lambda bi: (bi, 0, 0)),
        out_shape=jax.ShapeDtypeStruct((b, nq, qw), BF16),
        compiler_params=_cparams(("parallel",)),
        name="ctx_attn_sink" if use_sink else "ctx_attn",
    )(*args)


def _outproj_body(*refs, gated, aliased):
    h1_out, u2_out, aff_out = refs[-3:]
    ins = refs[:-4] if aliased else refs[:-3]
    if gated:
        o_ref, og_ref, w_ref, h_ref, g1_ref, gn_ref, sh2_ref, sc2_ref, rwt_ref = ins
        o = o_ref[0] * og_ref[0]
    else:
        o_ref, w_ref, h_ref, g1_ref, gn_ref, sh2_ref, sc2_ref, rwt_ref = ins
        o = o_ref[0]
    y = jnp.dot(o.astype(BF16), w_ref[...], preferred_element_type=F32)
    h1 = h_ref[0] + g1_ref[0] * y
    h1_out[0] = h1
    u2 = _modulate(h1, gn_ref[...], sh2_ref[0], sc2_ref[0])
    u2_out[...] = u2.astype(BF16)
    lt = lax.dot_general(rwt_ref[...], u2, (((1,), (1,)), ((), ())), precision=HI, preferred_element_type=F32)
    e = jnp.exp(lt - jnp.max(lt, axis=0, keepdims=True))
    aff_out[0] = e / jnp.sum(e, axis=0, keepdims=True)


def _outproj(o, w_bf, h, g1, gn, sh2, sc2, rwt, tm, o_gate=None, u_rows=None, u_row0=0, u_buf=None):
    b, n, d = h.shape
    u_rows = b * n if u_rows is None else u_rows
    kin = o.shape[2]
    ne = rwt.shape[0]
    tm = min(tm, n)
    vec = pl.BlockSpec((1, 1, d), lambda bi, i: (bi, 0, 0))
    ospec = pl.BlockSpec((1, tm, kin), lambda bi, i: (bi, i, 0))
    gated = o_gate is not None
    aliased = u_buf is not None
    nblk, blk0 = n // tm, u_row0 // tm
    n_in = 9 if gated else 8
    return pl.pallas_call(
        functools.partial(_outproj_body, gated=gated, aliased=aliased),
        grid=(b, n // tm),
        in_specs=([ospec, ospec] if gated else [ospec]) + [
                  pl.BlockSpec((kin, d), lambda bi, i: (0, 0)),
                  pl.BlockSpec((1, tm, d), lambda bi, i: (bi, i, 0)),
                  vec, pl.BlockSpec((1, d), lambda bi, i: (0, 0)), vec, vec,
                  pl.BlockSpec((ne, d), lambda bi, i: (0, 0))]
                 + ([pl.BlockSpec(memory_space=pl.ANY)] if aliased else []),
        out_specs=[pl.BlockSpec((1, tm, d), lambda bi, i: (bi, i, 0)),
                   pl.BlockSpec((tm, d), lambda bi, i: (blk0 + bi * nblk + i, 0)),
                   pl.BlockSpec((1, ne, tm), lambda bi, i: (bi, 0, i))],
        out_shape=[jax.ShapeDtypeStruct((b, n, d), F32),
                   jax.ShapeDtypeStruct((u_rows, d), BF16),
                   jax.ShapeDtypeStruct((b, ne, n), F32)],
        input_output_aliases={n_in: 1} if aliased else {},
        compiler_params=_cparams(("parallel", "parallel")),
        name="outproj_router",
    )(*((o, o_gate) if gated else (o,)), w_bf, h, g1, gn[None], sh2, sc2, rwt, *((u_buf,) if aliased else ()))


SEL_BLOCK = 128


def _bcast_lane_sum(x):
    return jnp.broadcast_to(jnp.sum(x, axis=-1, keepdims=True), x.shape)


def _block_prefix(tot):
    run = jnp.zeros(tot.shape[1:], F32)
    out = []
    for j in range(tot.shape[0]):
        out.append(run)
        run = run + tot[j]
    return jnp.stack(out)


def _select_body(aff_ref, tri_ref, trib_ref, idx_ref, gate_ref, pos_ref, inc_ref, off_ref, mk_ref, *, cap, ne):
    nb = aff_ref.shape[1] // ne
    x = aff_ref[0].reshape(nb, ne, SEL_BLOCK)
    bits = pltpu.bitcast(x, jnp.int32)
    count = lambda m: _bcast_lane_sum(jnp.sum(m.astype(F32), axis=0))

    def step(i, prefix):
        cand = prefix | jnp.left_shift(jnp.int32(1), 30 - i)
        return jnp.where(count(bits >= cand[None]) >= cap, cand, prefix)

    thr = lax.fori_loop(0, 31, step, jnp.zeros((ne, SEL_BLOCK), jnp.int32))
    gt = bits > thr[None]
    eq = bits == thr[None]
    need = cap - count(gt)

    def prefix_counts(m):
        mf = m.astype(BF16).reshape(nb * ne, SEL_BLOCK)
        inc = jnp.dot(mf, tri_ref[...], preferred_element_type=F32).reshape(nb, ne, SEL_BLOCK)
        tot = jnp.broadcast_to(inc[:, :, SEL_BLOCK - 1:], inc.shape)
        return inc, tot, _block_prefix(tot)

    e_inc, _, e_off = prefix_counts(eq)
    rank_eq = e_inc + e_off - eq.astype(F32)
    mask = gt | (eq & (rank_eq < need[None]))
    inc, tot, off = prefix_counts(mask)
    inc_ref[...] = inc.reshape(nb * ne, SEL_BLOCK)
    off_ref[...] = off.reshape(nb * ne, SEL_BLOCK)
    mk_ref[...] = mask.astype(F32).reshape(nb * ne, SEL_BLOCK)
    pos_ref[0] = jnp.where(mask, inc + off - 1.0, -1.0).astype(jnp.int32).reshape(nb * ne, SEL_BLOCK)

    slot = lax.broadcasted_iota(jnp.int32, (cap, 1), 0).astype(F32)
    lane_nb = lax.broadcasted_iota(jnp.int32, (cap, nb), 1).astype(F32)
    lane = lax.broadcasted_iota(jnp.int32, (cap, SEL_BLOCK), 1).astype(F32)
    ones = jnp.ones((8, SEL_BLOCK), BF16)
    for e in range(ne):
        rows = pl.ds(e, nb, stride=ne)
        mk = mk_ref[rows, :].astype(BF16)
        cnt_row = lax.dot_general(ones, mk, (((1,), (1,)), ((), ())), preferred_element_type=F32)
        pinc_row = jnp.dot(cnt_row.astype(BF16), trib_ref[...], preferred_element_type=F32)[0:1]
        blk = jnp.sum((pinc_row <= slot).astype(F32), axis=-1, keepdims=True)
        onehot = (lane_nb == blk).astype(BF16)
        rhs = jnp.concatenate([inc_ref[rows, :], off_ref[rows, :], aff_ref[0, rows, :]], axis=1)
        r0 = rhs.astype(BF16)
        d1 = rhs - r0.astype(F32)
        r1 = d1.astype(BF16)
        r2 = (d1 - r1.astype(F32)).astype(BF16)
        got = (jnp.dot(onehot, r0, preferred_element_type=F32) + jnp.dot(onehot, r1, preferred_element_type=F32)
               + jnp.dot(onehot, r2, preferred_element_type=F32))
        rank = slot - got[:, SEL_BLOCK:SEL_BLOCK + 1]
        local = jnp.sum((got[:, :SEL_BLOCK] <= rank).astype(F32), axis=-1, keepdims=True)
        gate = jnp.sum(jnp.where(lane == local, got[:, 2 * SEL_BLOCK:], 0.0), axis=-1, keepdims=True)
        idx_ref[0, :, e:e + 1] = (blk * SEL_BLOCK + local).astype(jnp.int32)
        gate_ref[0, :, e:e + 1] = gate


def _select(aff, cap, nb_pad):
    bsz, ne, n = aff.shape
    pad = nb_pad * SEL_BLOCK - n
    a = jnp.pad(aff, ((0, 0), (0, 0), (0, pad))).reshape(bsz, ne, nb_pad, SEL_BLOCK).transpose(0, 2, 1, 3)
    a = a.reshape(bsz, nb_pad * ne, SEL_BLOCK)
    tri = jnp.asarray(np.triu(np.ones((SEL_BLOCK, SEL_BLOCK), np.float32)), BF16)
    trib = jnp.asarray(np.triu(np.ones((nb_pad, nb_pad), np.float32)), BF16)
    idx, gates, pos = pl.pallas_call(
        functools.partial(_select_body, cap=cap, ne=ne),
        grid=(bsz,),
        in_specs=[pl.BlockSpec((1, nb_pad * ne, SEL_BLOCK), lambda b: (b, 0, 0)),
                  pl.BlockSpec(tri.shape, lambda b: (0, 0)), pl.BlockSpec(trib.shape, lambda b: (0, 0))],
        out_specs=[pl.BlockSpec((1, cap, ne), lambda b: (b, 0, 0))] * 2
                  + [pl.BlockSpec((1, nb_pad * ne, SEL_BLOCK), lambda b: (b, 0, 0))],
        out_shape=[jax.ShapeDtypeStruct((bsz, cap, ne), jnp.int32), jax.ShapeDtypeStruct((bsz, cap, ne), F32),
                   jax.ShapeDtypeStruct((bsz, nb_pad * ne, SEL_BLOCK), jnp.int32)],
        scratch_shapes=[pltpu.VMEM((nb_pad * ne, SEL_BLOCK), F32)] * 3,
        compiler_params=_cparams(("parallel",)),
        name="expert_select",
    )(a, tri, trib)
    pos = pos.reshape(bsz, nb_pad, ne, SEL_BLOCK).transpose(0, 1, 3, 2).reshape(bsz, nb_pad * SEL_BLOCK, ne)
    return idx.transpose(0, 2, 1), gates.transpose(0, 2, 1), pos[:, :n]


def _expert_body(xs_ref, wg_ref, wu_ref, wd_ref, gate_ref, o_ref, acc_ref):
    f = pl.program_id(2)
    xs = xs_ref[0]
    a = jnp.dot(xs, wg_ref[...].astype(BF16), preferred_element_type=F32)
    bb = jnp.dot(xs, wu_ref[...].astype(BF16), preferred_element_type=F32)
    hid = (a * _sigmoid(a)) * bb
    y = jnp.dot(hid.astype(BF16), wd_ref[...].astype(BF16), preferred_element_type=F32)

    @pl.when(f == 0)
    def _():
        acc_ref[...] = y

    @pl.when(f > 0)
    def _():
        acc_ref[...] += y

    @pl.when(f == pl.num_programs(2) - 1)
    def _():
        o_ref[0] = (acc_ref[...] * gate_ref[0]).astype(o_ref.dtype)


def _expert_ffn(xs, gates, w_gate, w_up, w_down, layer, tr, tf):
    ne, r, d = xs.shape
    ff = w_gate.shape[3]
    return pl.pallas_call(
        _expert_body,
        grid=(ne, r // tr, ff // tf),
        in_specs=[pl.BlockSpec((1, tr, d), lambda e, c, f: (e, c, 0)),
                  pl.BlockSpec((None, None, d, tf), lambda e, c, f: (layer, e, 0, f)),
                  pl.BlockSpec((None, None, d, tf), lambda e, c, f: (layer, e, 0, f)),
                  pl.BlockSpec((None, None, tf, d), lambda e, c, f: (layer, e, f, 0)),
                  pl.BlockSpec((1, tr, 1), lambda e, c, f: (e, c, 0))],
        out_specs=pl.BlockSpec((1, tr, d), lambda e, c, f: (e, c, 0)),
        out_shape=jax.ShapeDtypeStruct((ne, r, d), BF16),
        scratch_shapes=[pltpu.VMEM((tr, d), F32)],
        compiler_params=_cparams(("parallel", "parallel", "arbitrary")),
        name="expert_ffn",
    )(xs, w_gate, w_up, w_down, gates)


COMBINE_TILE = 256
COMBINE_WIN = 128
BF16_ROWS = 16


def _combine_body(ws_ref, cnt_ref, ye_hbm, pos_ref, h_ref, g_ref, o_ref, buf, xbuf, acc_ref, sem, xsem,
                  *, ne, nt, rows_total):
    win = COMBINE_WIN
    step = pl.program_id(0) * nt + pl.program_id(1)
    nsteps = pl.num_programs(0) * nt
    slot = step % 2

    def clamp(st):
        return pl.multiple_of(jnp.minimum(st, rows_total - win), BF16_ROWS)

    def first_start(s_idx, e):
        return clamp((ws_ref[s_idx * ne + e] // BF16_ROWS) * BF16_ROWS)

    def window_copy(s_idx, sl, e):
        return pltpu.make_async_copy(ye_hbm.at[e, pl.ds(first_start(s_idx, e), win), :], buf.at[sl, e],
                                     sem.at[sl, e])

    @pl.when(step == 0)
    def _():
        for e in range(ne):
            window_copy(0, 0, e).start()

    @pl.when(step + 1 < nsteps)
    def _():
        for e in range(ne):
            window_copy(step + 1, 1 - slot, e).start()

    tm = pos_ref.shape[1]
    d = h_ref.shape[2]
    lane = lax.broadcasted_iota(jnp.int32, (tm, win), 1)
    pos = [pos_ref[0, :, e:e + 1] for e in range(ne)]
    st0 = [first_start(step, e) for e in range(ne)]
    onehot = jnp.concatenate([((pos[e] - st0[e]) == lane).astype(BF16) for e in range(ne)], axis=1)
    for e in range(ne):
        window_copy(step, slot, e).wait()
    acc_ref[...] = jnp.dot(onehot, buf[slot].reshape(ne * win, d), preferred_element_type=F32)
    nwin = [(ws_ref[step * ne + e] + cnt_ref[step * ne + e] - st0[e] + win - 1) // win for e in range(ne)]
    extra = nwin[0]
    for e in range(1, ne):
        extra = jnp.maximum(extra, nwin[e])

    @pl.when(extra > 1)
    def _():
        for e in range(ne):
            def more(k, carry, e=e):
                lo = st0[e] + k * win
                st = clamp(lo)
                cp = pltpu.make_async_copy(ye_hbm.at[e, pl.ds(st, win), :], xbuf, xsem)
                cp.start()
                cp.wait()
                oh = (((pos[e] - st) == lane) & (pos[e] >= lo)).astype(BF16)
                acc_ref[...] += jnp.dot(oh, xbuf[...], preferred_element_type=F32)
                return carry

            lax.fori_loop(1, nwin[e], more, 0)

    o_ref[0] = h_ref[0] + g_ref[0] * acc_ref[...]


def _combine(ye, pos_global, idx, row_base, h, g, ne):
    bsz, n, d = h.shape
    tm = min(COMBINE_TILE, n)
    nt = n // tm
    rows_total = ye.shape[1]
    below = jnp.sum(idx[..., None] < jnp.arange(nt + 1) * tm, axis=2).astype(jnp.int32)
    ws = (row_base[:, None, None] + below[..., :-1]).transpose(0, 2, 1)
    cnt = (below[..., 1:] - below[..., :-1]).transpose(0, 2, 1)
    grid_spec = pltpu.PrefetchScalarGridSpec(
        num_scalar_prefetch=2,
        grid=(bsz, nt),
        in_specs=[pl.BlockSpec(memory_space=pl.ANY),
                  pl.BlockSpec((1, tm, ne), lambda b, t, ws, cnt: (b, t, 0)),
                  pl.BlockSpec((1, tm, d), lambda b, t, ws, cnt: (b, t, 0)),
                  pl.BlockSpec((1, 1, d), lambda b, t, ws, cnt: (b, 0, 0))],
        out_specs=pl.BlockSpec((1, tm, d), lambda b, t, ws, cnt: (b, t, 0)),
        scratch_shapes=[pltpu.VMEM((2, ne, COMBINE_WIN, d), BF16), pltpu.VMEM((COMBINE_WIN, d), BF16),
                        pltpu.VMEM((tm, d), F32), pltpu.SemaphoreType.DMA((2, ne)), pltpu.SemaphoreType.DMA(())])
    return pl.pallas_call(
        functools.partial(_combine_body, ne=ne, nt=nt, rows_total=rows_total),
        grid_spec=grid_spec,
        out_shape=jax.ShapeDtypeStruct((bsz, n, d), F32),
        compiler_params=_cparams(("arbitrary", "arbitrary")),
        name="moe_combine",
    )(ws.reshape(-1), cnt.reshape(-1), ye, pos_global, h, g)


FFT_SLAB = 128
FFT_UNROLL = 8


def _stack_complex(m):
    return np.block([[m.real, -m.imag], [m.imag, m.real]])


def _dft_consts(n_total, n_nonzero):
    n1 = n_total // FFT_SLAB
    na = n_nonzero // FFT_SLAB
    idx1 = np.arange(n1)
    f1 = np.exp(-2j * np.pi * np.outer(idx1, idx1) / n1)
    idx2 = np.arange(FFT_SLAB)
    f2 = np.exp(-2j * np.pi * np.outer(idx2, idx2) / FFT_SLAB)
    tw = np.exp(-2j * np.pi * np.outer(idx1, idx2) / n_total)
    return dict(
        m1=_stack_complex(f1[:, :na]),
        m1_real=np.concatenate([f1.real, f1.imag], axis=0),
        m2=_stack_complex(f2),
        m2i=_stack_complex(np.conj(f2).T / n_total),
        m1i=_stack_complex(np.conj(f1).T[:na, :]),
        twr=tw.real, twi=tw.imag)


def _lane_replicated(row_ref, c, ct):
    t = jnp.broadcast_to(row_ref[pl.ds(c, 1), :], (FFT_SLAB, FFT_SLAB)).T
    return t if ct == FFT_SLAB else jnp.tile(t, (1, ct // FFT_SLAB))


def _dot_split(mh_ref, ml_ref, x):
    xh = x.astype(BF16)
    xl = (x - xh.astype(F32)).astype(BF16)
    mh = mh_ref[...]
    return (jnp.dot(mh, xh, preferred_element_type=F32) + jnp.dot(mh, xl, preferred_element_type=F32)
            + jnp.dot(ml_ref[...], xh, preferred_element_type=F32))


def _spectrum_body(k_ref, m1h_ref, m1l_ref, m2h_ref, m2l_ref, twr_ref, twi_ref, or_ref, oi_ref, *, n1):
    ct = k_ref.shape[1]

    def first(b, carry):
        a = _dot_split(m1h_ref, m1l_ref, k_ref[pl.ds(b, n1, stride=FFT_SLAB), :])
        or_ref[pl.ds(b, n1, stride=FFT_SLAB), :] = a[:n1]
        oi_ref[pl.ds(b, n1, stride=FFT_SLAB), :] = a[n1:]
        return carry

    lax.fori_loop(0, FFT_SLAB, first, 0, unroll=FFT_UNROLL)

    def slab(c, carry):
        r0 = pl.multiple_of(c * FFT_SLAB, FFT_SLAB)
        ar, ai = or_ref[pl.ds(r0, FFT_SLAB), :], oi_ref[pl.ds(r0, FFT_SLAB), :]
        tr, ti = _lane_replicated(twr_ref, c, ct), _lane_replicated(twi_ref, c, ct)
        p = jnp.concatenate([ar * tr - ai * ti, ar * ti + ai * tr], axis=0)
        x = _dot_split(m2h_ref, m2l_ref, p)
        or_ref[pl.ds(r0, FFT_SLAB), :] = x[:FFT_SLAB]
        oi_ref[pl.ds(r0, FFT_SLAB), :] = x[FFT_SLAB:]
        return carry

    lax.fori_loop(0, n1, slab, 0, unroll=FFT_UNROLL)


def _split_bf16(m):
    hi = jnp.asarray(m, F32).astype(BF16)
    lo = (jnp.asarray(m, F32) - hi.astype(F32)).astype(BF16)
    return hi, lo


def _filter_spectrum(k, ct=128):
    n_total, ch = k.shape
    n1 = n_total // FFT_SLAB
    cs = _dft_consts(n_total, n_total)
    consts = [*_split_bf16(cs["m1_real"]), *_split_bf16(cs["m2"]),
              jnp.asarray(cs["twr"], F32), jnp.asarray(cs["twi"], F32)]
    full = lambda a: pl.BlockSpec(a.shape, lambda j: (0,) * a.ndim)
    col = pl.BlockSpec((n_total, ct), lambda j: (0, j))
    return pl.pallas_call(
        functools.partial(_spectrum_body, n1=n1),
        grid=(ch // ct,),
        in_specs=[col] + [full(a) for a in consts],
        out_specs=[col, col],
        out_shape=[jax.ShapeDtypeStruct((n_total, ch), F32)] * 2,
        compiler_params=_cparams(("parallel",)),
        name="filter_spectrum",
    )(k, *consts)


def _fftconv_body(z_ref, kr_ref, ki_ref, bias_ref, m1_ref, m2_ref, m2i_ref, m1i_ref, twr_ref, twi_ref,
                  o_ref, wr_ref, wi_ref, *, n1, na):
    ct = z_ref.shape[2]

    def first(b, carry):
        rows = pl.ds(b, na, stride=FFT_SLAB)
        s = jnp.concatenate([z_ref[0, rows, :], z_ref[1, rows, :]], axis=0).astype(BF16)
        a = jnp.dot(m1_ref[...], s, preferred_element_type=F32)
        wr_ref[pl.ds(b, n1, stride=FFT_SLAB), :] = a[:n1]
        wi_ref[pl.ds(b, n1, stride=FFT_SLAB), :] = a[n1:]
        return carry

    lax.fori_loop(0, FFT_SLAB, first, 0, unroll=FFT_UNROLL)

    def slab(c, carry):
        r0 = pl.multiple_of(c * FFT_SLAB, FFT_SLAB)
        rows = pl.ds(r0, FFT_SLAB)
        ar, ai = wr_ref[rows, :], wi_ref[rows, :]
        tr, ti = _lane_replicated(twr_ref, c, ct), _lane_replicated(twi_ref, c, ct)
        p = jnp.concatenate([ar * tr - ai * ti, ar * ti + ai * tr], axis=0).astype(BF16)
        x = jnp.dot(m2_ref[...], p, preferred_element_type=F32)
        xr, xi = x[:FFT_SLAB], x[FFT_SLAB:]
        kr, ki = kr_ref[rows, :], ki_ref[rows, :]
        y = jnp.concatenate([xr * kr - xi * ki, xr * ki + xi * kr], axis=0).astype(BF16)
        bb = jnp.dot(m2i_ref[...], y, preferred_element_type=F32)
        br, bi = bb[:FFT_SLAB], bb[FFT_SLAB:]
        wr_ref[rows, :] = br * tr + bi * ti
        wi_ref[rows, :] = bi * tr - br * ti
        return carry

    lax.fori_loop(0, n1, slab, 0, unroll=FFT_UNROLL)

    def last(b, carry):
        rows = pl.ds(b, n1, stride=FFT_SLAB)
        s = jnp.concatenate([wr_ref[rows, :], wi_ref[rows, :]], axis=0).astype(BF16)
        y = jnp.dot(m1i_ref[...], s, preferred_element_type=F32)
        orow = pl.ds(b, na, stride=FFT_SLAB)
        bias = bias_ref[...]
        o_ref[0, orow, :] = y[:na] + z_ref[0, orow, :] * bias
        o_ref[1, orow, :] = y[na:] + z_ref[1, orow, :] * bias
        return carry

    lax.fori_loop(0, FFT_SLAB, last, 0, unroll=FFT_UNROLL)


def _fftconv(z, kf_re, kf_im, bias, ct=128):
    bsz, n, ch = z.shape
    n_total = 2 * n
    n1, na = n_total // FFT_SLAB, n // FFT_SLAB
    cs = _dft_consts(n_total, n)
    mats = [jnp.asarray(cs[name], BF16) for name in ("m1", "m2", "m2i", "m1i")]
    tws = [jnp.asarray(cs[name], F32) for name in ("twr", "twi")]
    full = lambda a: pl.BlockSpec(a.shape, lambda j, p: (0,) * a.ndim)
    single = pl.Buffered(1)
    zspec = pl.BlockSpec((2, n, ct), lambda j, p: (p, 0, j), pipeline_mode=single)
    kspec = pl.BlockSpec((n_total, ct), lambda j, p: (0, j), pipeline_mode=single)
    return pl.pallas_call(
        functools.partial(_fftconv_body, n1=n1, na=na),
        grid=(ch // ct, bsz // 2),
        in_specs=[zspec, kspec, kspec, pl.BlockSpec((1, ct), lambda j, p: (0, j))]
                 + [full(a) for a in mats] + [full(a) for a in tws],
        out_specs=pl.BlockSpec((2, n, ct), lambda j, p: (p, 0, j), pipeline_mode=single),
        out_shape=jax.ShapeDtypeStruct((bsz, n, ch), F32),
        scratch_shapes=[pltpu.VMEM((n_total, ct), F32), pltpu.VMEM((n_total, ct), F32)],
        compiler_params=_cparams(("parallel", "parallel")),
        name="fftconv",
    )(z, kf_re, kf_im, bias[None].astype(F32), *mats, *tws)


def _hyena_kernels(n, w1, b1, f1, w2, b2, f2, w3, width):
    t = jnp.linspace(0.0, 1.0, n, dtype=F32)[:, None]
    bands = (HY_EMB_DIM - 1) // 2
    w = 2.0 * math.pi * jnp.arange(n, dtype=F32)[:, None] / n
    f = jnp.linspace(1e-4, bands - 1, bands, dtype=F32)[None, :]
    z = jnp.concatenate([t, jnp.cos(f * w), -jnp.sin(f * w)], axis=-1)
    h = jnp.sin(f1.astype(F32) * (jnp.dot(z, w1.astype(F32), precision=HI) + b1.astype(F32)))
    h = jnp.sin(f2.astype(F32) * (jnp.dot(h, w2.astype(F32), precision=HI) + b2.astype(F32)))
    max_decay = math.log(HY_TARGET) / HY_FAST_DECAY
    min_decay = math.log(HY_TARGET) / HY_SLOW_DECAY
    deltas = jnp.abs(jnp.linspace(min_decay, max_decay, width, dtype=F32))[None, :]
    h_rev, t_rev = h[1:][::-1], t[1:][::-1]
    w3r = w3.astype(F32).reshape(w3.shape[0], HY_ORDER, 2, width)
    out = []
    for o in range(HY_ORDER):
        fwd = jnp.dot(h, w3r[:, o, 0], precision=HI) * jnp.exp(-t * deltas)
        bwd = jnp.dot(h_rev, w3r[:, o, 1], precision=HI) * jnp.exp(-t_rev * deltas)
        k = jnp.concatenate([fwd, jnp.zeros((1, width), F32), bwd], axis=0)
        out.append(k * lax.rsqrt(jnp.sum(k * k, axis=0, keepdims=True) + NORM_EPS))
    return out


def _small_conv_body(z_ref, k_ref, bias_ref, mfk_ref, mf_ref, mi_ref, o_ref):
    bsz, n, _ = z_ref.shape
    nt = 2 * n
    kf = jnp.dot(mfk_ref[...], k_ref[...], precision=HI, preferred_element_type=F32)
    kr, ki = kf[:nt], kf[nt:]
    bias = bias_ref[...]
    for p in range(bsz // 2):
        z0, z1 = z_ref[2 * p], z_ref[2 * p + 1]
        x = jnp.dot(mf_ref[...], jnp.concatenate([z0, z1], axis=0).astype(BF16), preferred_element_type=F32)
        xr, xi = x[:nt], x[nt:]
        y = jnp.concatenate([xr * kr - xi * ki, xr * ki + xi * kr], axis=0).astype(BF16)
        w = jnp.dot(mi_ref[...], y, preferred_element_type=F32)
        o_ref[2 * p] = w[:n] + z0 * bias
        o_ref[2 * p + 1] = w[n:] + z1 * bias


def _small_conv(z, k, bias, ct=256):
    bsz, n, ch = z.shape
    nt = 2 * n
    idx = np.arange(nt)
    f = np.exp(-2j * np.pi * np.outer(idx, idx) / nt)
    mfk = jnp.asarray(np.concatenate([f.real, f.imag], axis=0), F32)
    mf = jnp.asarray(_stack_complex(f[:, :n]), BF16)
    mi = jnp.asarray(_stack_complex(np.conj(f).T[:n, :] / nt), BF16)
    full = lambda a: pl.BlockSpec(a.shape, lambda j: (0,) * a.ndim)
    return pl.pallas_call(
        _small_conv_body,
        grid=(ch // ct,),
        in_specs=[pl.BlockSpec((bsz, n, ct), lambda j: (0, 0, j)), pl.BlockSpec((nt, ct), lambda j: (0, j)),
                  pl.BlockSpec((1, ct), lambda j: (0, j)), full(mfk), full(mf), full(mi)],
        out_specs=pl.BlockSpec((bsz, n, ct), lambda j: (0, 0, j)),
        out_shape=jax.ShapeDtypeStruct((bsz, n, ch), F32),
        compiler_params=_cparams(("parallel",)),
        name="small_conv",
    )(z, k, bias[None].astype(F32), mfk, mf, mi)


def _mul_body(a_ref, b_ref, o_ref):
    o_ref[...] = a_ref[...] * b_ref[...]


def _mul(a, b, tm):
    bsz, n, d = a.shape
    tm = min(tm, n)
    spec = pl.BlockSpec((1, tm, d), lambda bi, i: (bi, i, 0))
    return pl.pallas_call(
        _mul_body, grid=(bsz, n // tm), in_specs=[spec, spec], out_specs=spec,
        out_shape=jax.ShapeDtypeStruct(a.shape, a.dtype),
        compiler_params=_cparams(("parallel", "parallel")), name="gate_mul",
    )(a, b)


def _hyena_core(h, shift, scale, g, w_in_bf, short_w, short_b, w1, b1, f1, w2, b2, f2, w3, fbias, tm):
    n = h.shape[1]
    width = w_in_bf.shape[1] // 3
    v, x1, x2 = _proj_hyena(h, shift, scale, g, w_in_bf, short_w, short_b, tm)
    ks = _hyena_kernels(n, w1, b1, f1, w2, b2, f2, w3, width)
    if 2 * n // FFT_SLAB >= FFT_SLAB:
        conv = lambda z, o: _fftconv(z, *_filter_spectrum(ks[o]), fbias[o])
    else:
        conv = lambda z, o: _small_conv(z, ks[o], fbias[o])
    w0 = conv(v, 0)
    w1_ = conv(_mul(x1, w0, tm), 1)
    return w1_, x2


def _sink_rows(sink, hkv, grp, tq):
    return jnp.repeat(sink.astype(F32).reshape(hkv, grp), tq, axis=1)[..., None]


def kernel(x, c, ctx, c_ctx, ada_w, ada_b, norm_mix_g, norm_ffn_g, router_w, exp_w_gate, exp_w_up, exp_w_down,
           a_w_in, a_w_out, a_q_g, a_k_g, a_sink, b_w_in, b_short_w, b_short_b, b_w1, b_b1, b_f1, b_w2, b_b2,
           b_f2, b_w3, b_bias, b_w_out, c_w_in, c_w_out, c_q_g, c_k_g, c_rpb):
    bsz, n, d = x.shape
    nctx = ctx.shape[1]
    depth = ada_w.shape[0]
    ne = router_w.shape[2]
    cap = EC_CAPACITY * n // ne
    cap_c = EC_CAPACITY * nctx // ne
    tm = 256

    pad_rows = (-(bsz + 1)) % 8
    rows = jnp.concatenate([c, c_ctx[None], jnp.zeros((pad_rows, d), F32)], axis=0)
    mod_all = _ada_mod(rows, ada_w, ada_b)

    h, hc = x, ctx
    for i in range(depth):
        last = i == depth - 1
        kind, j = i % N_MIXERS, i // N_MIXERS
        mod = mod_all[i, :bsz].reshape(bsz, 1, 6, d)
        sh1, sc1, g1, sh2, sc2, g2 = [mod[:, :, t] for t in range(6)]
        modc = jnp.broadcast_to(mod_all[i, bsz].reshape(1, 1, 6, d), (bsz, 1, 6, d))
        csh1, csc1, cg1, csh2, csc2, cg2 = [modc[:, :, t] for t in range(6)]
        yc = o_gate = yc_gate = None
        if kind == 0:
            hkv, grp = A_KV_HEADS, a_w_in.shape[2] // HEAD_DIM // A_KV_HEADS - 2
            qw, kvw = hkv * grp * HEAD_DIM, hkv * HEAD_DIM
            w_in = a_w_in[j].astype(BF16)
            w_out = a_w_out[j].astype(BF16)
            q_rot, q_pl, k_rot, v = _proj_attn(h, sh1, sc1, norm_mix_g[i], w_in, a_q_g[j], a_k_g[j],
                                               qw, kvw, True, tm)
            qc, kc, vc = _proj_attn(hc, csh1, csc1, norm_mix_g[i], w_in, a_q_g[j], a_k_g[j],
                                    qw, kvw, False, tm)
            o = _local_attn(q_rot, q_pl, k_rot, v, kc, vc, _window_bias(n),
                            _sink_rows(a_sink[j], hkv, grp, A_BLOCK), hkv, grp, A_BLOCK)
            if not last:
                yc = _ctx_attn(qc, kc, vc, _sink_rows(a_sink[j], hkv, grp, nctx), hkv, grp)
        elif kind == 1:
            w_in = b_w_in[j].astype(BF16)
            w_out = b_w_out[j].astype(BF16)
            hy = (b_short_w[j], b_short_b[j], b_w1[j], b_b1[j], b_f1[j], b_w2[j], b_b2[j], b_f2[j], b_w3[j],
                  b_bias[j])
            o, o_gate = _hyena_core(h, sh1, sc1, norm_mix_g[i], w_in, *hy, tm)
            if not last:
                yc, yc_gate = _hyena_core(hc, csh1, csc1, norm_mix_g[i], w_in, *hy, tm)
        else:
            nh = c_w_in.shape[2] // HEAD_DIM // 3
            hw = nh * HEAD_DIM
            w_in = c_w_in[j].astype(BF16)
            w_out = c_w_out[j].astype(BF16)
            q, k, v = _proj_attn(h, sh1, sc1, norm_mix_g[i], w_in, c_q_g[j], c_k_g[j], hw, hw, False, tm)
            qc, kc, vc = _proj_attn(hc, csh1, csc1, norm_mix_g[i], w_in, c_q_g[j], c_k_g[j], hw, hw, False, tm)
            o = _local_attn(q, q, k, v, kc, vc, _neighbourhood_bias(c_rpb[j], n), None, nh, 1, C_BLOCK)
            if not last:
                yc = _ctx_attn(qc, kc, vc, None, nh, 1)

        rwt = router_w[i].T
        u_rows = bsz * n if last else bsz * (n + nctx)
        u_init = None if last else jnp.zeros((u_rows, d), BF16)
        h1, u_all, aff = _outproj(o, w_out, h, g1, norm_ffn_g[i], sh2, sc2, rwt, 2 * tm, o_gate, u_rows=u_rows,
                                  u_buf=u_init)
        il, gl, pos_l = _select(aff, cap, n // SEL_BLOCK)
        base_l = jnp.arange(bsz) * cap
        rows_all = (il + (jnp.arange(bsz) * n)[:, None, None]).transpose(1, 0, 2).reshape(ne, bsz * cap)
        gates_all = gl.transpose(1, 0, 2).reshape(ne, bsz * cap)
        if not last:
            hc1, u_all, affc = _outproj(yc, w_out, hc, cg1, norm_ffn_g[i], csh2, csc2, rwt, 2 * tm, yc_gate,
                                        u_rows=u_rows, u_row0=bsz * n, u_buf=u_all)
            icx, gcx, pos_c = _select(affc, cap_c, n // SEL_BLOCK)
            rows_c = icx + (bsz * n + jnp.arange(bsz) * nctx)[:, None, None]
            rows_all = jnp.concatenate([rows_all, rows_c.transpose(1, 0, 2).reshape(ne, bsz * cap_c)], axis=1)
            gates_all = jnp.concatenate([gates_all, gcx.transpose(1, 0, 2).reshape(ne, bsz * cap_c)], axis=1)
        r = rows_all.shape[1]
        xs = u_all[rows_all]
        ye = _expert_ffn(xs, gates_all[..., None], exp_w_gate, exp_w_up, exp_w_down, i, r // 4, 1024)
        h = _combine(ye, jnp.where(pos_l >= 0, pos_l + base_l[:, None, None], -1), il, base_l, h1, g2, ne)
        if not last:
            base_c = bsz * cap + jnp.arange(bsz) * cap_c
            hc = _combine(ye, jnp.where(pos_c >= 0, pos_c + base_c[:, None, None], -1), icx, base_c, hc1, cg2, ne)
    return h
```

```python
import functools
import math

import numpy as np
import jax
import jax.numpy as jnp
from jax import lax
from jax.experimental import pallas as pl
from jax.experimental.pallas import tpu as pltpu

F32 = jnp.float32
BF16 = jnp.bfloat16
HI = lax.Precision.HIGHEST

GRID_W = 64
HEAD_DIM = 64
NORM_EPS = 1e-6
N_MIXERS = 3
A_KV_HEADS = 4
A_BLOCK = 128
ROPE_BASE = 10000.0
HY_ORDER = 2
HY_EMB_DIM = 33
HY_SHORT = 3
HY_FAST_DECAY = 0.3
HY_SLOW_DECAY = 1.5
HY_TARGET = 1e-2
NA_ROWS = 8
NA_COLS = 16
HEAD_GROUP = 4
C_BLOCK = 256
N_EXPERTS = 16
EC_CAPACITY = 2
NEG = -1e30

LANES = 128
VMEM_LIMIT = 56 * 1024 * 1024


def _cparams(sem):
    return pltpu.CompilerParams(dimension_semantics=sem, vmem_limit_bytes=VMEM_LIMIT)


def _sigmoid(x):
    return 1.0 / (1.0 + jnp.exp(-x))


def _modulate(h, g, shift, scale):
    ms = jnp.mean(h * h, axis=-1, keepdims=True)
    y = h * lax.rsqrt(ms + NORM_EPS)
    return (y * g) * (1.0 + scale) + shift


def _mod_body(s_ref, w_ref, b_ref, o_ref):
    s = s_ref[...]
    s = s * _sigmoid(s)
    o_ref[0] = jnp.dot(s, w_ref[0], precision=HI, preferred_element_type=F32) + b_ref[0]


def _ada_mod(rows, ada_w, ada_b):
    depth, d, n6 = ada_w.shape
    r = rows.shape[0]
    tn = 1536
    return pl.pallas_call(
        _mod_body,
        grid=(depth, n6 // tn),
        in_specs=[pl.BlockSpec((r, d), lambda l, j: (0, 0)),
                  pl.BlockSpec((1, d, tn), lambda l, j: (l, 0, j)),
                  pl.BlockSpec((1, 1, tn), lambda l, j: (l, 0, j))],
        out_specs=pl.BlockSpec((1, r, tn), lambda l, j: (l, 0, j)),
        out_shape=jax.ShapeDtypeStruct((depth, r, n6), F32),
        compiler_params=_cparams(("arbitrary", "arbitrary")),
        name="ada_mod",
    )(rows, ada_w, ada_b.reshape(depth, 1, n6))


def _dot_onehot(x, onehot_bf):
    x0 = x.astype(BF16)
    r1 = x - x0.astype(F32)
    x1 = r1.astype(BF16)
    x2 = (r1 - x1.astype(F32)).astype(BF16)
    return (jnp.dot(x0, onehot_bf, preferred_element_type=F32) + jnp.dot(x1, onehot_bf, preferred_element_type=F32)
            + jnp.dot(x2, onehot_bf, preferred_element_type=F32))


def _head_norm(x, gsum, gexp, gain):
    ss = _dot_onehot(x * x, gsum)
    r = lax.rsqrt(ss * (1.0 / HEAD_DIM) + NORM_EPS)
    rb = _dot_onehot(r, gexp)
    return (x * rb) * gain


def _rope(x, cos, sin_signed):
    rows, w = x.shape
    lane = lax.broadcasted_iota(jnp.int32, (rows, LANES), 1)
    first = (lane % 32) < 16
    outs = []
    for c in range(w // LANES):
        xc = x[:, c * LANES:(c + 1) * LANES]
        partner = jnp.where(first, pltpu.roll(xc, LANES - 16, 1), pltpu.roll(xc, 16, 1))
        outs.append(xc * cos + partner * sin_signed)
    return jnp.concatenate(outs, axis=1)


def _proj_attn_body(*refs, qw, kvw, rope):
    h_ref, sh_ref, sc_ref, g_ref, w_ref, qg_ref, kg_ref, gsq_ref, geq_ref, gsk_ref, gek_ref = refs[:11]
    u = _modulate(h_ref[0], g_ref[...], sh_ref[0], sc_ref[0])
    p = jnp.dot(u.astype(BF16), w_ref[...], preferred_element_type=F32)
    q = _head_norm(p[:, :qw], gsq_ref[...], geq_ref[...], qg_ref[...]) * (HEAD_DIM ** -0.5)
    k = _head_norm(p[:, qw:qw + kvw], gsk_ref[...], gek_ref[...], kg_ref[...])
    v = p[:, qw + kvw:]
    if rope:
        cos_ref, sin_ref, qr_out, qp_out, kr_out, v_out = refs[11:]
        cos, sin = cos_ref[...], sin_ref[...]
        qr_out[0] = _rope(q, cos, sin).astype(BF16)
        qp_out[0] = q.astype(BF16)
        kr_out[0] = _rope(k, cos, sin).astype(BF16)
        v_out[0] = v.astype(BF16)
    else:
        q_out, k_out, v_out = refs[11:]
        q_out[0] = q.astype(BF16)
        k_out[0] = k.astype(BF16)
        v_out[0] = v.astype(BF16)


def _group_mats(w):
    nh = w // HEAD_DIM
    gs = np.zeros((w, LANES), np.float32)
    gs[np.arange(w), np.arange(w) // HEAD_DIM] = 1.0
    return jnp.asarray(gs, BF16), jnp.asarray(gs.T.copy(), BF16)


def _rope_tables(n):
    t = jnp.arange(n)
    row = (t // GRID_W).astype(F32)
    col = (t % GRID_W).astype(F32)
    axis_dim = HEAD_DIM // 2
    inv_freq = 1.0 / (ROPE_BASE ** (jnp.arange(0, axis_dim, 2, dtype=F32) / axis_dim))
    ang_r = row[:, None] * inv_freq
    ang_c = col[:, None] * inv_freq
    cos64 = jnp.concatenate([jnp.cos(ang_r), jnp.cos(ang_r), jnp.cos(ang_c), jnp.cos(ang_c)], axis=-1)
    sin64 = jnp.concatenate([-jnp.sin(ang_r), jnp.sin(ang_r), -jnp.sin(ang_c), jnp.sin(ang_c)], axis=-1)
    return jnp.tile(cos64, (1, 2)), jnp.tile(sin64, (1, 2))


def _proj_attn(h, shift, scale, g, w_bf, q_g, k_g, qw, kvw, rope, tm):
    b, n, d = h.shape
    nout = w_bf.shape[1]
    tm = min(tm, n)
    gsq, geq = _group_mats(qw)
    gsk, gek = _group_mats(kvw)
    qg = jnp.tile(q_g, qw // HEAD_DIM)[None]
    kg = jnp.tile(k_g, kvw // HEAD_DIM)[None]
    full = lambda a: pl.BlockSpec(a.shape, lambda bi, i: (0,) * a.ndim)
    vec = pl.BlockSpec((1, 1, d), lambda bi, i: (bi, 0, 0))
    args = [h, shift, scale, g[None], w_bf, qg, kg, gsq, geq, gsk, gek]
    in_specs = [pl.BlockSpec((1, tm, d), lambda bi, i: (bi, i, 0)), vec, vec, full(args[3]), full(w_bf),
                full(qg), full(kg), full(gsq), full(geq), full(gsk), full(gek)]
    ospec = lambda w: pl.BlockSpec((1, tm, w), lambda bi, i: (bi, i, 0))
    oshape = lambda w: jax.ShapeDtypeStruct((b, n, w), BF16)
    if rope:
        cos, sin = _rope_tables(n)
        args += [cos, sin]
        in_specs += [pl.BlockSpec((tm, LANES), lambda bi, i: (i, 0))] * 2
        out_specs = [ospec(qw), ospec(qw), ospec(kvw), ospec(kvw)]
        out_shape = [oshape(qw), oshape(qw), oshape(kvw), oshape(kvw)]
    else:
        out_specs = [ospec(qw), ospec(kvw), ospec(kvw)]
        out_shape = [oshape(qw), oshape(kvw), oshape(kvw)]
    return pl.pallas_call(
        functools.partial(_proj_attn_body, qw=qw, kvw=kvw, rope=rope),
        grid=(b, n // tm), in_specs=in_specs, out_specs=out_specs, out_shape=out_shape,
        compiler_params=_cparams(("parallel", "parallel")),
        name="proj_attn_rope" if rope else "proj_attn",
    )(*args)


HALO = 8


def _proj_hyena_body(hc_ref, hp_ref, hn_ref, sh_ref, sc_ref, g_ref, w_ref, sw_ref, sb_ref, v_out, x1_out, x2_out,
                     *, width):
    i = pl.program_id(1)
    tm = hc_ref.shape[1]
    hall = jnp.concatenate([hp_ref[0], hc_ref[0], hn_ref[0]], axis=0)
    u = _modulate(hall, g_ref[...], sh_ref[0], sc_ref[0])
    p = jnp.dot(u.astype(BF16), w_ref[...], preferred_element_type=F32)
    cur = p[HALO:HALO + tm]
    up = jnp.where(i == 0, 0.0, p[HALO - 1:HALO])
    dn = jnp.where(i == pl.num_programs(1) - 1, 0.0, p[HALO + tm:HALO + tm + 1])
    prev = jnp.concatenate([up, cur[:tm - 1]], axis=0)
    nxt = jnp.concatenate([cur[1:], dn], axis=0)
    w = sw_ref[...]
    y = prev * w[0:1] + cur * w[1:2] + nxt * w[2:3] + sb_ref[...]
    v_out[0] = y[:, :width]
    x1_out[0] = y[:, width:2 * width]
    x2_out[0] = y[:, 2 * width:]


def _proj_hyena(h, shift, scale, g, w_bf, short_w, short_b, tm):
    b, n, d = h.shape
    w3 = w_bf.shape[1]
    width = w3 // 3
    tm = min(tm, n)
    grp, ngrp = tm // HALO, n // HALO
    vec = pl.BlockSpec((1, 1, d), lambda bi, i: (bi, 0, 0))
    ospec = pl.BlockSpec((1, tm, width), lambda bi, i: (bi, i, 0))
    oshape = jax.ShapeDtypeStruct((b, n, width), F32)
    return pl.pallas_call(
        functools.partial(_proj_hyena_body, width=width),
        grid=(b, n // tm),
        in_specs=[pl.BlockSpec((1, tm, d), lambda bi, i: (bi, i, 0)),
                  pl.BlockSpec((1, HALO, d), lambda bi, i: (bi, jnp.maximum(i * grp - 1, 0), 0)),
                  pl.BlockSpec((1, HALO, d), lambda bi, i: (bi, jnp.minimum((i + 1) * grp, ngrp - 1), 0)),
                  vec, vec,
                  pl.BlockSpec((1, d), lambda bi, i: (0, 0)),
                  pl.BlockSpec((d, w3), lambda bi, i: (0, 0)),
                  pl.BlockSpec((HY_SHORT, w3), lambda bi, i: (0, 0)),
                  pl.BlockSpec((1, w3), lambda bi, i: (0, 0))],
        out_specs=[ospec, ospec, ospec], out_shape=[oshape, oshape, oshape],
        compiler_params=_cparams(("parallel", "parallel")),
        name="proj_hyena",
    )(h, h, h, shift, scale, g[None], w_bf, short_w, short_b[None])


def _local_attn_body(*refs, hkv, grp, tq, use_sink, head_bias):
    (ql_ref, qc_ref, kp_ref, kc_ref, kn_ref, vp_ref, vc_ref, vn_ref, kx_ref, vx_ref, bias_ref) = refs[:11]
    if use_sink:
        sink_ref, o_ref = refs[11:]
    else:
        (o_ref,) = refs[11:]
    dh = HEAD_DIM
    nt = (((1,), (1,)), ((), ()))
    qcols = lambda hh: [slice((hh * grp + g) * dh, (hh * grp + g + 1) * dh) for g in range(grp)]
    for h0 in range(0, hkv, HEAD_GROUP):
        heads = range(h0, min(h0 + HEAD_GROUP, hkv))
        s, vall = [], []
        for hh in heads:
            ks = slice(hh * dh, (hh + 1) * dh)
            q_l = jnp.concatenate([ql_ref[0, :, c] for c in qcols(hh)], axis=0)
            q_c = jnp.concatenate([qc_ref[0, :, c] for c in qcols(hh)], axis=0)
            kw = jnp.concatenate([kp_ref[0, :, ks], kc_ref[0, :, ks], kn_ref[0, :, ks]], axis=0)
            s_loc = lax.dot_general(q_l, kw, nt, preferred_element_type=F32)
            s_loc = s_loc + bias_ref[0, hh if head_bias else 0]
            s_ctx = lax.dot_general(q_c, kx_ref[0, :, ks], nt, preferred_element_type=F32)
            s.append(jnp.concatenate([s_loc, s_ctx], axis=1))
            v = jnp.concatenate([vp_ref[0, :, ks], vc_ref[0, :, ks], vn_ref[0, :, ks], vx_ref[0, :, ks]], axis=0)
            vall.append(jnp.concatenate([v, jnp.ones_like(v)], axis=1))
        m = [jnp.max(x, axis=-1, keepdims=True) for x in s]
        if use_sink:
            sk = [sink_ref[hh] for hh in heads]
            m = [jnp.maximum(a, b) for a, b in zip(m, sk)]
        p = [jnp.exp(x - a).astype(BF16) for x, a in zip(s, m)]
        ox = [jnp.dot(x, v, preferred_element_type=F32) for x, v in zip(p, vall)]
        den = [x[:, dh:dh + 1] for x in ox]
        if use_sink:
            den = [d + jnp.exp(b - a) for d, a, b in zip(den, m, sk)]
        o = [x[:, :dh] * (1.0 / d) for x, d in zip(ox, den)]
        for hh, oh in zip(heads, o):
            for g, c in enumerate(qcols(hh)):
                o_ref[0, :, c] = oh[g * tq:(g + 1) * tq].astype(BF16)


def _local_attn(q_loc, q_ctx, k, v, kx, vx, bias, sink_rows, hkv, grp, tq):
    b, n, qw = q_loc.shape
    kvw = k.shape[2]
    nb = n // tq
    nctx = kx.shape[1]
    head_bias = bias.shape[1] > 1
    use_sink = sink_rows is not None
    qspec = pl.BlockSpec((1, tq, qw), lambda bi, i: (bi, i, 0))
    prev = pl.BlockSpec((1, tq, kvw), lambda bi, i: (bi, jnp.maximum(i - 1, 0), 0))
    cur = pl.BlockSpec((1, tq, kvw), lambda bi, i: (bi, i, 0))
    nxt = pl.BlockSpec((1, tq, kvw), lambda bi, i: (bi, jnp.minimum(i + 1, nb - 1), 0))
    xspec = pl.BlockSpec((1, nctx, kvw), lambda bi, i: (bi, 0, 0))
    bspec = pl.BlockSpec((1,) + bias.shape[1:],
                         lambda bi, i: (jnp.where(i == 0, 0, jnp.where(i == nb - 1, 2, 1)), 0, 0, 0))
    args = [q_loc, q_ctx, k, k, k, v, v, v, kx, vx, bias]
    in_specs = [qspec, qspec, prev, cur, nxt, prev, cur, nxt, xspec, xspec, bspec]
    if use_sink:
        args.append(sink_rows)
        in_specs.append(pl.BlockSpec(sink_rows.shape, lambda bi, i: (0, 0, 0)))
    return pl.pallas_call(
        functools.partial(_local_attn_body, hkv=hkv, grp=grp, tq=tq, use_sink=use_sink, head_bias=head_bias),
        grid=(b, nb), in_specs=in_specs,
        out_specs=pl.BlockSpec((1, tq, qw), lambda bi, i: (bi, i, 0)),
        out_shape=jax.ShapeDtypeStruct((b, n, qw), BF16),
        compiler_params=_cparams(("parallel", "parallel")),
        name="local_attn_sink" if use_sink else "local_attn",
    )(*args)


def _window_bias(n):
    tq = A_BLOCK
    grp = 4
    qi = np.arange(tq)[:, None]
    kj = np.arange(3 * tq)[None, :] - tq
    band = np.abs(qi - kj) <= A_BLOCK
    nb = n // tq
    out = []
    for which in range(3):
        blk = {0: 0, 1: min(1, nb - 1), 2: nb - 1}[which]
        pos = blk * tq + kj
        ok = band & (pos >= 0) & (pos < n)
        out.append(np.tile(np.where(ok, 0.0, NEG).astype(np.float32), (grp, 1)))
    return jnp.asarray(np.stack(out)[:, None])


def _nbr_bias_body(rpb_ref, colok_ref, o_ref, *, row_ok):
    gw = GRID_W
    col_ok = colok_ref[...] > 0.5
    neg = jnp.full((gw, gw), NEG, F32)
    toep = []
    for ro in range(2 * NA_ROWS - 1):
        x = jnp.broadcast_to(rpb_ref[0, ro:ro + 1, :], (gw, LANES))
        t = pltpu.roll(x, LANES - (NA_COLS - 1), 1, stride=1, stride_axis=0)[:, :gw]
        toep.append(jnp.where(col_ok, t, NEG))
    rq = C_BLOCK // gw
    for which in range(3):
        rows_out = []
        for qa in range(rq):
            blocks = [toep[(ka - rq) - qa + (NA_ROWS - 1)] if row_ok[which][qa][ka] else neg
                      for ka in range(3 * rq)]
            rows_out.append(jnp.concatenate(blocks, axis=1))
        o_ref[which, 0] = jnp.concatenate(rows_out, axis=0)


def _neighbourhood_bias(rpb, n):
    nh = rpb.shape[0]
    rows = n // GRID_W
    kr = min(NA_ROWS, rows)
    rq = C_BLOCK // GRID_W
    nb = n // C_BLOCK
    assert nb >= 3 and rpb.shape[1:] == (2 * NA_ROWS - 1, 2 * NA_COLS - 1)
    qc = np.arange(GRID_W)
    c0 = np.clip(qc - NA_COLS // 2, 0, GRID_W - NA_COLS)
    col_ok = ((qc[None, :] >= c0[:, None]) & (qc[None, :] < c0[:, None] + NA_COLS)).astype(np.float32)
    row_ok = []
    for which in range(3):
        blk = {0: 0, 1: 1, 2: nb - 1}[which]
        qr = blk * rq + np.arange(rq)
        kr_abs = blk * rq + np.arange(3 * rq) - rq
        r0 = np.clip(qr - kr // 2, 0, rows - kr)
        ok = (kr_abs[None, :] >= r0[:, None]) & (kr_abs[None, :] < r0[:, None] + kr)
        row_ok.append(tuple(tuple(bool(v) for v in row) for row in ok))
    rpb_pad = jnp.pad(rpb.astype(F32), ((0, 0), (0, 1), (0, LANES - rpb.shape[2])))
    return pl.pallas_call(
        functools.partial(_nbr_bias_body, row_ok=tuple(row_ok)),
        grid=(nh,),
        in_specs=[pl.BlockSpec((1, 2 * NA_ROWS, LANES), lambda hh: (hh, 0, 0)),
                  pl.BlockSpec((GRID_W, GRID_W), lambda hh: (0, 0))],
        out_specs=pl.BlockSpec((3, 1, C_BLOCK, 3 * C_BLOCK), lambda hh: (0, hh, 0, 0)),
        out_shape=jax.ShapeDtypeStruct((3, nh, C_BLOCK, 3 * C_BLOCK), F32),
        compiler_params=_cparams(("parallel",)),
        name="nbr_bias",
    )(rpb_pad, jnp.asarray(col_ok))


def _ctx_attn_body(*refs, hkv, grp, use_sink):
    q_ref, k_ref, v_ref = refs[:3]
    if use_sink:
        sink_ref, o_ref = refs[3:]
    else:
        (o_ref,) = refs[3:]
    dh = HEAD_DIM
    nq = q_ref.shape[1]
    nt = (((1,), (1,)), ((), ()))
    for hh in range(hkv):
        ks = slice(hh * dh, (hh + 1) * dh)
        qcols = [slice((hh * grp + g) * dh, (hh * grp + g + 1) * dh) for g in range(grp)]
        q = jnp.concatenate([q_ref[0, :, c] for c in qcols], axis=0)
        s = lax.dot_general(q, k_ref[0, :, ks], nt, preferred_element_type=F32)
        m = jnp.max(s, axis=-1, keepdims=True)
        if use_sink:
            sk = sink_ref[hh]
            m = jnp.maximum(m, sk)
        e = jnp.exp(s - m)
        den = jnp.sum(e, axis=-1, keepdims=True)
        if use_sink:
            den = den + jnp.exp(sk - m)
        o = jnp.dot((e * (1.0 / den)).astype(BF16), v_ref[0, :, ks], preferred_element_type=F32)
        for g in range(grp):
            o_ref[0, :, qcols[g]] = o[g * nq:(g + 1) * nq].astype(BF16)


def _ctx_attn(q, k, v, sink_rows, hkv, grp):
    b, nq, qw = q.shape
    kvw = k.shape[2]
    use_sink = sink_rows is not None
    args = [q, k, v]
    in_specs = [pl.BlockSpec((1, nq, qw), lambda bi: (bi, 0, 0)),
                pl.BlockSpec((1, nq, kvw), lambda bi: (bi, 0, 0)),
                pl.BlockSpec((1, nq, kvw), lambda bi: (bi, 0, 0))]
    if use_sink:
        args.append(sink_rows)
        in_specs.append(pl.BlockSpec(sink_rows.shape, lambda bi: (0, 0, 0)))
    return pl.pallas_call(
        functools.partial(_ctx_attn_body, hkv=hkv, grp=grp, use_sink=use_sink),
        grid=(b,), in_specs=in_specs,
        out_specs=pl.BlockSpec((1, nq, qw), lambda bi: (bi, 0, 0)),
        out_shape=jax.ShapeDtypeStruct((b, nq, qw), BF16),
        compiler_params=_cparams(("parallel",)),
        name="ctx_attn_sink" if use_sink else "ctx_attn",
    )(*args)


def _outproj_body(*refs, gated, aliased):
    h1_out, u2_out, aff_out = refs[-3:]
    ins = refs[:-4] if aliased else refs[:-3]
    if gated:
        o_ref, og_ref, w_ref, h_ref, g1_ref, gn_ref, sh2_ref, sc2_ref, rwt_ref = ins
        o = o_ref[0] * og_ref[0]
    else:
        o_ref, w_ref, h_ref, g1_ref, gn_ref, sh2_ref, sc2_ref, rwt_ref = ins
        o = o_ref[0]
    y = jnp.dot(o.astype(BF16), w_ref[...], preferred_element_type=F32)
    h1 = h_ref[0] + g1_ref[0] * y
    h1_out[0] = h1
    u2 = _modulate(h1, gn_ref[...], sh2_ref[0], sc2_ref[0])
    u2_out[...] = u2.astype(BF16)
    lt = lax.dot_general(rwt_ref[...], u2, (((1,), (1,)), ((), ())), precision=HI, preferred_element_type=F32)
    e = jnp.exp(lt - jnp.max(lt, axis=0, keepdims=True))
    aff_out[0] = e / jnp.sum(e, axis=0, keepdims=True)


def _outproj(o, w_bf, h, g1, gn, sh2, sc2, rwt, tm, o_gate=None, u_rows=None, u_row0=0, u_buf=None):
    b, n, d = h.shape
    u_rows = b * n if u_rows is None else u_rows
    kin = o.shape[2]
    ne = rwt.shape[0]
    tm = min(tm, n)
    vec = pl.BlockSpec((1, 1, d), lambda bi, i: (bi, 0, 0))
    ospec = pl.BlockSpec((1, tm, kin), lambda bi, i: (bi, i, 0))
    gated = o_gate is not None
    aliased = u_buf is not None
    nblk, blk0 = n // tm, u_row0 // tm
    n_in = 9 if gated else 8
    return pl.pallas_call(
        functools.partial(_outproj_body, gated=gated, aliased=aliased),
        grid=(b, n // tm),
        in_specs=([ospec, ospec] if gated else [ospec]) + [
                  pl.BlockSpec((kin, d), lambda bi, i: (0, 0)),
                  pl.BlockSpec((1, tm, d), lambda bi, i: (bi, i, 0)),
                  vec, pl.BlockSpec((1, d), lambda bi, i: (0, 0)), vec, vec,
                  pl.BlockSpec((ne, d), lambda bi, i: (0, 0))]
                 + ([pl.BlockSpec(memory_space=pl.ANY)] if aliased else []),
        out_specs=[pl.BlockSpec((1, tm, d), lambda bi, i: (bi, i, 0)),
                   pl.BlockSpec((tm, d), lambda bi, i: (blk0 + bi * nblk + i, 0)),
                   pl.BlockSpec((1, ne, tm), lambda bi, i: (bi, 0, i))],
        out_shape=[jax.ShapeDtypeStruct((b, n, d), F32),
                   jax.ShapeDtypeStruct((u_rows, d), BF16),
                   jax.ShapeDtypeStruct((b, ne, n), F32)],
        input_output_aliases={n_in: 1} if aliased else {},
        compiler_params=_cparams(("parallel", "parallel")),
        name="outproj_router",
    )(*((o, o_gate) if gated else (o,)), w_bf, h, g1, gn[None], sh2, sc2, rwt, *((u_buf,) if aliased else ()))


SEL_BLOCK = 128


def _bcast_lane_sum(x):
    return jnp.broadcast_to(jnp.sum(x, axis=-1, keepdims=True), x.shape)


def _block_prefix(tot):
    run = jnp.zeros(tot.shape[1:], F32)
    out = []
    for j in range(tot.shape[0]):
        out.append(run)
        run = run + tot[j]
    return jnp.stack(out)


def _select_body(aff_ref, tri_ref, trib_ref, idx_ref, gate_ref, pos_ref, inc_ref, off_ref, mk_ref, *, cap, ne):
    nb = aff_ref.shape[1] // ne
    x = aff_ref[0].reshape(nb, ne, SEL_BLOCK)
    bits = pltpu.bitcast(x, jnp.int32)
    count = lambda m: _bcast_lane_sum(jnp.sum(m.astype(F32), axis=0))

    def step(i, prefix):
        cand = prefix | jnp.left_shift(jnp.int32(1), 30 - i)
        return jnp.where(count(bits >= cand[None]) >= cap, cand, prefix)

    thr = lax.fori_loop(0, 31, step, jnp.zeros((ne, SEL_BLOCK), jnp.int32))
    gt = bits > thr[None]
    eq = bits == thr[None]
    need = cap - count(gt)

    def prefix_counts(m):
        mf = m.astype(BF16).reshape(nb * ne, SEL_BLOCK)
        inc = jnp.dot(mf, tri_ref[...], preferred_element_type=F32).reshape(nb, ne, SEL_BLOCK)
        tot = jnp.broadcast_to(inc[:, :, SEL_BLOCK - 1:], inc.shape)
        return inc, tot, _block_prefix(tot)

    e_inc, _, e_off = prefix_counts(eq)
    rank_eq = e_inc + e_off - eq.astype(F32)
    mask = gt | (eq & (rank_eq < need[None]))
    inc, tot, off = prefix_counts(mask)
    inc_ref[...] = inc.reshape(nb * ne, SEL_BLOCK)
    off_ref[...] = off.reshape(nb * ne, SEL_BLOCK)
    mk_ref[...] = mask.astype(F32).reshape(nb * ne, SEL_BLOCK)
    pos_ref[0] = jnp.where(mask, inc + off - 1.0, -1.0).astype(jnp.int32).reshape(nb * ne, SEL_BLOCK)

    slot = lax.broadcasted_iota(jnp.int32, (cap, 1), 0).astype(F32)
    lane_nb = lax.broadcasted_iota(jnp.int32, (cap, nb), 1).astype(F32)
    lane = lax.broadcasted_iota(jnp.int32, (cap, SEL_BLOCK), 1).astype(F32)
    ones = jnp.ones((8, SEL_BLOCK), BF16)
    for e in range(ne):
        rows = pl.ds(e, nb, stride=ne)
        mk = mk_ref[rows, :].astype(BF16)
        cnt_row = lax.dot_general(ones, mk, (((1,), (1,)), ((), ())), preferred_element_type=F32)
        pinc_row = jnp.dot(cnt_row.astype(BF16), trib_ref[...], preferred_element_type=F32)[0:1]
        blk = jnp.sum((pinc_row <= slot).astype(F32), axis=-1, keepdims=True)
        onehot = (lane_nb == blk).astype(BF16)
        rhs = jnp.concatenate([inc_ref[rows, :], off_ref[rows, :], aff_ref[0, rows, :]], axis=1)
        r0 = rhs.astype(BF16)
        d1 = rhs - r0.astype(F32)
        r1 = d1.astype(BF16)
        r2 = (d1 - r1.astype(F32)).astype(BF16)
        got = (jnp.dot(onehot, r0, preferred_element_type=F32) + jnp.dot(onehot, r1, preferred_element_type=F32)
               + jnp.dot(onehot, r2, preferred_element_type=F32))
        rank = slot - got[:, SEL_BLOCK:SEL_BLOCK + 1]
        local = jnp.sum((got[:, :SEL_BLOCK] <= rank).astype(F32), axis=-1, keepdims=True)
        gate = jnp.sum(jnp.where(lane == local, got[:, 2 * SEL_BLOCK:], 0.0), axis=-1, keepdims=True)
        idx_ref[0, :, e:e + 1] = (blk * SEL_BLOCK + local).astype(jnp.int32)
        gate_ref[0, :, e:e + 1] = gate


def _select(aff, cap, nb_pad):
    bsz, ne, n = aff.shape
    pad = nb_pad * SEL_BLOCK - n
    a = jnp.pad(aff, ((0, 0), (0, 0), (0, pad))).reshape(bsz, ne, nb_pad, SEL_BLOCK).transpose(0, 2, 1, 3)
    a = a.reshape(bsz, nb_pad * ne, SEL_BLOCK)
    tri = jnp.asarray(np.triu(np.ones((SEL_BLOCK, SEL_BLOCK), np.float32)), BF16)
    trib = jnp.asarray(np.triu(np.ones((nb_pad, nb_pad), np.float32)), BF16)
    idx, gates, pos = pl.pallas_call(
        functools.partial(_select_body, cap=cap, ne=ne),
        grid=(bsz,),
        in_specs=[pl.BlockSpec((1, nb_pad * ne, SEL_BLOCK), lambda b: (b, 0, 0)),
                  pl.BlockSpec(tri.shape, lambda b: (0, 0)), pl.BlockSpec(trib.shape, lambda b: (0, 0))],
        out_specs=[pl.BlockSpec((1, cap, ne), lambda b: (b, 0, 0))] * 2
                  + [pl.BlockSpec((1, nb_pad * ne, SEL_BLOCK), lambda b: (b, 0, 0))],
        out_shape=[jax.ShapeDtypeStruct((bsz, cap, ne), jnp.int32), jax.ShapeDtypeStruct((bsz, cap, ne), F32),
                   jax.ShapeDtypeStruct((bsz, nb_pad * ne, SEL_BLOCK), jnp.int32)],
        scratch_shapes=[pltpu.VMEM((nb_pad * ne, SEL_BLOCK), F32)] * 3,
        compiler_params=_cparams(("parallel",)),
        name="expert_select",
    )(a, tri, trib)
    pos = pos.reshape(bsz, nb_pad, ne, SEL_BLOCK).transpose(0, 1, 3, 2).reshape(bsz, nb_pad * SEL_BLOCK, ne)
    return idx.transpose(0, 2, 1), gates.transpose(0, 2, 1), pos[:, :n]


def _expert_body(xs_ref, wg_ref, wu_ref, wd_ref, gate_ref, o_ref, acc_ref):
    f = pl.program_id(2)
    xs = xs_ref[0]
    a = jnp.dot(xs, wg_ref[...].astype(BF16), preferred_element_type=F32)
    bb = jnp.dot(xs, wu_ref[...].astype(BF16), preferred_element_type=F32)
    hid = (a * _sigmoid(a)) * bb
    y = jnp.dot(hid.astype(BF16), wd_ref[...].astype(BF16), preferred_element_type=F32)

    @pl.when(f == 0)
    def _():
        acc_ref[...] = y

    @pl.when(f > 0)
    def _():
        acc_ref[...] += y

    @pl.when(f == pl.num_programs(2) - 1)
    def _():
        o_ref[0] = (acc_ref[...] * gate_ref[0]).astype(o_ref.dtype)


def _expert_ffn(xs, gates, w_gate, w_up, w_down, layer, tr, tf):
    ne, r, d = xs.shape
    ff = w_gate.shape[3]
    return pl.pallas_call(
        _expert_body,
        grid=(ne, r // tr, ff // tf),
        in_specs=[pl.BlockSpec((1, tr, d), lambda e, c, f: (e, c, 0)),
                  pl.BlockSpec((None, None, d, tf), lambda e, c, f: (layer, e, 0, f)),
                  pl.BlockSpec((None, None, d, tf), lambda e, c, f: (layer, e, 0, f)),
                  pl.BlockSpec((None, None, tf, d), lambda e, c, f: (layer, e, f, 0)),
                  pl.BlockSpec((1, tr, 1), lambda e, c, f: (e, c, 0))],
        out_specs=pl.BlockSpec((1, tr, d), lambda e, c, f: (e, c, 0)),
        out_shape=jax.ShapeDtypeStruct((ne, r, d), BF16),
        scratch_shapes=[pltpu.VMEM((tr, d), F32)],
        compiler_params=_cparams(("parallel", "parallel", "arbitrary")),
        name="expert_ffn",
    )(xs, w_gate, w_up, w_down, gates)


COMBINE_TILE = 256
COMBINE_WIN = 128
BF16_ROWS = 16


def _combine_body(ws_ref, cnt_ref, ye_hbm, pos_ref, h_ref, g_ref, o_ref, buf, xbuf, acc_ref, sem, xsem,
                  *, ne, nt, rows_total):
    win = COMBINE_WIN
    step = pl.program_id(0) * nt + pl.program_id(1)
    nsteps = pl.num_programs(0) * nt
    slot = step % 2

    def clamp(st):
        return pl.multiple_of(jnp.minimum(st, rows_total - win), BF16_ROWS)

    def first_start(s_idx, e):
        return clamp((ws_ref[s_idx * ne + e] // BF16_ROWS) * BF16_ROWS)

    def window_copy(s_idx, sl, e):
        return pltpu.make_async_copy(ye_hbm.at[e, pl.ds(first_start(s_idx, e), win), :], buf.at[sl, e],
                                     sem.at[sl, e])

    @pl.when(step == 0)
    def _():
        for e in range(ne):
            window_copy(0, 0, e).start()

    @pl.when(step + 1 < nsteps)
    def _():
        for e in range(ne):
            window_copy(step + 1, 1 - slot, e).start()

    tm = pos_ref.shape[1]
    d = h_ref.shape[2]
    lane = lax.broadcasted_iota(jnp.int32, (tm, win), 1)
    pos = [pos_ref[0, :, e:e + 1] for e in range(ne)]
    st0 = [first_start(step, e) for e in range(ne)]
    onehot = jnp.concatenate([((pos[e] - st0[e]) == lane).astype(BF16) for e in range(ne)], axis=1)
    for e in range(ne):
        window_copy(step, slot, e).wait()
    acc_ref[...] = jnp.dot(onehot, buf[slot].reshape(ne * win, d), preferred_element_type=F32)
    nwin = [(ws_ref[step * ne + e] + cnt_ref[step * ne + e] - st0[e] + win - 1) // win for e in range(ne)]
    extra = nwin[0]
    for e in range(1, ne):
        extra = jnp.maximum(extra, nwin[e])

    @pl.when(extra > 1)
    def _():
        for e in range(ne):
            def more(k, carry, e=e):
                lo = st0[e] + k * win
                st = clamp(lo)
                cp = pltpu.make_async_copy(ye_hbm.at[e, pl.ds(st, win), :], xbuf, xsem)
                cp.start()
                cp.wait()
                oh = (((pos[e] - st) == lane) & (pos[e] >= lo)).astype(BF16)
                acc_ref[...] += jnp.dot(oh, xbuf[...], preferred_element_type=F32)
                return carry

            lax.fori_loop(1, nwin[e], more, 0)

    o_ref[0] = h_ref[0] + g_ref[0] * acc_ref[...]


def _combine(ye, pos_global, idx, row_base, h, g, ne):
    bsz, n, d = h.shape
    tm = min(COMBINE_TILE, n)
    nt = n // tm
    rows_total = ye.shape[1]
    below = jnp.sum(idx[..., None] < jnp.arange(nt + 1) * tm, axis=2).astype(jnp.int32)
    ws = (row_base[:, None, None] + below[..., :-1]).transpose(0, 2, 1)
    cnt = (below[..., 1:] - below[..., :-1]).transpose(0, 2, 1)
    grid_spec = pltpu.PrefetchScalarGridSpec(
        num_scalar_prefetch=2,
        grid=(bsz, nt),
        in_specs=[pl.BlockSpec(memory_space=pl.ANY),
                  pl.BlockSpec((1, tm, ne), lambda b, t, ws, cnt: (b, t, 0)),
                  pl.BlockSpec((1, tm, d), lambda b, t, ws, cnt: (b, t, 0)),
                  pl.BlockSpec((1, 1, d), lambda b, t, ws, cnt: (b, 0, 0))],
        out_specs=pl.BlockSpec((1, tm, d), lambda b, t, ws, cnt: (b, t, 0)),
        scratch_shapes=[pltpu.VMEM((2, ne, COMBINE_WIN, d), BF16), pltpu.VMEM((COMBINE_WIN, d), BF16),
                        pltpu.VMEM((tm, d), F32), pltpu.SemaphoreType.DMA((2, ne)), pltpu.SemaphoreType.DMA(())])
    return pl.pallas_call(
        functools.partial(_combine_body, ne=ne, nt=nt, rows_total=rows_total),
        grid_spec=grid_spec,
        out_shape=jax.ShapeDtypeStruct((bsz, n, d), F32),
        compiler_params=_cparams(("arbitrary", "arbitrary")),
        name="moe_combine",
    )(ws.reshape(-1), cnt.reshape(-1), ye, pos_global, h, g)


FFT_SLAB = 128
FFT_UNROLL = 8


def _stack_complex(m):
    return np.block([[m.real, -m.imag], [m.imag, m.real]])


def _dft_consts(n_total, n_nonzero):
    n1 = n_total // FFT_SLAB
    na = n_nonzero // FFT_SLAB
    idx1 = np.arange(n1)
    f1 = np.exp(-2j * np.pi * np.outer(idx1, idx1) / n1)
    idx2 = np.arange(FFT_SLAB)
    f2 = np.exp(-2j * np.pi * np.outer(idx2, idx2) / FFT_SLAB)
    tw = np.exp(-2j * np.pi * np.outer(idx1, idx2) / n_total)
    return dict(
        m1=_stack_complex(f1[:, :na]),
        m1_real=np.concatenate([f1.real, f1.imag], axis=0),
        m2=_stack_complex(f2),
        m2i=_stack_complex(np.conj(f2).T / n_total),
        m1i=_stack_complex(np.conj(f1).T[:na, :]),
        twr=tw.real, twi=tw.imag)


def _lane_replicated(row_ref, c, ct):
    t = jnp.broadcast_to(row_ref[pl.ds(c, 1), :], (FFT_SLAB, FFT_SLAB)).T
    return t if ct == FFT_SLAB else jnp.tile(t, (1, ct // FFT_SLAB))


def _dot_split(mh_ref, ml_ref, x):
    xh = x.astype(BF16)
    xl = (x - xh.astype(F32)).astype(BF16)
    mh = mh_ref[...]
    return (jnp.dot(mh, xh, preferred_element_type=F32) + jnp.dot(mh, xl, preferred_element_type=F32)
            + jnp.dot(ml_ref[...], xh, preferred_element_type=F32))


def _spectrum_body(k_ref, m1h_ref, m1l_ref, m2h_ref, m2l_ref, twr_ref, twi_ref, or_ref, oi_ref, *, n1):
    ct = k_ref.shape[1]

    def first(b, carry):
        a = _dot_split(m1h_ref, m1l_ref, k_ref[pl.ds(b, n1, stride=FFT_SLAB), :])
        or_ref[pl.ds(b, n1, stride=FFT_SLAB), :] = a[:n1]
        oi_ref[pl.ds(b, n1, stride=FFT_SLAB), :] = a[n1:]
        return carry

    lax.fori_loop(0, FFT_SLAB, first, 0, unroll=FFT_UNROLL)

    def slab(c, carry):
        r0 = pl.multiple_of(c * FFT_SLAB, FFT_SLAB)
        ar, ai = or_ref[pl.ds(r0, FFT_SLAB), :], oi_ref[pl.ds(r0, FFT_SLAB), :]
        tr, ti = _lane_replicated(twr_ref, c, ct), _lane_replicated(twi_ref, c, ct)
        p = jnp.concatenate([ar * tr - ai * ti, ar * ti + ai * tr], axis=0)
        x = _dot_split(m2h_ref, m2l_ref, p)
        or_ref[pl.ds(r0, FFT_SLAB), :] = x[:FFT_SLAB]
        oi_ref[pl.ds(r0, FFT_SLAB), :] = x[FFT_SLAB:]
        return carry

    lax.fori_loop(0, n1, slab, 0, unroll=FFT_UNROLL)


def _split_bf16(m):
    hi = jnp.asarray(m, F32).astype(BF16)
    lo = (jnp.asarray(m, F32) - hi.astype(F32)).astype(BF16)
    return hi, lo


def _filter_spectrum(k, ct=128):
    n_total, ch = k.shape
    n1 = n_total // FFT_SLAB
    cs = _dft_consts(n_total, n_total)
    consts = [*_split_bf16(cs["m1_real"]), *_split_bf16(cs["m2"]),
              jnp.asarray(cs["twr"], F32), jnp.asarray(cs["twi"], F32)]
    full = lambda a: pl.BlockSpec(a.shape, lambda j: (0,) * a.ndim)
    col = pl.BlockSpec((n_total, ct), lambda j: (0, j))
    return pl.pallas_call(
        functools.partial(_spectrum_body, n1=n1),
        grid=(ch // ct,),
        in_specs=[col] + [full(a) for a in consts],
        out_specs=[col, col],
        out_shape=[jax.ShapeDtypeStruct((n_total, ch), F32)] * 2,
        compiler_params=_cparams(("parallel",)),
        name="filter_spectrum",
    )(k, *consts)


def _fftconv_body(z_ref, kr_ref, ki_ref, bias_ref, m1_ref, m2_ref, m2i_ref, m1i_ref, twr_ref, twi_ref,
                  o_ref, wr_ref, wi_ref, *, n1, na):
    ct = z_ref.shape[2]

    def first(b, carry):
        rows = pl.ds(b, na, stride=FFT_SLAB)
        s = jnp.concatenate([z_ref[0, rows, :], z_ref[1, rows, :]], axis=0).astype(BF16)
        a = jnp.dot(m1_ref[...], s, preferred_element_type=F32)
        wr_ref[pl.ds(b, n1, stride=FFT_SLAB), :] = a[:n1]
        wi_ref[pl.ds(b, n1, stride=FFT_SLAB), :] = a[n1:]
        return carry

    lax.fori_loop(0, FFT_SLAB, first, 0, unroll=FFT_UNROLL)

    def slab(c, carry):
        r0 = pl.multiple_of(c * FFT_SLAB, FFT_SLAB)
        rows = pl.ds(r0, FFT_SLAB)
        ar, ai = wr_ref[rows, :], wi_ref[rows, :]
        tr, ti = _lane_replicated(twr_ref, c, ct), _lane_replicated(twi_ref, c, ct)
        p = jnp.concatenate([ar * tr - ai * ti, ar * ti + ai * tr], axis=0).astype(BF16)
        x = jnp.dot(m2_ref[...], p, preferred_element_type=F32)
        xr, xi = x[:FFT_SLAB], x[FFT_SLAB:]
        kr, ki = kr_ref[rows, :], ki_ref[rows, :]
        y = jnp.concatenate([xr * kr - xi * ki, xr * ki + xi * kr], axis=0).astype(BF16)
        bb = jnp.dot(m2i_ref[...], y, preferred_element_type=F32)
        br, bi = bb[:FFT_SLAB], bb[FFT_SLAB:]
        wr_ref[rows, :] = br * tr + bi * ti
        wi_ref[rows, :] = bi * tr - br * ti
        return carry

    lax.fori_loop(0, n1, slab, 0, unroll=FFT_UNROLL)

    def last(b, carry):
        rows = pl.ds(b, n1, stride=FFT_SLAB)
        s = jnp.concatenate([wr_ref[rows, :], wi_ref[rows, :]], axis=0).astype(BF16)
        y = jnp.dot(m1i_ref[...], s, preferred_element_type=F32)
        orow = pl.ds(b, na, stride=FFT_SLAB)
        bias = bias_ref[...]
        o_ref[0, orow, :] = y[:na] + z_ref[0, orow, :] * bias
        o_ref[1, orow, :] = y[na:] + z_ref[1, orow, :] * bias
        return carry

    lax.fori_loop(0, FFT_SLAB, last, 0, unroll=FFT_UNROLL)


def _fftconv(z, kf_re, kf_im, bias, ct=128):
    bsz, n, ch = z.shape
    n_total = 2 * n
    n1, na = n_total // FFT_SLAB, n // FFT_SLAB
    cs = _dft_consts(n_total, n)
    mats = [jnp.asarray(cs[name], BF16) for name in ("m1", "m2", "m2i", "m1i")]
    tws = [jnp.asarray(cs[name], F32) for name in ("twr", "twi")]
    full = lambda a: pl.BlockSpec(a.shape, lambda j, p: (0,) * a.ndim)
    single = pl.Buffered(1)
    zspec = pl.BlockSpec((2, n, ct), lambda j, p: (p, 0, j), pipeline_mode=single)
    kspec = pl.BlockSpec((n_total, ct), lambda j, p: (0, j), pipeline_mode=single)
    return pl.pallas_call(
        functools.partial(_fftconv_body, n1=n1, na=na),
        grid=(ch // ct, bsz // 2),
        in_specs=[zspec, kspec, kspec, pl.BlockSpec((1, ct), lambda j, p: (0, j))]
                 + [full(a) for a in mats] + [full(a) for a in tws],
        out_specs=pl.BlockSpec((2, n, ct), lambda j, p: (p, 0, j), pipeline_mode=single),
        out_shape=jax.ShapeDtypeStruct((bsz, n, ch), F32),
        scratch_shapes=[pltpu.VMEM((n_total, ct), F32), pltpu.VMEM((n_total, ct), F32)],
        compiler_params=_cparams(("parallel", "parallel")),
        name="fftconv",
    )(z, kf_re, kf_im, bias[None].astype(F32), *mats, *tws)


def _filter_hidden_body(z_ref, w1_ref, b1_ref, f1_ref, w2_ref, b2_ref, f2_ref, o_ref):
    z = z_ref[0]
    fd = w1_ref.shape[1]
    h = jnp.sin(f1_ref[...] * (jnp.dot(z, w1_ref[...], precision=HI, preferred_element_type=F32) + b1_ref[...]))
    h = jnp.sin(f2_ref[...] * (jnp.dot(h, w2_ref[...], precision=HI, preferred_element_type=F32) + b2_ref[...]))
    hext = jnp.concatenate([h, z[:, :LANES - fd]], axis=1)
    row = lax.broadcasted_iota(jnp.int32, hext.shape, 0)
    dead = (pl.program_id(0) == 1) & (row == 0)
    o_ref[0] = jnp.where(dead, 0.0, hext)


def _filter_taps_body(hf_ref, hb_ref, wf_ref, wb_ref, dl_ref, o_ref):
    n = hf_ref.shape[1]
    dot = lambda a, b: jnp.dot(a, b, precision=HI, preferred_element_type=F32)
    hf, hb = hf_ref[0], hb_ref[0]
    fd = LANES // 2
    fwd = dot(hf, wf_ref[...]) * jnp.exp(-(hf[:, fd:fd + 1] * dl_ref[...]))
    bwd = dot(hb, wb_ref[...]) * jnp.exp(-(hb[:, fd:fd + 1] * dl_ref[...]))
    ss = jnp.sum(fwd * fwd, axis=0, keepdims=True) + jnp.sum(bwd * bwd, axis=0, keepdims=True)
    sc = lax.rsqrt(ss + NORM_EPS)
    o_ref[0:n, :] = fwd * sc
    o_ref[n:, :] = bwd * sc


def _hyena_kernels(n, w1, b1, f1, w2, b2, f2, w3, width, ct=128):
    emb, fd = w1.shape
    assert fd == LANES // 2 and emb <= LANES
    t = jnp.linspace(0.0, 1.0, n, dtype=F32)[:, None]
    bands = (HY_EMB_DIM - 1) // 2
    w = 2.0 * math.pi * jnp.arange(n, dtype=F32)[:, None] / n
    f = jnp.linspace(1e-4, bands - 1, bands, dtype=F32)[None, :]
    z = jnp.concatenate([t, jnp.cos(f * w), -jnp.sin(f * w)], axis=-1)
    z_rev = jnp.concatenate([z[:1], z[1:][::-1]], axis=0)
    zz = jnp.pad(jnp.stack([z, z_rev]), ((0, 0), (0, 0), (0, LANES - emb)))
    w1p = jnp.pad(w1.astype(F32), ((0, LANES - emb), (0, 0)))
    full = lambda a: pl.BlockSpec(a.shape, lambda i: (0,) * a.ndim)
    row = lambda a: a.astype(F32)[None]
    small = [w1p, row(b1), row(f1), w2.astype(F32), row(b2), row(f2)]
    hx = pl.pallas_call(
        _filter_hidden_body,
        grid=(2,),
        in_specs=[pl.BlockSpec((1, n, LANES), lambda i: (i, 0, 0))] + [full(a) for a in small],
        out_specs=pl.BlockSpec((1, n, LANES), lambda i: (i, 0, 0)),
        out_shape=jax.ShapeDtypeStruct((2, n, LANES), F32),
        compiler_params=_cparams(("parallel",)),
        name="filter_hidden",
    )(zz, *small)
    max_decay = math.log(HY_TARGET) / HY_FAST_DECAY
    min_decay = math.log(HY_TARGET) / HY_SLOW_DECAY
    dl = jnp.abs(jnp.linspace(min_decay, max_decay, width, dtype=F32))[None, :]
    w3r = w3.astype(F32).reshape(fd, HY_ORDER, 2, width).transpose(1, 2, 0, 3)
    w3p = jnp.pad(w3r, ((0, 0), (0, 0), (0, LANES - fd), (0, 0)))
    single = pl.Buffered(1)
    ks = pl.pallas_call(
        _filter_taps_body,
        grid=(HY_ORDER, width // ct),
        in_specs=[pl.BlockSpec((1, n, LANES), lambda o, j: (0, 0, 0), pipeline_mode=single),
                  pl.BlockSpec((1, n, LANES), lambda o, j: (1, 0, 0), pipeline_mode=single),
                  pl.BlockSpec((None, None, LANES, ct), lambda o, j: (o, 0, 0, j)),
                  pl.BlockSpec((None, None, LANES, ct), lambda o, j: (o, 1, 0, j)),
                  pl.BlockSpec((1, ct), lambda o, j: (0, j))],
        out_specs=pl.BlockSpec((None, 2 * n, ct), lambda o, j: (o, 0, j)),
        out_shape=jax.ShapeDtypeStruct((HY_ORDER, 2 * n, width), F32),
        compiler_params=_cparams(("parallel", "parallel")),
        name="filter_taps",
    )(hx, hx, w3p, w3p, dl)
    return [ks[o] for o in range(HY_ORDER)]


def _small_conv_body(z_ref, k_ref, bias_ref, mfk_ref, mf_ref, mi_ref, o_ref):
    bsz, n, _ = z_ref.shape
    nt = 2 * n
    kf = jnp.dot(mfk_ref[...], k_ref[...], precision=HI, preferred_element_type=F32)
    kr, ki = kf[:nt], kf[nt:]
    bias = bias_ref[...]
    for p in range(bsz // 2):
        z0, z1 = z_ref[2 * p], z_ref[2 * p + 1]
        x = jnp.dot(mf_ref[...], jnp.concatenate([z0, z1], axis=0).astype(BF16), preferred_element_type=F32)
        xr, xi = x[:nt], x[nt:]
        y = jnp.concatenate([xr * kr - xi * ki, xr * ki + xi * kr], axis=0).astype(BF16)
        w = jnp.dot(mi_ref[...], y, preferred_element_type=F32)
        o_ref[2 * p] = w[:n] + z0 * bias
        o_ref[2 * p + 1] = w[n:] + z1 * bias


def _small_conv(z, k, bias, ct=256):
    bsz, n, ch = z.shape
    nt = 2 * n
    idx = np.arange(nt)
    f = np.exp(-2j * np.pi * np.outer(idx, idx) / nt)
    mfk = jnp.asarray(np.concatenate([f.real, f.imag], axis=0), F32)
    mf = jnp.asarray(_stack_complex(f[:, :n]), BF16)
    mi = jnp.asarray(_stack_complex(np.conj(f).T[:n, :] / nt), BF16)
    full = lambda a: pl.BlockSpec(a.shape, lambda j: (0,) * a.ndim)
    return pl.pallas_call(
        _small_conv_body,
        grid=(ch // ct,),
        in_specs=[pl.BlockSpec((bsz, n, ct), lambda j: (0, 0, j)), pl.BlockSpec((nt, ct), lambda j: (0, j)),
                  pl.BlockSpec((1, ct), lambda j: (0, j)), full(mfk), full(mf), full(mi)],
        out_specs=pl.BlockSpec((bsz, n, ct), lambda j: (0, 0, j)),
        out_shape=jax.ShapeDtypeStruct((bsz, n, ch), F32),
        compiler_params=_cparams(("parallel",)),
        name="small_conv",
    )(z, k, bias[None].astype(F32), mfk, mf, mi)


def _mul_body(a_ref, b_ref, o_ref):
    o_ref[...] = a_ref[...] * b_ref[...]


def _mul(a, b, tm):
    bsz, n, d = a.shape
    tm = min(tm, n)
    spec = pl.BlockSpec((1, tm, d), lambda bi, i: (bi, i, 0))
    return pl.pallas_call(
        _mul_body, grid=(bsz, n // tm), in_specs=[spec, spec], out_specs=spec,
        out_shape=jax.ShapeDtypeStruct(a.shape, a.dtype),
        compiler_params=_cparams(("parallel", "parallel")), name="gate_mul",
    )(a, b)


def _hyena_core(h, shift, scale, g, w_in_bf, short_w, short_b, w1, b1, f1, w2, b2, f2, w3, fbias, tm):
    n = h.shape[1]
    width = w_in_bf.shape[1] // 3
    v, x1, x2 = _proj_hyena(h, shift, scale, g, w_in_bf, short_w, short_b, tm)
    ks = _hyena_kernels(n, w1, b1, f1, w2, b2, f2, w3, width)
    if 2 * n // FFT_SLAB >= FFT_SLAB:
        conv = lambda z, o: _fftconv(z, *_filter_spectrum(ks[o]), fbias[o])
    else:
        conv = lambda z, o: _small_conv(z, ks[o], fbias[o])
    w0 = conv(v, 0)
    w1_ = conv(_mul(x1, w0, tm), 1)
    return w1_, x2


def _sink_rows(sink, hkv, grp, tq):
    return jnp.repeat(sink.astype(F32).reshape(hkv, grp), tq, axis=1)[..., None]


def kernel(x, c, ctx, c_ctx, ada_w, ada_b, norm_mix_g, norm_ffn_g, router_w, exp_w_gate, exp_w_up, exp_w_down,
           a_w_in, a_w_out, a_q_g, a_k_g, a_sink, b_w_in, b_short_w, b_short_b, b_w1, b_b1, b_f1, b_w2, b_b2,
           b_f2, b_w3, b_bias, b_w_out, c_w_in, c_w_out, c_q_g, c_k_g, c_rpb):
    bsz, n, d = x.shape
    nctx = ctx.shape[1]
    depth = ada_w.shape[0]
    ne = router_w.shape[2]
    cap = EC_CAPACITY * n // ne
    cap_c = EC_CAPACITY * nctx // ne
    tm = 256

    pad_rows = (-(bsz + 1)) % 8
    rows = jnp.concatenate([c, c_ctx[None], jnp.zeros((pad_rows, d), F32)], axis=0)
    mod_all = _ada_mod(rows, ada_w, ada_b)

    h, hc = x, ctx
    for i in range(depth):
        last = i == depth - 1
        kind, j = i % N_MIXERS, i // N_MIXERS
        mod = mod_all[i, :bsz].reshape(bsz, 1, 6, d)
        sh1, sc1, g1, sh2, sc2, g2 = [mod[:, :, t] for t in range(6)]
        modc = jnp.broadcast_to(mod_all[i, bsz].reshape(1, 1, 6, d), (bsz, 1, 6, d))
        csh1, csc1, cg1, csh2, csc2, cg2 = [modc[:, :, t] for t in range(6)]
        yc = o_gate = yc_gate = None
        if kind == 0:
            hkv, grp = A_KV_HEADS, a_w_in.shape[2] // HEAD_DIM // A_KV_HEADS - 2
            qw, kvw = hkv * grp * HEAD_DIM, hkv * HEAD_DIM
            w_in = a_w_in[j].astype(BF16)
            w_out = a_w_out[j].astype(BF16)
            q_rot, q_pl, k_rot, v = _proj_attn(h, sh1, sc1, norm_mix_g[i], w_in, a_q_g[j], a_k_g[j],
                                               qw, kvw, True, tm)
            qc, kc, vc = _proj_attn(hc, csh1, csc1, norm_mix_g[i], w_in, a_q_g[j], a_k_g[j],
                                    qw, kvw, False, tm)
            o = _local_attn(q_rot, q_pl, k_rot, v, kc, vc, _window_bias(n),
                            _sink_rows(a_sink[j], hkv, grp, A_BLOCK), hkv, grp, A_BLOCK)
            if not last:
                yc = _ctx_attn(qc, kc, vc, _sink_rows(a_sink[j], hkv, grp, nctx), hkv, grp)
        elif kind == 1:
            w_in = b_w_in[j].astype(BF16)
            w_out = b_w_out[j].astype(BF16)
            hy = (b_short_w[j], b_short_b[j], b_w1[j], b_b1[j], b_f1[j], b_w2[j], b_b2[j], b_f2[j], b_w3[j],
                  b_bias[j])
            o, o_gate = _hyena_core(h, sh1, sc1, norm_mix_g[i], w_in, *hy, tm)
            if not last:
                yc, yc_gate = _hyena_core(hc, csh1, csc1, norm_mix_g[i], w_in, *hy, tm)
        else:
            nh = c_w_in.shape[2] // HEAD_DIM // 3
            hw = nh * HEAD_DIM
            w_in = c_w_in[j].astype(BF16)
            w_out = c_w_out[j].astype(BF16)
            q, k, v = _proj_attn(h, sh1, sc1, norm_mix_g[i], w_in, c_q_g[j], c_k_g[j], hw, hw, False, tm)
            qc, kc, vc = _proj_attn(hc, csh1, csc1, norm_mix_g[i], w_in, c_q_g[j], c_k_g[j], hw, hw, False, tm)
            o = _local_attn(q, q, k, v, kc, vc, _neighbourhood_bias(c_rpb[j], n), None, nh, 1, C_BLOCK)
            if not last:
                yc = _ctx_attn(qc, kc, vc, None, nh, 1)

        rwt = router_w[i].T
        u_rows = bsz * n if last else bsz * (n + nctx)
        u_init = None if last else jnp.zeros((u_rows, d), BF16)
        h1, u_all, aff = _outproj(o, w_out, h, g1, norm_ffn_g[i], sh2, sc2, rwt, 2 * tm, o_gate, u_rows=u_rows,
                                  u_buf=u_init)
        il, gl, pos_l = _select(aff, cap, n // SEL_BLOCK)
        base_l = jnp.arange(bsz) * cap
        rows_all = (il + (jnp.arange(bsz) * n)[:, None, None]).transpose(1, 0, 2).reshape(ne, bsz * cap)
        gates_all = gl.transpose(1, 0, 2).reshape(ne, bsz * cap)
        if not last:
            hc1, u_all, affc = _outproj(yc, w_out, hc, cg1, norm_ffn_g[i], csh2, csc2, rwt, 2 * tm, yc_gate,
                                        u_rows=u_rows, u_row0=bsz * n, u_buf=u_all)
            icx, gcx, pos_c = _select(affc, cap_c, n // SEL_BLOCK)
            rows_c = icx + (bsz * n + jnp.arange(bsz) * nctx)[:, None, None]
            rows_all = jnp.concatenate([rows_all, rows_c.transpose(1, 0, 2).reshape(ne, bsz * cap_c)], axis=1)
            gates_all = jnp.concatenate([gates_all, gcx.transpose(1, 0, 2).reshape(ne, bsz * cap_c)], axis=1)
        r = rows_all.shape[1]
        xs = u_all[rows_all]
        ye = _expert_ffn(xs, gates_all[..., None], exp_w_gate, exp_w_up, exp_w_down, i, r // 4, 1024)
        h = _combine(ye, jnp.where(pos_l >= 0, pos_l + base_l[:, None, None], -1), il, base_l, h1, g2, ne)
        if not last:
            base_c = bsz * cap + jnp.arange(bsz) * cap_c
            hc = _combine(ye, jnp.where(pos_c >= 0, pos_c + base_c[:, None, None], -1), icx, base_c, hc1, cg2, ne)
    return h
```

```python
import functools
import math

import numpy as np
import jax
import jax.numpy as jnp
from jax import lax
from jax.experimental import pallas as pl
from jax.experimental.pallas import tpu as pltpu

F32 = jnp.float32
BF16 = jnp.bfloat16
HI = lax.Precision.HIGHEST

GRID_W = 64
HEAD_DIM = 64
NORM_EPS = 1e-6
N_MIXERS = 3
A_KV_HEADS = 4
A_BLOCK = 128
ROPE_BASE = 10000.0
HY_ORDER = 2
HY_EMB_DIM = 33
HY_SHORT = 3
HY_FAST_DECAY = 0.3
HY_SLOW_DECAY = 1.5
HY_TARGET = 1e-2
NA_ROWS = 8
NA_COLS = 16
HEAD_GROUP = 4
C_BLOCK = 256
N_EXPERTS = 16
EC_CAPACITY = 2
NEG = -1e30

LANES = 128
VMEM_LIMIT = 56 * 1024 * 1024


def _cparams(sem):
    return pltpu.CompilerParams(dimension_semantics=sem, vmem_limit_bytes=VMEM_LIMIT)


def _sigmoid(x):
    return 1.0 / (1.0 + jnp.exp(-x))


def _modulate(h, g, shift, scale):
    ms = jnp.mean(h * h, axis=-1, keepdims=True)
    y = h * lax.rsqrt(ms + NORM_EPS)
    return (y * g) * (1.0 + scale) + shift


def _mod_body(s_ref, w_ref, b_ref, o_ref):
    s = s_ref[...]
    s = s * _sigmoid(s)
    o_ref[0] = jnp.dot(s, w_ref[0], precision=HI, preferred_element_type=F32) + b_ref[0]


def _ada_mod(rows, ada_w, ada_b):
    depth, d, n6 = ada_w.shape
    r = rows.shape[0]
    tn = 1536
    return pl.pallas_call(
        _mod_body,
        grid=(depth, n6 // tn),
        in_specs=[pl.BlockSpec((r, d), lambda l, j: (0, 0)),
                  pl.BlockSpec((1, d, tn), lambda l, j: (l, 0, j)),
                  pl.BlockSpec((1, 1, tn), lambda l, j: (l, 0, j))],
        out_specs=pl.BlockSpec((1, r, tn), lambda l, j: (l, 0, j)),
        out_shape=jax.ShapeDtypeStruct((depth, r, n6), F32),
        compiler_params=_cparams(("arbitrary", "arbitrary")),
        name="ada_mod",
    )(rows, ada_w, ada_b.reshape(depth, 1, n6))


def _dot_onehot(x, onehot_bf):
    x0 = x.astype(BF16)
    r1 = x - x0.astype(F32)
    x1 = r1.astype(BF16)
    x2 = (r1 - x1.astype(F32)).astype(BF16)
    return (jnp.dot(x0, onehot_bf, preferred_element_type=F32) + jnp.dot(x1, onehot_bf, preferred_element_type=F32)
            + jnp.dot(x2, onehot_bf, preferred_element_type=F32))


def _head_norm(x, gsum, gexp, gain):
    ss = _dot_onehot(x * x, gsum)
    r = lax.rsqrt(ss * (1.0 / HEAD_DIM) + NORM_EPS)
    rb = _dot_onehot(r, gexp)
    return (x * rb) * gain


def _rope(x, cos, sin_signed):
    rows, w = x.shape
    lane = lax.broadcasted_iota(jnp.int32, (rows, LANES), 1)
    first = (lane % 32) < 16
    outs = []
    for c in range(w // LANES):
        xc = x[:, c * LANES:(c + 1) * LANES]
        partner = jnp.where(first, pltpu.roll(xc, LANES - 16, 1), pltpu.roll(xc, 16, 1))
        outs.append(xc * cos + partner * sin_signed)
    return jnp.concatenate(outs, axis=1)


def _proj_attn_body(*refs, qw, kvw, rope):
    h_ref, sh_ref, sc_ref, g_ref, w_ref, qg_ref, kg_ref, gsq_ref, geq_ref, gsk_ref, gek_ref = refs[:11]
    u = _modulate(h_ref[0], g_ref[...], sh_ref[0], sc_ref[0])
    p = jnp.dot(u.astype(BF16), w_ref[...], preferred_element_type=F32)
    q = _head_norm(p[:, :qw], gsq_ref[...], geq_ref[...], qg_ref[...]) * (HEAD_DIM ** -0.5)
    k = _head_norm(p[:, qw:qw + kvw], gsk_ref[...], gek_ref[...], kg_ref[...])
    v = p[:, qw + kvw:]
    if rope:
        cos_ref, sin_ref, qr_out, qp_out, kr_out, v_out = refs[11:]
        cos, sin = cos_ref[...], sin_ref[...]
        qr_out[0] = _rope(q, cos, sin).astype(BF16)
        qp_out[0] = q.astype(BF16)
        kr_out[0] = _rope(k, cos, sin).astype(BF16)
        v_out[0] = v.astype(BF16)
    else:
        q_out, k_out, v_out = refs[11:]
        q_out[0] = q.astype(BF16)
        k_out[0] = k.astype(BF16)
        v_out[0] = v.astype(BF16)


def _group_mats(w):
    nh = w // HEAD_DIM
    gs = np.zeros((w, LANES), np.float32)
    gs[np.arange(w), np.arange(w) // HEAD_DIM] = 1.0
    return jnp.asarray(gs, BF16), jnp.asarray(gs.T.copy(), BF16)


def _rope_tables(n):
    t = jnp.arange(n)
    row = (t // GRID_W).astype(F32)
    col = (t % GRID_W).astype(F32)
    axis_dim = HEAD_DIM // 2
    inv_freq = 1.0 / (ROPE_BASE ** (jnp.arange(0, axis_dim, 2, dtype=F32) / axis_dim))
    ang_r = row[:, None] * inv_freq
    ang_c = col[:, None] * inv_freq
    cos64 = jnp.concatenate([jnp.cos(ang_r), jnp.cos(ang_r), jnp.cos(ang_c), jnp.cos(ang_c)], axis=-1)
    sin64 = jnp.concatenate([-jnp.sin(ang_r), jnp.sin(ang_r), -jnp.sin(ang_c), jnp.sin(ang_c)], axis=-1)
    return jnp.tile(cos64, (1, 2)), jnp.tile(sin64, (1, 2))


def _proj_attn(h, shift, scale, g, w_bf, q_g, k_g, qw, kvw, rope, tm):
    b, n, d = h.shape
    nout = w_bf.shape[1]
    tm = min(tm, n)
    gsq, geq = _group_mats(qw)
    gsk, gek = _group_mats(kvw)
    qg = jnp.tile(q_g, qw // HEAD_DIM)[None]
    kg = jnp.tile(k_g, kvw // HEAD_DIM)[None]
    full = lambda a: pl.BlockSpec(a.shape, lambda bi, i: (0,) * a.ndim)
    vec = pl.BlockSpec((1, 1, d), lambda bi, i: (bi, 0, 0))
    args = [h, shift, scale, g[None], w_bf, qg, kg, gsq, geq, gsk, gek]
    in_specs = [pl.BlockSpec((1, tm, d), lambda bi, i: (bi, i, 0)), vec, vec, full(args[3]), full(w_bf),
                full(qg), full(kg), full(gsq), full(geq), full(gsk), full(gek)]
    ospec = lambda w: pl.BlockSpec((1, tm, w), lambda bi, i: (bi, i, 0))
    oshape = lambda w: jax.ShapeDtypeStruct((b, n, w), BF16)
    if rope:
        cos, sin = _rope_tables(n)
        args += [cos, sin]
        in_specs += [pl.BlockSpec((tm, LANES), lambda bi, i: (i, 0))] * 2
        out_specs = [ospec(qw), ospec(qw), ospec(kvw), ospec(kvw)]
        out_shape = [oshape(qw), oshape(qw), oshape(kvw), oshape(kvw)]
    else:
        out_specs = [ospec(qw), ospec(kvw), ospec(kvw)]
        out_shape = [oshape(qw), oshape(kvw), oshape(kvw)]
    return pl.pallas_call(
        functools.partial(_proj_attn_body, qw=qw, kvw=kvw, rope=rope),
        grid=(b, n // tm), in_specs=in_specs, out_specs=out_specs, out_shape=out_shape,
        compiler_params=_cparams(("parallel", "parallel")),
        name="proj_attn_rope" if rope else "proj_attn",
    )(*args)


HALO = 8


def _proj_hyena_body(hc_ref, hp_ref, hn_ref, sh_ref, sc_ref, g_ref, w_ref, sw_ref, sb_ref, v_out, x1_out, x2_out,
                     *, width):
    i = pl.program_id(1)
    tm = hc_ref.shape[1]
    hall = jnp.concatenate([hp_ref[0], hc_ref[0], hn_ref[0]], axis=0)
    u = _modulate(hall, g_ref[...], sh_ref[0], sc_ref[0])
    p = jnp.dot(u.astype(BF16), w_ref[...], preferred_element_type=F32)
    cur = p[HALO:HALO + tm]
    up = jnp.where(i == 0, 0.0, p[HALO - 1:HALO])
    dn = jnp.where(i == pl.num_programs(1) - 1, 0.0, p[HALO + tm:HALO + tm + 1])
    prev = jnp.concatenate([up, cur[:tm - 1]], axis=0)
    nxt = jnp.concatenate([cur[1:], dn], axis=0)
    w = sw_ref[...]
    y = prev * w[0:1] + cur * w[1:2] + nxt * w[2:3] + sb_ref[...]
    v_out[0] = y[:, :width]
    x1_out[0] = y[:, width:2 * width]
    x2_out[0] = y[:, 2 * width:]


def _proj_hyena(h, shift, scale, g, w_bf, short_w, short_b, tm):
    b, n, d = h.shape
    w3 = w_bf.shape[1]
    width = w3 // 3
    tm = min(tm, n)
    grp, ngrp = tm // HALO, n // HALO
    vec = pl.BlockSpec((1, 1, d), lambda bi, i: (bi, 0, 0))
    ospec = pl.BlockSpec((1, tm, width), lambda bi, i: (bi, i, 0))
    oshape = jax.ShapeDtypeStruct((b, n, width), F32)
    return pl.pallas_call(
        functools.partial(_proj_hyena_body, width=width),
        grid=(b, n // tm),
        in_specs=[pl.BlockSpec((1, tm, d), lambda bi, i: (bi, i, 0)),
                  pl.BlockSpec((1, HALO, d), lambda bi, i: (bi, jnp.maximum(i * grp - 1, 0), 0)),
                  pl.BlockSpec((1, HALO, d), lambda bi, i: (bi, jnp.minimum((i + 1) * grp, ngrp - 1), 0)),
                  vec, vec,
                  pl.BlockSpec((1, d), lambda bi, i: (0, 0)),
                  pl.BlockSpec((d, w3), lambda bi, i: (0, 0)),
                  pl.BlockSpec((HY_SHORT, w3), lambda bi, i: (0, 0)),
                  pl.BlockSpec((1, w3), lambda bi, i: (0, 0))],
        out_specs=[ospec, ospec, ospec], out_shape=[oshape, oshape, oshape],
        compiler_params=_cparams(("parallel", "parallel")),
        name="proj_hyena",
    )(h, h, h, shift, scale, g[None], w_bf, short_w, short_b[None])


def _local_attn_body(*refs, hkv, grp, tq, use_sink, head_bias):
    (ql_ref, qc_ref, kp_ref, kc_ref, kn_ref, vp_ref, vc_ref, vn_ref, kx_ref, vx_ref, bias_ref) = refs[:11]
    if use_sink:
        sink_ref, o_ref = refs[11:]
    else:
        (o_ref,) = refs[11:]
    dh = HEAD_DIM
    nt = (((1,), (1,)), ((), ()))
    qcols = lambda hh: [slice((hh * grp + g) * dh, (hh * grp + g + 1) * dh) for g in range(grp)]
    for h0 in range(0, hkv, HEAD_GROUP):
        heads = range(h0, min(h0 + HEAD_GROUP, hkv))
        s, vall = [], []
        for hh in heads:
            ks = slice(hh * dh, (hh + 1) * dh)
            q_l = jnp.concatenate([ql_ref[0, :, c] for c in qcols(hh)], axis=0)
            q_c = jnp.concatenate([qc_ref[0, :, c] for c in qcols(hh)], axis=0)
            kw = jnp.concatenate([kp_ref[0, :, ks], kc_ref[0, :, ks], kn_ref[0, :, ks]], axis=0)
            s_loc = lax.dot_general(q_l, kw, nt, preferred_element_type=F32)
            s_loc = s_loc + bias_ref[0, hh if head_bias else 0]
            s_ctx = lax.dot_general(q_c, kx_ref[0, :, ks], nt, preferred_element_type=F32)
            s.append(jnp.concatenate([s_loc, s_ctx], axis=1))
            v = jnp.concatenate([vp_ref[0, :, ks], vc_ref[0, :, ks], vn_ref[0, :, ks], vx_ref[0, :, ks]], axis=0)
            vall.append(jnp.concatenate([v, jnp.ones_like(v)], axis=1))
        m = [jnp.max(x, axis=-1, keepdims=True) for x in s]
        if use_sink:
            sk = [sink_ref[hh] for hh in heads]
            m = [jnp.maximum(a, b) for a, b in zip(m, sk)]
        p = [jnp.exp(x - a).astype(BF16) for x, a in zip(s, m)]
        ox = [jnp.dot(x, v, preferred_element_type=F32) for x, v in zip(p, vall)]
        den = [x[:, dh:dh + 1] for x in ox]
        if use_sink:
            den = [d + jnp.exp(b - a) for d, a, b in zip(den, m, sk)]
        o = [x[:, :dh] * (1.0 / d) for x, d in zip(ox, den)]
        for hh, oh in zip(heads, o):
            for g, c in enumerate(qcols(hh)):
                o_ref[0, :, c] = oh[g * tq:(g + 1) * tq].astype(BF16)


def _local_attn(q_loc, q_ctx, k, v, kx, vx, bias, sink_rows, hkv, grp, tq):
    b, n, qw = q_loc.shape
    kvw = k.shape[2]
    nb = n // tq
    nctx = kx.shape[1]
    head_bias = bias.shape[1] > 1
    use_sink = sink_rows is not None
    qspec = pl.BlockSpec((1, tq, qw), lambda bi, i: (bi, i, 0))
    prev = pl.BlockSpec((1, tq, kvw), lambda bi, i: (bi, jnp.maximum(i - 1, 0), 0))
    cur = pl.BlockSpec((1, tq, kvw), lambda bi, i: (bi, i, 0))
    nxt = pl.BlockSpec((1, tq, kvw), lambda bi, i: (bi, jnp.minimum(i + 1, nb - 1), 0))
    xspec = pl.BlockSpec((1, nctx, kvw), lambda bi, i: (bi, 0, 0))
    bspec = pl.BlockSpec((1,) + bias.shape[1:],
                         lambda bi, i: (jnp.where(i == 0, 0, jnp.where(i == nb - 1, 2, 1)), 0, 0, 0))
    args = [q_loc, q_ctx, k, k, k, v, v, v, kx, vx, bias]
    in_specs = [qspec, qspec, prev, cur, nxt, prev, cur, nxt, xspec, xspec, bspec]
    if use_sink:
        args.append(sink_rows)
        in_specs.append(pl.BlockSpec(sink_rows.shape, lambda bi, i: (0, 0, 0)))
    return pl.pallas_call(
        functools.partial(_local_attn_body, hkv=hkv, grp=grp, tq=tq, use_sink=use_sink, head_bias=head_bias),
        grid=(b, nb), in_specs=in_specs,
        out_specs=pl.BlockSpec((1, tq, qw), lambda bi, i: (bi, i, 0)),
        out_shape=jax.ShapeDtypeStruct((b, n, qw), BF16),
        compiler_params=_cparams(("parallel", "parallel")),
        name="local_attn_sink" if use_sink else "local_attn",
    )(*args)


def _window_bias(n):
    tq = A_BLOCK
    grp = 4
    qi = np.arange(tq)[:, None]
    kj = np.arange(3 * tq)[None, :] - tq
    band = np.abs(qi - kj) <= A_BLOCK
    nb = n // tq
    out = []
    for which in range(3):
        blk = {0: 0, 1: min(1, nb - 1), 2: nb - 1}[which]
        pos = blk * tq + kj
        ok = band & (pos >= 0) & (pos < n)
        out.append(np.tile(np.where(ok, 0.0, NEG).astype(np.float32), (grp, 1)))
    return jnp.asarray(np.stack(out)[:, None])


def _nbr_bias_body(rpb_ref, colok_ref, o_ref, *, row_ok):
    gw = GRID_W
    col_ok = colok_ref[...] > 0.5
    neg = jnp.full((gw, gw), NEG, F32)
    toep = []
    for ro in range(2 * NA_ROWS - 1):
        x = jnp.broadcast_to(rpb_ref[0, ro:ro + 1, :], (gw, LANES))
        t = pltpu.roll(x, LANES - (NA_COLS - 1), 1, stride=1, stride_axis=0)[:, :gw]
        toep.append(jnp.where(col_ok, t, NEG))
    rq = C_BLOCK // gw
    for which in range(3):
        rows_out = []
        for qa in range(rq):
            blocks = [toep[(ka - rq) - qa + (NA_ROWS - 1)] if row_ok[which][qa][ka] else neg
                      for ka in range(3 * rq)]
            rows_out.append(jnp.concatenate(blocks, axis=1))
        o_ref[which, 0] = jnp.concatenate(rows_out, axis=0)


def _neighbourhood_bias(rpb, n):
    nh = rpb.shape[0]
    rows = n // GRID_W
    kr = min(NA_ROWS, rows)
    rq = C_BLOCK // GRID_W
    nb = n // C_BLOCK
    assert nb >= 3 and rpb.shape[1:] == (2 * NA_ROWS - 1, 2 * NA_COLS - 1)
    qc = np.arange(GRID_W)
    c0 = np.clip(qc - NA_COLS // 2, 0, GRID_W - NA_COLS)
    col_ok = ((qc[None, :] >= c0[:, None]) & (qc[None, :] < c0[:, None] + NA_COLS)).astype(np.float32)
    row_ok = []
    for which in range(3):
        blk = {0: 0, 1: 1, 2: nb - 1}[which]
        qr = blk * rq + np.arange(rq)
        kr_abs = blk * rq + np.arange(3 * rq) - rq
        r0 = np.clip(qr - kr // 2, 0, rows - kr)
        ok = (kr_abs[None, :] >= r0[:, None]) & (kr_abs[None, :] < r0[:, None] + kr)
        row_ok.append(tuple(tuple(bool(v) for v in row) for row in ok))
    rpb_pad = jnp.pad(rpb.astype(F32), ((0, 0), (0, 1), (0, LANES - rpb.shape[2])))
    return pl.pallas_call(
        functools.partial(_nbr_bias_body, row_ok=tuple(row_ok)),
        grid=(nh,),
        in_specs=[pl.BlockSpec((1, 2 * NA_ROWS, LANES), lambda hh: (hh, 0, 0)),
                  pl.BlockSpec((GRID_W, GRID_W), lambda hh: (0, 0))],
        out_specs=pl.BlockSpec((3, 1, C_BLOCK, 3 * C_BLOCK), lambda hh: (0, hh, 0, 0)),
        out_shape=jax.ShapeDtypeStruct((3, nh, C_BLOCK, 3 * C_BLOCK), F32),
        compiler_params=_cparams(("parallel",)),
        name="nbr_bias",
    )(rpb_pad, jnp.asarray(col_ok))


def _ctx_attn_body(*refs, hkv, grp, use_sink):
    q_ref, k_ref, v_ref = refs[:3]
    if use_sink:
        sink_ref, o_ref = refs[3:]
    else:
        (o_ref,) = refs[3:]
    dh = HEAD_DIM
    nq = q_ref.shape[1]
    nt = (((1,), (1,)), ((), ()))
    for hh in range(hkv):
        ks = slice(hh * dh, (hh + 1) * dh)
        qcols = [slice((hh * grp + g) * dh, (hh * grp + g + 1) * dh) for g in range(grp)]
        q = jnp.concatenate([q_ref[0, :, c] for c in qcols], axis=0)
        s = lax.dot_general(q, k_ref[0, :, ks], nt, preferred_element_type=F32)
        m = jnp.max(s, axis=-1, keepdims=True)
        if use_sink:
            sk = sink_ref[hh]
            m = jnp.maximum(m, sk)
        e = jnp.exp(s - m)
        den = jnp.sum(e, axis=-1, keepdims=True)
        if use_sink:
            den = den + jnp.exp(sk - m)
        o = jnp.dot((e * (1.0 / den)).astype(BF16), v_ref[0, :, ks], preferred_element_type=F32)
        for g in range(grp):
            o_ref[0, :, qcols[g]] = o[g * nq:(g + 1) * nq].astype(BF16)


def _ctx_attn(q, k, v, sink_rows, hkv, grp):
    b, nq, qw = q.shape
    kvw = k.shape[2]
    use_sink = sink_rows is not None
    args = [q, k, v]
    in_specs = [pl.BlockSpec((1, nq, qw), lambda bi: (bi, 0, 0)),
                pl.BlockSpec((1, nq, kvw), lambda bi: (bi, 0, 0)),
                pl.BlockSpec((1, nq, kvw), lambda bi: (bi, 0, 0))]
    if use_sink:
        args.append(sink_rows)
        in_specs.append(pl.BlockSpec(sink_rows.shape, lambda bi: (0, 0, 0)))
    return pl.pallas_call(
        functools.partial(_ctx_attn_body, hkv=hkv, grp=grp, use_sink=use_sink),
        grid=(b,), in_specs=in_specs,
        out_specs=pl.BlockSpec((1, nq, qw), lambda bi: (bi, 0, 0)),
        out_shape=jax.ShapeDtypeStruct((b, nq, qw), BF16),
        compiler_params=_cparams(("parallel",)),
        name="ctx_attn_sink" if use_sink else "ctx_attn",
    )(*args)


def _outproj_body(*refs, gated, aliased):
    h1_out, u2_out, aff_out = refs[-3:]
    ins = refs[:-4] if aliased else refs[:-3]
    if gated:
        o_ref, og_ref, w_ref, h_ref, g1_ref, gn_ref, sh2_ref, sc2_ref, rwt_ref = ins
        o = o_ref[0] * og_ref[0]
    else:
        o_ref, w_ref, h_ref, g1_ref, gn_ref, sh2_ref, sc2_ref, rwt_ref = ins
        o = o_ref[0]
    y = jnp.dot(o.astype(BF16), w_ref[...], preferred_element_type=F32)
    h1 = h_ref[0] + g1_ref[0] * y
    h1_out[0] = h1
    u2 = _modulate(h1, gn_ref[...], sh2_ref[0], sc2_ref[0])
    u2_out[...] = u2.astype(BF16)
    lt = lax.dot_general(rwt_ref[...], u2, (((1,), (1,)), ((), ())), precision=HI, preferred_element_type=F32)
    e = jnp.exp(lt - jnp.max(lt, axis=0, keepdims=True))
    aff_out[0] = e / jnp.sum(e, axis=0, keepdims=True)


def _outproj(o, w_bf, h, g1, gn, sh2, sc2, rwt, tm, o_gate=None, u_rows=None, u_row0=0, u_buf=None):
    b, n, d = h.shape
    u_rows = b * n if u_rows is None else u_rows
    kin = o.shape[2]
    ne = rwt.shape[0]
    tm = min(tm, n)
    vec = pl.BlockSpec((1, 1, d), lambda bi, i: (bi, 0, 0))
    ospec = pl.BlockSpec((1, tm, kin), lambda bi, i: (bi, i, 0))
    gated = o_gate is not None
    aliased = u_buf is not None
    nblk, blk0 = n // tm, u_row0 // tm
    n_in = 9 if gated else 8
    return pl.pallas_call(
        functools.partial(_outproj_body, gated=gated, aliased=aliased),
        grid=(b, n // tm),
        in_specs=([ospec, ospec] if gated else [ospec]) + [
                  pl.BlockSpec((kin, d), lambda bi, i: (0, 0)),
                  pl.BlockSpec((1, tm, d), lambda bi, i: (bi, i, 0)),
                  vec, pl.BlockSpec((1, d), lambda bi, i: (0, 0)), vec, vec,
                  pl.BlockSpec((ne, d), lambda bi, i: (0, 0))]
                 + ([pl.BlockSpec(memory_space=pl.ANY)] if aliased else []),
        out_specs=[pl.BlockSpec((1, tm, d), lambda bi, i: (bi, i, 0)),
                   pl.BlockSpec((tm, d), lambda bi, i: (blk0 + bi * nblk + i, 0)),
                   pl.BlockSpec((1, ne, tm), lambda bi, i: (bi, 0, i))],
        out_shape=[jax.ShapeDtypeStruct((b, n, d), F32),
                   jax.ShapeDtypeStruct((u_rows, d), BF16),
                   jax.ShapeDtypeStruct((b, ne, n), F32)],
        input_output_aliases={n_in: 1} if aliased else {},
        compiler_params=_cparams(("parallel", "parallel")),
        name="outproj_router",
    )(*((o, o_gate) if gated else (o,)), w_bf, h, g1, gn[None], sh2, sc2, rwt, *((u_buf,) if aliased else ()))


SEL_BLOCK = 128


def _bcast_lane_sum(x):
    return jnp.broadcast_to(jnp.sum(x, axis=-1, keepdims=True), x.shape)


def _block_prefix(tot):
    run = jnp.zeros(tot.shape[1:], F32)
    out = []
    for j in range(tot.shape[0]):
        out.append(run)
        run = run + tot[j]
    return jnp.stack(out)


def _select_body(aff_ref, tri_ref, trib_ref, idx_ref, gate_ref, pos_ref, inc_ref, off_ref, mk_ref, *, cap, ne):
    nb = aff_ref.shape[1] // ne
    x = aff_ref[0].reshape(nb, ne, SEL_BLOCK)
    bits = pltpu.bitcast(x, jnp.int32)
    count = lambda m: _bcast_lane_sum(jnp.sum(m.astype(F32), axis=0))

    def step(i, prefix):
        cand = prefix | jnp.left_shift(jnp.int32(1), 30 - i)
        return jnp.where(count(bits >= cand[None]) >= cap, cand, prefix)

    thr = lax.fori_loop(0, 31, step, jnp.zeros((ne, SEL_BLOCK), jnp.int32))
    gt = bits > thr[None]
    eq = bits == thr[None]
    need = cap - count(gt)

    def prefix_counts(m):
        mf = m.astype(BF16).reshape(nb * ne, SEL_BLOCK)
        inc = jnp.dot(mf, tri_ref[...], preferred_element_type=F32).reshape(nb, ne, SEL_BLOCK)
        tot = jnp.broadcast_to(inc[:, :, SEL_BLOCK - 1:], inc.shape)
        return inc, tot, _block_prefix(tot)

    e_inc, _, e_off = prefix_counts(eq)
    rank_eq = e_inc + e_off - eq.astype(F32)
    mask = gt | (eq & (rank_eq < need[None]))
    inc, tot, off = prefix_counts(mask)
    inc_ref[...] = inc.reshape(nb * ne, SEL_BLOCK)
    off_ref[...] = off.reshape(nb * ne, SEL_BLOCK)
    mk_ref[...] = mask.astype(F32).reshape(nb * ne, SEL_BLOCK)
    pos_ref[0] = jnp.where(mask, inc + off - 1.0, -1.0).astype(jnp.int32).reshape(nb * ne, SEL_BLOCK)

    slot = lax.broadcasted_iota(jnp.int32, (cap, 1), 0).astype(F32)
    lane_nb = lax.broadcasted_iota(jnp.int32, (cap, nb), 1).astype(F32)
    lane = lax.broadcasted_iota(jnp.int32, (cap, SEL_BLOCK), 1).astype(F32)
    ones = jnp.ones((8, SEL_BLOCK), BF16)
    for e in range(ne):
        rows = pl.ds(e, nb, stride=ne)
        mk = mk_ref[rows, :].astype(BF16)
        cnt_row = lax.dot_general(ones, mk, (((1,), (1,)), ((), ())), preferred_element_type=F32)
        pinc_row = jnp.dot(cnt_row.astype(BF16), trib_ref[...], preferred_element_type=F32)[0:1]
        blk = jnp.sum((pinc_row <= slot).astype(F32), axis=-1, keepdims=True)
        onehot = (lane_nb == blk).astype(BF16)
        rhs = jnp.concatenate([inc_ref[rows, :], off_ref[rows, :], aff_ref[0, rows, :]], axis=1)
        r0 = rhs.astype(BF16)
        d1 = rhs - r0.astype(F32)
        r1 = d1.astype(BF16)
        r2 = (d1 - r1.astype(F32)).astype(BF16)
        got = (jnp.dot(onehot, r0, preferred_element_type=F32) + jnp.dot(onehot, r1, preferred_element_type=F32)
               + jnp.dot(onehot, r2, preferred_element_type=F32))
        rank = slot - got[:, SEL_BLOCK:SEL_BLOCK + 1]
        local = jnp.sum((got[:, :SEL_BLOCK] <= rank).astype(F32), axis=-1, keepdims=True)
        gate = jnp.sum(jnp.where(lane == local, got[:, 2 * SEL_BLOCK:], 0.0), axis=-1, keepdims=True)
        idx_ref[0, :, e:e + 1] = (blk * SEL_BLOCK + local).astype(jnp.int32)
        gate_ref[0, :, e:e + 1] = gate


def _select(aff, cap, nb_pad):
    bsz, ne, n = aff.shape
    pad = nb_pad * SEL_BLOCK - n
    a = jnp.pad(aff, ((0, 0), (0, 0), (0, pad))).reshape(bsz, ne, nb_pad, SEL_BLOCK).transpose(0, 2, 1, 3)
    a = a.reshape(bsz, nb_pad * ne, SEL_BLOCK)
    tri = jnp.asarray(np.triu(np.ones((SEL_BLOCK, SEL_BLOCK), np.float32)), BF16)
    trib = jnp.asarray(np.triu(np.ones((nb_pad, nb_pad), np.float32)), BF16)
    idx, gates, pos = pl.pallas_call(
        functools.partial(_select_body, cap=cap, ne=ne),
        grid=(bsz,),
        in_specs=[pl.BlockSpec((1, nb_pad * ne, SEL_BLOCK), lambda b: (b, 0, 0)),
                  pl.BlockSpec(tri.shape, lambda b: (0, 0)), pl.BlockSpec(trib.shape, lambda b: (0, 0))],
        out_specs=[pl.BlockSpec((1, cap, ne), lambda b: (b, 0, 0))] * 2
                  + [pl.BlockSpec((1, nb_pad * ne, SEL_BLOCK), lambda b: (b, 0, 0))],
        out_shape=[jax.ShapeDtypeStruct((bsz, cap, ne), jnp.int32), jax.ShapeDtypeStruct((bsz, cap, ne), F32),
                   jax.ShapeDtypeStruct((bsz, nb_pad * ne, SEL_BLOCK), jnp.int32)],
        scratch_shapes=[pltpu.VMEM((nb_pad * ne, SEL_BLOCK), F32)] * 3,
        compiler_params=_cparams(("parallel",)),
        name="expert_select",
    )(a, tri, trib)
    pos = pos.reshape(bsz, nb_pad, ne, SEL_BLOCK).transpose(0, 1, 3, 2).reshape(bsz, nb_pad * SEL_BLOCK, ne)
    return idx.transpose(0, 2, 1), gates.transpose(0, 2, 1), pos[:, :n]


def _expert_body(xs_ref, wg_ref, wu_ref, wd_ref, gate_ref, o_ref, acc_ref):
    f = pl.program_id(2)
    xs = xs_ref[0]
    a = jnp.dot(xs, wg_ref[...].astype(BF16), preferred_element_type=F32)
    bb = jnp.dot(xs, wu_ref[...].astype(BF16), preferred_element_type=F32)
    hid = (a * _sigmoid(a)) * bb
    y = jnp.dot(hid.astype(BF16), wd_ref[...].astype(BF16), preferred_element_type=F32)

    @pl.when(f == 0)
    def _():
        acc_ref[...] = y

    @pl.when(f > 0)
    def _():
        acc_ref[...] += y

    @pl.when(f == pl.num_programs(2) - 1)
    def _():
        o_ref[0] = (acc_ref[...] * gate_ref[0]).astype(o_ref.dtype)


def _expert_ffn(xs, gates, w_gate, w_up, w_down, layer, tr, tf):
    ne, r, d = xs.shape
    ff = w_gate.shape[3]
    return pl.pallas_call(
        _expert_body,
        grid=(ne, r // tr, ff // tf),
        in_specs=[pl.BlockSpec((1, tr, d), lambda e, c, f: (e, c, 0)),
                  pl.BlockSpec((None, None, d, tf), lambda e, c, f: (layer, e, 0, f)),
                  pl.BlockSpec((None, None, d, tf), lambda e, c, f: (layer, e, 0, f)),
                  pl.BlockSpec((None, None, tf, d), lambda e, c, f: (layer, e, f, 0)),
                  pl.BlockSpec((1, tr, 1), lambda e, c, f: (e, c, 0))],
        out_specs=pl.BlockSpec((1, tr, d), lambda e, c, f: (e, c, 0)),
        out_shape=jax.ShapeDtypeStruct((ne, r, d), BF16),
        scratch_shapes=[pltpu.VMEM((tr, d), F32)],
        compiler_params=_cparams(("parallel", "parallel", "arbitrary")),
        name="expert_ffn",
    )(xs, w_gate, w_up, w_down, gates)


COMBINE_TILE = 256
COMBINE_WIN = 128
BF16_ROWS = 16


def _combine_body(ws_ref, cnt_ref, ye_hbm, pos_ref, h_ref, g_ref, o_ref, buf, xbuf, acc_ref, sem, xsem,
                  *, ne, nt, rows_total):
    win = COMBINE_WIN
    step = pl.program_id(0) * nt + pl.program_id(1)
    nsteps = pl.num_programs(0) * nt
    slot = step % 2

    def clamp(st):
        return pl.multiple_of(jnp.minimum(st, rows_total - win), BF16_ROWS)

    def first_start(s_idx, e):
        return clamp((ws_ref[s_idx * ne + e] // BF16_ROWS) * BF16_ROWS)

    def window_copy(s_idx, sl, e):
        return pltpu.make_async_copy(ye_hbm.at[e, pl.ds(first_start(s_idx, e), win), :], buf.at[sl, e],
                                     sem.at[sl, e])

    @pl.when(step == 0)
    def _():
        for e in range(ne):
            window_copy(0, 0, e).start()

    @pl.when(step + 1 < nsteps)
    def _():
        for e in range(ne):
            window_copy(step + 1, 1 - slot, e).start()

    tm = pos_ref.shape[1]
    d = h_ref.shape[2]
    lane = lax.broadcasted_iota(jnp.int32, (tm, win), 1)
    pos = [pos_ref[0, :, e:e + 1] for e in range(ne)]
    st0 = [first_start(step, e) for e in range(ne)]
    onehot = jnp.concatenate([((pos[e] - st0[e]) == lane).astype(BF16) for e in range(ne)], axis=1)
    for e in range(ne):
        window_copy(step, slot, e).wait()
    acc_ref[...] = jnp.dot(onehot, buf[slot].reshape(ne * win, d), preferred_element_type=F32)
    nwin = [(ws_ref[step * ne + e] + cnt_ref[step * ne + e] - st0[e] + win - 1) // win for e in range(ne)]
    extra = nwin[0]
    for e in range(1, ne):
        extra = jnp.maximum(extra, nwin[e])

    @pl.when(extra > 1)
    def _():
        for e in range(ne):
            def more(k, carry, e=e):
                lo = st0[e] + k * win
                st = clamp(lo)
                cp = pltpu.make_async_copy(ye_hbm.at[e, pl.ds(st, win), :], xbuf, xsem)
                cp.start()
                cp.wait()
                oh = (((pos[e] - st) == lane) & (pos[e] >= lo)).astype(BF16)
                acc_ref[...] += jnp.dot(oh, xbuf[...], preferred_element_type=F32)
                return carry

            lax.fori_loop(1, nwin[e], more, 0)

    o_ref[0] = h_ref[0] + g_ref[0] * acc_ref[...]


def _combine(ye, pos_global, idx, row_base, h, g, ne):
    bsz, n, d = h.shape
    tm = min(COMBINE_TILE, n)
    nt = n // tm
    rows_total = ye.shape[1]
    below = jnp.sum(idx[..., None] < jnp.arange(nt + 1) * tm, axis=2).astype(jnp.int32)
    ws = (row_base[:, None, None] + below[..., :-1]).transpose(0, 2, 1)
    cnt = (below[..., 1:] - below[..., :-1]).transpose(0, 2, 1)
    grid_spec = pltpu.PrefetchScalarGridSpec(
        num_scalar_prefetch=2,
        grid=(bsz, nt),
        in_specs=[pl.BlockSpec(memory_space=pl.ANY),
                  pl.BlockSpec((1, tm, ne), lambda b, t, ws, cnt: (b, t, 0)),
                  pl.BlockSpec((1, tm, d), lambda b, t, ws, cnt: (b, t, 0)),
                  pl.BlockSpec((1, 1, d), lambda b, t, ws, cnt: (b, 0, 0))],
        out_specs=pl.BlockSpec((1, tm, d), lambda b, t, ws, cnt: (b, t, 0)),
        scratch_shapes=[pltpu.VMEM((2, ne, COMBINE_WIN, d), BF16), pltpu.VMEM((COMBINE_WIN, d), BF16),
                        pltpu.VMEM((tm, d), F32), pltpu.SemaphoreType.DMA((2, ne)), pltpu.SemaphoreType.DMA(())])
    return pl.pallas_call(
        functools.partial(_combine_body, ne=ne, nt=nt, rows_total=rows_total),
        grid_spec=grid_spec,
        out_shape=jax.ShapeDtypeStruct((bsz, n, d), F32),
        compiler_params=_cparams(("arbitrary", "arbitrary")),
        name="moe_combine",
    )(ws.reshape(-1), cnt.reshape(-1), ye, pos_global, h, g)


FFT_SLAB = 128
FFT_UNROLL = 8


def _stack_complex(m):
    return np.block([[m.real, -m.imag], [m.imag, m.real]])


def _dft_consts(n_total, n_nonzero):
    n1 = n_total // FFT_SLAB
    na = n_nonzero // FFT_SLAB
    idx1 = np.arange(n1)
    f1 = np.exp(-2j * np.pi * np.outer(idx1, idx1) / n1)
    idx2 = np.arange(FFT_SLAB)
    f2 = np.exp(-2j * np.pi * np.outer(idx2, idx2) / FFT_SLAB)
    tw = np.exp(-2j * np.pi * np.outer(idx1, idx2) / n_total)
    return dict(
        m1=_stack_complex(f1[:, :na]),
        m1_real=np.concatenate([f1.real, f1.imag], axis=0),
        m2=_stack_complex(f2),
        m2i=_stack_complex(np.conj(f2).T / n_total),
        m1i=_stack_complex(np.conj(f1).T[:na, :]),
        twr=tw.real, twi=tw.imag)


def _lane_replicated(row_ref, c, ct):
    t = jnp.broadcast_to(row_ref[pl.ds(c, 1), :], (FFT_SLAB, FFT_SLAB)).T
    return t if ct == FFT_SLAB else jnp.tile(t, (1, ct // FFT_SLAB))


def _dot_split(mh_ref, ml_ref, x):
    xh = x.astype(BF16)
    xl = (x - xh.astype(F32)).astype(BF16)
    mh = mh_ref[...]
    return (jnp.dot(mh, xh, preferred_element_type=F32) + jnp.dot(mh, xl, preferred_element_type=F32)
            + jnp.dot(ml_ref[...], xh, preferred_element_type=F32))


def _spectrum_body(k_ref, m1h_ref, m1l_ref, m2h_ref, m2l_ref, twr_ref, twi_ref, or_ref, oi_ref, *, n1):
    ct = k_ref.shape[1]

    def first(b, carry):
        a = _dot_split(m1h_ref, m1l_ref, k_ref[pl.ds(b, n1, stride=FFT_SLAB), :])
        or_ref[pl.ds(b, n1, stride=FFT_SLAB), :] = a[:n1]
        oi_ref[pl.ds(b, n1, stride=FFT_SLAB), :] = a[n1:]
        return carry

    lax.fori_loop(0, FFT_SLAB, first, 0, unroll=FFT_UNROLL)

    def slab(c, carry):
        r0 = pl.multiple_of(c * FFT_SLAB, FFT_SLAB)
        ar, ai = or_ref[pl.ds(r0, FFT_SLAB), :], oi_ref[pl.ds(r0, FFT_SLAB), :]
        tr, ti = _lane_replicated(twr_ref, c, ct), _lane_replicated(twi_ref, c, ct)
        p = jnp.concatenate([ar * tr - ai * ti, ar * ti + ai * tr], axis=0)
        x = _dot_split(m2h_ref, m2l_ref, p)
        or_ref[pl.ds(r0, FFT_SLAB), :] = x[:FFT_SLAB]
        oi_ref[pl.ds(r0, FFT_SLAB), :] = x[FFT_SLAB:]
        return carry

    lax.fori_loop(0, n1, slab, 0, unroll=FFT_UNROLL)


def _split_bf16(m):
    hi = jnp.asarray(m, F32).astype(BF16)
    lo = (jnp.asarray(m, F32) - hi.astype(F32)).astype(BF16)
    return hi, lo


def _filter_spectrum(ks, order, ct=128):
    _, n_total, ch = ks.shape
    n1 = n_total // FFT_SLAB
    cs = _dft_consts(n_total, n_total)
    consts = [*_split_bf16(cs["m1_real"]), *_split_bf16(cs["m2"]),
              jnp.asarray(cs["twr"], F32), jnp.asarray(cs["twi"], F32)]
    full = lambda a: pl.BlockSpec(a.shape, lambda j: (0,) * a.ndim)
    col = pl.BlockSpec((n_total, ct), lambda j: (0, j))
    return pl.pallas_call(
        functools.partial(_spectrum_body, n1=n1),
        grid=(ch // ct,),
        in_specs=[pl.BlockSpec((None, n_total, ct), lambda j: (order, 0, j))] + [full(a) for a in consts],
        out_specs=[col, col],
        out_shape=[jax.ShapeDtypeStruct((n_total, ch), F32)] * 2,
        compiler_params=_cparams(("parallel",)),
        name="filter_spectrum",
    )(ks, *consts)


def _fftconv_body(z_ref, kr_ref, ki_ref, bias_ref, m1_ref, m2_ref, m2i_ref, m1i_ref, twr_ref, twi_ref,
                  o_ref, wr_ref, wi_ref, *, n1, na):
    ct = z_ref.shape[2]

    def first(b, carry):
        rows = pl.ds(b, na, stride=FFT_SLAB)
        s = jnp.concatenate([z_ref[0, rows, :], z_ref[1, rows, :]], axis=0).astype(BF16)
        a = jnp.dot(m1_ref[...], s, preferred_element_type=F32)
        wr_ref[pl.ds(b, n1, stride=FFT_SLAB), :] = a[:n1]
        wi_ref[pl.ds(b, n1, stride=FFT_SLAB), :] = a[n1:]
        return carry

    lax.fori_loop(0, FFT_SLAB, first, 0, unroll=FFT_UNROLL)

    def slab(c, carry):
        r0 = pl.multiple_of(c * FFT_SLAB, FFT_SLAB)
        rows = pl.ds(r0, FFT_SLAB)
        ar, ai = wr_ref[rows, :], wi_ref[rows, :]
        tr, ti = _lane_replicated(twr_ref, c, ct), _lane_replicated(twi_ref, c, ct)
        p = jnp.concatenate([ar * tr - ai * ti, ar * ti + ai * tr], axis=0).astype(BF16)
        x = jnp.dot(m2_ref[...], p, preferred_element_type=F32)
        xr, xi = x[:FFT_SLAB], x[FFT_SLAB:]
        kr, ki = kr_ref[rows, :], ki_ref[rows, :]
        y = jnp.concatenate([xr * kr - xi * ki, xr * ki + xi * kr], axis=0).astype(BF16)
        bb = jnp.dot(m2i_ref[...], y, preferred_element_type=F32)
        br, bi = bb[:FFT_SLAB], bb[FFT_SLAB:]
        wr_ref[rows, :] = br * tr + bi * ti
        wi_ref[rows, :] = bi * tr - br * ti
        return carry

    lax.fori_loop(0, n1, slab, 0, unroll=FFT_UNROLL)

    def last(b, carry):
        rows = pl.ds(b, n1, stride=FFT_SLAB)
        s = jnp.concatenate([wr_ref[rows, :], wi_ref[rows, :]], axis=0).astype(BF16)
        y = jnp.dot(m1i_ref[...], s, preferred_element_type=F32)
        orow = pl.ds(b, na, stride=FFT_SLAB)
        bias = bias_ref[...]
        o_ref[0, orow, :] = y[:na] + z_ref[0, orow, :] * bias
        o_ref[1, orow, :] = y[na:] + z_ref[1, orow, :] * bias
        return carry

    lax.fori_loop(0, FFT_SLAB, last, 0, unroll=FFT_UNROLL)


def _fftconv(z, kf_re, kf_im, bias, ct=128):
    bsz, n, ch = z.shape
    n_total = 2 * n
    n1, na = n_total // FFT_SLAB, n // FFT_SLAB
    cs = _dft_consts(n_total, n)
    mats = [jnp.asarray(cs[name], BF16) for name in ("m1", "m2", "m2i", "m1i")]
    tws = [jnp.asarray(cs[name], F32) for name in ("twr", "twi")]
    full = lambda a: pl.BlockSpec(a.shape, lambda j, p: (0,) * a.ndim)
    single = pl.Buffered(1)
    zspec = pl.BlockSpec((2, n, ct), lambda j, p: (p, 0, j), pipeline_mode=single)
    kspec = pl.BlockSpec((n_total, ct), lambda j, p: (0, j), pipeline_mode=single)
    return pl.pallas_call(
        functools.partial(_fftconv_body, n1=n1, na=na),
        grid=(ch // ct, bsz // 2),
        in_specs=[zspec, kspec, kspec, pl.BlockSpec((1, ct), lambda j, p: (0, j))]
                 + [full(a) for a in mats] + [full(a) for a in tws],
        out_specs=pl.BlockSpec((2, n, ct), lambda j, p: (p, 0, j), pipeline_mode=single),
        out_shape=jax.ShapeDtypeStruct((bsz, n, ch), F32),
        scratch_shapes=[pltpu.VMEM((n_total, ct), F32), pltpu.VMEM((n_total, ct), F32)],
        compiler_params=_cparams(("parallel", "parallel")),
        name="fftconv",
    )(z, kf_re, kf_im, bias[None].astype(F32), *mats, *tws)


def _filter_hidden_body(z_ref, w1_ref, b1_ref, f1_ref, w2_ref, b2_ref, f2_ref, o_ref):
    z = z_ref[0]
    fd = w1_ref.shape[1]
    h = jnp.sin(f1_ref[...] * (jnp.dot(z, w1_ref[...], precision=HI, preferred_element_type=F32) + b1_ref[...]))
    h = jnp.sin(f2_ref[...] * (jnp.dot(h, w2_ref[...], precision=HI, preferred_element_type=F32) + b2_ref[...]))
    hext = jnp.concatenate([h, z[:, :LANES - fd]], axis=1)
    row = lax.broadcasted_iota(jnp.int32, hext.shape, 0)
    dead = (pl.program_id(0) == 1) & (row == 0)
    o_ref[0] = jnp.where(dead, 0.0, hext)


def _filter_taps_body(hf_ref, hb_ref, wf_ref, wb_ref, dl_ref, o_ref):
    n = hf_ref.shape[1]
    dot = lambda a, b: jnp.dot(a, b, precision=HI, preferred_element_type=F32)
    hf, hb = hf_ref[0], hb_ref[0]
    fd = LANES // 2
    fwd = dot(hf, wf_ref[...]) * jnp.exp(-(hf[:, fd:fd + 1] * dl_ref[...]))
    bwd = dot(hb, wb_ref[...]) * jnp.exp(-(hb[:, fd:fd + 1] * dl_ref[...]))
    ss = jnp.sum(fwd * fwd, axis=0, keepdims=True) + jnp.sum(bwd * bwd, axis=0, keepdims=True)
    sc = lax.rsqrt(ss + NORM_EPS)
    o_ref[0:n, :] = fwd * sc
    o_ref[n:, :] = bwd * sc


def _hyena_kernels(n, w1, b1, f1, w2, b2, f2, w3, width, ct=128):
    emb, fd = w1.shape
    assert fd == LANES // 2 and emb <= LANES
    t = jnp.linspace(0.0, 1.0, n, dtype=F32)[:, None]
    bands = (HY_EMB_DIM - 1) // 2
    w = 2.0 * math.pi * jnp.arange(n, dtype=F32)[:, None] / n
    f = jnp.linspace(1e-4, bands - 1, bands, dtype=F32)[None, :]
    z = jnp.concatenate([t, jnp.cos(f * w), -jnp.sin(f * w)], axis=-1)
    z_rev = jnp.concatenate([z[:1], z[1:][::-1]], axis=0)
    zz = jnp.pad(jnp.stack([z, z_rev]), ((0, 0), (0, 0), (0, LANES - emb)))
    w1p = jnp.pad(w1.astype(F32), ((0, LANES - emb), (0, 0)))
    full = lambda a: pl.BlockSpec(a.shape, lambda i: (0,) * a.ndim)
    row = lambda a: a.astype(F32)[None]
    small = [w1p, row(b1), row(f1), w2.astype(F32), row(b2), row(f2)]
    hx = pl.pallas_call(
        _filter_hidden_body,
        grid=(2,),
        in_specs=[pl.BlockSpec((1, n, LANES), lambda i: (i, 0, 0))] + [full(a) for a in small],
        out_specs=pl.BlockSpec((1, n, LANES), lambda i: (i, 0, 0)),
        out_shape=jax.ShapeDtypeStruct((2, n, LANES), F32),
        compiler_params=_cparams(("parallel",)),
        name="filter_hidden",
    )(zz, *small)
    max_decay = math.log(HY_TARGET) / HY_FAST_DECAY
    min_decay = math.log(HY_TARGET) / HY_SLOW_DECAY
    dl = jnp.abs(jnp.linspace(min_decay, max_decay, width, dtype=F32))[None, :]
    w3r = w3.astype(F32).reshape(fd, HY_ORDER, 2, width).transpose(1, 2, 0, 3)
    w3p = jnp.pad(w3r, ((0, 0), (0, 0), (0, LANES - fd), (0, 0)))
    single = pl.Buffered(1)
    ks = pl.pallas_call(
        _filter_taps_body,
        grid=(HY_ORDER, width // ct),
        in_specs=[pl.BlockSpec((1, n, LANES), lambda o, j: (0, 0, 0), pipeline_mode=single),
                  pl.BlockSpec((1, n, LANES), lambda o, j: (1, 0, 0), pipeline_mode=single),
                  pl.BlockSpec((None, None, LANES, ct), lambda o, j: (o, 0, 0, j)),
                  pl.BlockSpec((None, None, LANES, ct), lambda o, j: (o, 1, 0, j)),
                  pl.BlockSpec((1, ct), lambda o, j: (0, j))],
        out_specs=pl.BlockSpec((None, 2 * n, ct), lambda o, j: (o, 0, j)),
        out_shape=jax.ShapeDtypeStruct((HY_ORDER, 2 * n, width), F32),
        compiler_params=_cparams(("parallel", "parallel")),
        name="filter_taps",
    )(hx, hx, w3p, w3p, dl)
    return ks


def _small_conv_body(z_ref, k_ref, bias_ref, mfk_ref, mf_ref, mi_ref, o_ref):
    bsz, n, _ = z_ref.shape
    nt = 2 * n
    kf = jnp.dot(mfk_ref[...], k_ref[...], precision=HI, preferred_element_type=F32)
    kr, ki = kf[:nt], kf[nt:]
    bias = bias_ref[...]
    for p in range(bsz // 2):
        z0, z1 = z_ref[2 * p], z_ref[2 * p + 1]
        x = jnp.dot(mf_ref[...], jnp.concatenate([z0, z1], axis=0).astype(BF16), preferred_element_type=F32)
        xr, xi = x[:nt], x[nt:]
        y = jnp.concatenate([xr * kr - xi * ki, xr * ki + xi * kr], axis=0).astype(BF16)
        w = jnp.dot(mi_ref[...], y, preferred_element_type=F32)
        o_ref[2 * p] = w[:n] + z0 * bias
        o_ref[2 * p + 1] = w[n:] + z1 * bias


def _small_conv(z, k, bias, ct=256):
    bsz, n, ch = z.shape
    nt = 2 * n
    idx = np.arange(nt)
    f = np.exp(-2j * np.pi * np.outer(idx, idx) / nt)
    mfk = jnp.asarray(np.concatenate([f.real, f.imag], axis=0), F32)
    mf = jnp.asarray(_stack_complex(f[:, :n]), BF16)
    mi = jnp.asarray(_stack_complex(np.conj(f).T[:n, :] / nt), BF16)
    full = lambda a: pl.BlockSpec(a.shape, lambda j: (0,) * a.ndim)
    return pl.pallas_call(
        _small_conv_body,
        grid=(ch // ct,),
        in_specs=[pl.BlockSpec((bsz, n, ct), lambda j: (0, 0, j)), pl.BlockSpec((nt, ct), lambda j: (0, j)),
                  pl.BlockSpec((1, ct), lambda j: (0, j)), full(mfk), full(mf), full(mi)],
        out_specs=pl.BlockSpec((bsz, n, ct), lambda j: (0, 0, j)),
        out_shape=jax.ShapeDtypeStruct((bsz, n, ch), F32),
        compiler_params=_cparams(("parallel",)),
        name="small_conv",
    )(z, k, bias[None].astype(F32), mfk, mf, mi)


def _mul_body(a_ref, b_ref, o_ref):
    o_ref[...] = a_ref[...] * b_ref[...]


def _mul(a, b, tm):
    bsz, n, d = a.shape
    tm = min(tm, n)
    spec = pl.BlockSpec((1, tm, d), lambda bi, i: (bi, i, 0))
    return pl.pallas_call(
        _mul_body, grid=(bsz, n // tm), in_specs=[spec, spec], out_specs=spec,
        out_shape=jax.ShapeDtypeStruct(a.shape, a.dtype),
        compiler_params=_cparams(("parallel", "parallel")), name="gate_mul",
    )(a, b)


def _hyena_core(h, shift, scale, g, w_in_bf, short_w, short_b, w1, b1, f1, w2, b2, f2, w3, fbias, tm):
    n = h.shape[1]
    width = w_in_bf.shape[1] // 3
    v, x1, x2 = _proj_hyena(h, shift, scale, g, w_in_bf, short_w, short_b, tm)
    ks = _hyena_kernels(n, w1, b1, f1, w2, b2, f2, w3, width)
    if 2 * n // FFT_SLAB >= FFT_SLAB:
        conv = lambda z, o: _fftconv(z, *_filter_spectrum(ks, o), fbias[o])
    else:
        conv = lambda z, o: _small_conv(z, ks[o], fbias[o])
    w0 = conv(v, 0)
    w1_ = conv(_mul(x1, w0, tm), 1)
    return w1_, x2


def _sink_rows(sink, hkv, grp, tq):
    return jnp.repeat(sink.astype(F32).reshape(hkv, grp), tq, axis=1)[..., None]


def kernel(x, c, ctx, c_ctx, ada_w, ada_b, norm_mix_g, norm_ffn_g, router_w, exp_w_gate, exp_w_up, exp_w_down,
           a_w_in, a_w_out, a_q_g, a_k_g, a_sink, b_w_in, b_short_w, b_short_b, b_w1, b_b1, b_f1, b_w2, b_b2,
           b_f2, b_w3, b_bias, b_w_out, c_w_in, c_w_out, c_q_g, c_k_g, c_rpb):
    bsz, n, d = x.shape
    nctx = ctx.shape[1]
    depth = ada_w.shape[0]
    ne = router_w.shape[2]
    cap = EC_CAPACITY * n // ne
    cap_c = EC_CAPACITY * nctx // ne
    tm = 256

    pad_rows = (-(bsz + 1)) % 8
    rows = jnp.concatenate([c, c_ctx[None], jnp.zeros((pad_rows, d), F32)], axis=0)
    mod_all = _ada_mod(rows, ada_w, ada_b)

    h, hc = x, ctx
    for i in range(depth):
        last = i == depth - 1
        kind, j = i % N_MIXERS, i // N_MIXERS
        mod = mod_all[i, :bsz].reshape(bsz, 1, 6, d)
        sh1, sc1, g1, sh2, sc2, g2 = [mod[:, :, t] for t in range(6)]
        modc = jnp.broadcast_to(mod_all[i, bsz].reshape(1, 1, 6, d), (bsz, 1, 6, d))
        csh1, csc1, cg1, csh2, csc2, cg2 = [modc[:, :, t] for t in range(6)]
        yc = o_gate = yc_gate = None
        if kind == 0:
            hkv, grp = A_KV_HEADS, a_w_in.shape[2] // HEAD_DIM // A_KV_HEADS - 2
            qw, kvw = hkv * grp * HEAD_DIM, hkv * HEAD_DIM
            w_in = a_w_in[j].astype(BF16)
            w_out = a_w_out[j].astype(BF16)
            q_rot, q_pl, k_rot, v = _proj_attn(h, sh1, sc1, norm_mix_g[i], w_in, a_q_g[j], a_k_g[j],
                                               qw, kvw, True, tm)
            qc, kc, vc = _proj_attn(hc, csh1, csc1, norm_mix_g[i], w_in, a_q_g[j], a_k_g[j],
                                    qw, kvw, False, tm)
            o = _local_attn(q_rot, q_pl, k_rot, v, kc, vc, _window_bias(n),
                            _sink_rows(a_sink[j], hkv, grp, A_BLOCK), hkv, grp, A_BLOCK)
            if not last:
                yc = _ctx_attn(qc, kc, vc, _sink_rows(a_sink[j], hkv, grp, nctx), hkv, grp)
        elif kind == 1:
            w_in = b_w_in[j].astype(BF16)
            w_out = b_w_out[j].astype(BF16)
            hy = (b_short_w[j], b_short_b[j], b_w1[j], b_b1[j], b_f1[j], b_w2[j], b_b2[j], b_f2[j], b_w3[j],
                  b_bias[j])
            o, o_gate = _hyena_core(h, sh1, sc1, norm_mix_g[i], w_in, *hy, tm)
            if not last:
                yc, yc_gate = _hyena_core(hc, csh1, csc1, norm_mix_g[i], w_in, *hy, tm)
        else:
            nh = c_w_in.shape[2] // HEAD_DIM // 3
            hw = nh * HEAD_DIM
            w_in = c_w_in[j].astype(BF16)
            w_out = c_w_out[j].astype(BF16)
            q, k, v = _proj_attn(h, sh1, sc1, norm_mix_g[i], w_in, c_q_g[j], c_k_g[j], hw, hw, False, tm)
            qc, kc, vc = _proj_attn(hc, csh1, csc1, norm_mix_g[i], w_in, c_q_g[j], c_k_g[j], hw, hw, False, tm)
            o = _local_attn(q, q, k, v, kc, vc, _neighbourhood_bias(c_rpb[j], n), None, nh, 1, C_BLOCK)
            if not last:
                yc = _ctx_attn(qc, kc, vc, None, nh, 1)

        rwt = router_w[i].T
        u_rows = bsz * n if last else bsz * (n + nctx)
        u_init = None if last else jnp.zeros((u_rows, d), BF16)
        h1, u_all, aff = _outproj(o, w_out, h, g1, norm_ffn_g[i], sh2, sc2, rwt, 2 * tm, o_gate, u_rows=u_rows,
                                  u_buf=u_init)
        il, gl, pos_l = _select(aff, cap, n // SEL_BLOCK)
        base_l = jnp.arange(bsz) * cap
        rows_all = (il + (jnp.arange(bsz) * n)[:, None, None]).transpose(1, 0, 2).reshape(ne, bsz * cap)
        gates_all = gl.transpose(1, 0, 2).reshape(ne, bsz * cap)
        if not last:
            hc1, u_all, affc = _outproj(yc, w_out, hc, cg1, norm_ffn_g[i], csh2, csc2, rwt, 2 * tm, yc_gate,
                                        u_rows=u_rows, u_row0=bsz * n, u_buf=u_all)
            icx, gcx, pos_c = _select(affc, cap_c, n // SEL_BLOCK)
            rows_c = icx + (bsz * n + jnp.arange(bsz) * nctx)[:, None, None]
            rows_all = jnp.concatenate([rows_all, rows_c.transpose(1, 0, 2).reshape(ne, bsz * cap_c)], axis=1)
            gates_all = jnp.concatenate([gates_all, gcx.transpose(1, 0, 2).reshape(ne, bsz * cap_c)], axis=1)
        r = rows_all.shape[1]
        xs = u_all[rows_all]
        ye = _expert_ffn(xs, gates_all[..., None], exp_w_gate, exp_w_up, exp_w_down, i, r // 4, 1024)
        h = _combine(ye, jnp.where(pos_l >= 0, pos_l + base_l[:, None, None], -1), il, base_l, h1, g2, ne)
        if not last:
            base_c = bsz * cap + jnp.arange(bsz) * cap_c
            hc = _combine(ye, jnp.where(pos_c >= 0, pos_c + base_c[:, None, None], -1), icx, base_c, hc1, cg2, ne)
    return h
```
